```python
import math
import jax, jax.numpy as jnp
from jax import lax
import numpy as np

D_MODEL = 1024
BATCH = 8
SEQ = 2048
DEPTH = 2
DEC_BATCH = 32
DEC_SEQ = 1
PAST_LEN = 8192
PAGE_SIZE = 128

HEAD_DIM = 64
MIX_WIDTH = D_MODEL
MEM_HEADS = 4
MEM_LEN = 256
MEMQ = MEM_HEADS * HEAD_DIM
TOK_HEADS = (MIX_WIDTH - MEMQ) // HEAD_DIM
QA = TOK_HEADS * HEAD_DIM
NSA_KV_GROUPS = 3
NSA_HPG = TOK_HEADS // NSA_KV_GROUPS
KVW = NSA_KV_GROUPS * HEAD_DIM
CMP_LEN = 32
CMP_STRIDE = 16
CMP_RATIO = CMP_LEN // CMP_STRIDE
CMP_HID = 2 * HEAD_DIM
SEL_LEN = 64
SEL_TOPK = 16
WINDOW = 512
Q_BLOCK = 128
GATE_W = 3 * TOK_HEADS
NSA_COLS = (QA, KVW, KVW, KVW, KVW, KVW, KVW, GATE_W, MEMQ)
C_A = sum(NSA_COLS)
RWKV_HEADS = TOK_HEADS
RWKV_WIDTH = RWKV_HEADS * HEAD_DIM
DECAY_RANK = 64
ICLR_RANK = 64
GATE_RANK = 128
RWKV_GN_EPS = 64e-5
RWKV_COLS = (RWKV_WIDTH, RWKV_WIDTH, RWKV_WIDTH, DECAY_RANK, ICLR_RANK, GATE_RANK)
SHIFT_W = sum(RWKV_COLS)
B_COLS = (SHIFT_W, MEMQ)
C_B = sum(B_COLS)
N_GROUPS = 4
E_PER_GROUP = 8
N_EXPERTS = N_GROUPS * E_PER_GROUP
TOPK_FINE = 2
D_EXPERT = D_MODEL // 2
MOE_BLOCK = 64
N_A = (DEPTH + 1) // 2
N_B = DEPTH // 2
NEG = -1e30
BIG = 1e30

kernel_name = 'hybrid_nsa_rwkv7_hmoe_decode_step'


def rms_norm(x, g, eps=1e-6):
    xf = x.astype(jnp.float32)
    return xf * lax.rsqrt(jnp.mean(xf * xf, axis=-1, keepdims=True) + eps) * g


def masked_softmax(s, mask):
    s = jnp.where(mask, s.astype(jnp.float32), NEG)
    m = jnp.max(s, axis=-1, keepdims=True)
    e = jnp.where(mask, jnp.exp(s - m), 0.0)
    return e / jnp.maximum(jnp.sum(e, axis=-1, keepdims=True), 1e-30)


def alibi_slopes(n):
    return jnp.exp2(-8.0 * jnp.arange(1, n + 1, dtype=jnp.float32) / n)


def split_cols(p, widths):
    offs = np.cumsum((0,) + tuple(widths))
    return [p[..., int(offs[i]):int(offs[i + 1])] for i in range(len(widths))]


def nsa_project(h, w_in, q_gain, k_gain):
    B, T, _ = h.shape
    q, kc, vc, ks, vs, kw, vw, g, mq = split_cols(h @ w_in, NSA_COLS)
    hd = lambda z, n: z.reshape(B, T, n, HEAD_DIM)
    G = NSA_KV_GROUPS
    return (rms_norm(hd(q, TOK_HEADS), q_gain), hd(kc, G), hd(vc, G),
            rms_norm(hd(ks, G), k_gain[1]), hd(vs, G),
            rms_norm(hd(kw, G), k_gain[2]), hd(vw, G),
            jax.nn.sigmoid(g.astype(jnp.float32)).reshape(B, T, TOK_HEADS, 3), mq)


def nsa_compress(rows, pe, w1, b1, w2):
    B, L, G, Dh = rows.shape
    n_chunk = L // CMP_STRIDE
    n_cmp = n_chunk - CMP_RATIO + 1
    chunks = rows[:, :n_chunk * CMP_STRIDE].reshape(B, n_chunk, CMP_STRIDE, G, Dh)
    pe_r = pe.reshape(CMP_RATIO, CMP_STRIDE, 1, Dh)
    w1_r = w1.reshape(CMP_RATIO, CMP_STRIDE, Dh, CMP_HID)
    hid = b1
    for r in range(CMP_RATIO):
        hid = hid + jnp.einsum('bnlgd,ldf->bngf', chunks[:, r:r + n_cmp] + pe_r[r], w1_r[r])
    return jnp.einsum('bngf,fd->bngd', jax.nn.gelu(hid), w2)


def nsa_summaries(kc, vc, pe, w1, b1, w2, k_gain):
    ck = rms_norm(nsa_compress(kc, pe[0], w1[0], b1[0], w2[0]), k_gain)
    cv = nsa_compress(vc, pe[1], w1[1], b1[1], w2[1])
    c_end = jnp.arange(ck.shape[1]) * CMP_STRIDE + CMP_LEN - 1
    return ck, cv, c_end


def sel_blocks(rows):
    B, L, G, Dh = rows.shape
    ns = -(-L // SEL_LEN)
    rows = jnp.pad(rows, ((0, 0), (0, ns * SEL_LEN - L), (0, 0), (0, 0)))
    return rows.reshape(B, ns, SEL_LEN, G, Dh).transpose(0, 3, 1, 2, 4)


def cmp_sel_map(n_cmp, n_sel):
    c0 = jnp.arange(n_cmp) * CMP_STRIDE
    s0 = jnp.arange(n_sel) * SEL_LEN
    return ((c0[:, None] < s0[None, :] + SEL_LEN) & (c0[:, None] + CMP_LEN > s0[None, :])).astype(jnp.float32)


def nsa_block(q, gates, q_pos, ck, cv, c_end, ks_blk, vs_blk, kw, vw, kw_pos, cmp_to_sel):
    B, T, H, Dh = q.shape
    G = NSA_KV_GROUPS
    qg = (q.reshape(B, T, G, NSA_HPG, Dh) * (Dh ** -0.5)).astype(jnp.float32)
    slope = alibi_slopes(H).reshape(G, NSA_HPG)[None, :, None, :]
    t = q_pos
    s_c = jnp.einsum('btgjd,bngd->bgtjn', qg, ck)
    dist_c = t[:, None] - c_end[None, :]
    s_c = s_c - slope[..., None] * dist_c.astype(jnp.float32)[None, None, :, None, :]
    p_c = masked_softmax(s_c, (dist_c >= 0)[None, None, :, None, :])
    o_c = jnp.einsum('bgtjn,bngd->btgjd', p_c, cv)
    n_sel = ks_blk.shape[2]
    k_top = min(SEL_TOPK, n_sel)
    imp = jnp.einsum('bgtjn,ns->bgts', p_c, cmp_to_sel)
    blk = jnp.arange(n_sel)
    cur = (t // SEL_LEN)[:, None]
    forced = (blk == 0) | (blk == cur) | (blk == cur - 1)
    valid = blk * SEL_LEN <= t[:, None]
    score = jnp.where(valid, jnp.where(forced, BIG, imp), -BIG)
    _, idx = lax.top_k(score, k_top)
    bi = jnp.arange(B)[:, None, None, None]
    gi = jnp.arange(G)[None, :, None, None]
    k_sel = ks_blk[bi, gi, idx]
    v_sel = vs_blk[bi, gi, idx]
    s_s = jnp.einsum('btgjd,bgtnkd->bgtjnk', qg, k_sel)
    pos_s = idx[..., None] * SEL_LEN + jnp.arange(SEL_LEN)
    dist_s = t[None, None, :, None, None] - pos_s
    s_s = s_s - slope[..., None, None] * dist_s.astype(jnp.float32)[:, :, :, None]
    mask_s = (dist_s >= 0)[:, :, :, None]
    p_s = masked_softmax(s_s.reshape(B, G, T, NSA_HPG, -1), mask_s.reshape(B, G, T, 1, -1))
    o_s = jnp.einsum('bgtjnk,bgtnkd->btgjd', p_s.reshape(s_s.shape), v_sel)
    s_w = jnp.einsum('btgjd,bsgd->bgtjs', qg, kw)
    dist_w = t[:, None] - kw_pos[None, :]
    mask_w = (dist_w >= 0) & (dist_w <= WINDOW) & (kw_pos[None, :] >= 0)
    s_w = s_w - slope[..., None] * dist_w.astype(jnp.float32)[None, None, :, None, :]
    p_w = masked_softmax(s_w, mask_w[None, None, :, None, :])
    o_w = jnp.einsum('bgtjs,bsgd->btgjd', p_w, vw)
    gt = gates.reshape(B, T, G, NSA_HPG, 3)
    o = gt[..., 0:1] * o_c + gt[..., 1:2] * o_s + gt[..., 2:3] * o_w
    return o.reshape(B, T, H * Dh)


def nsa_prompt(q, kc, vc, ks, vs, kw, vw, gates, cmp_prm):
    B, S, H, Dh = q.shape
    ck, cv, c_end = nsa_summaries(kc, vc, *cmp_prm)
    ks_blk, vs_blk = sel_blocks(ks), sel_blocks(vs)
    cmp_to_sel = cmp_sel_map(ck.shape[1], ks_blk.shape[2])
    pad = ((0, 0), (WINDOW, 0), (0, 0), (0, 0))
    kw_pad, vw_pad = jnp.pad(kw, pad), jnp.pad(vw, pad)
    nqb = S // Q_BLOCK
    qb = q.reshape(B, nqb, Q_BLOCK, H, Dh).swapaxes(0, 1)
    gb = gates.reshape(B, nqb, Q_BLOCK, H, 3).swapaxes(0, 1)
    starts = jnp.arange(nqb) * Q_BLOCK

    def one(args):
        qi, gi, q0 = args
        kwi = lax.dynamic_slice_in_dim(kw_pad, q0, WINDOW + Q_BLOCK, axis=1)
        vwi = lax.dynamic_slice_in_dim(vw_pad, q0, WINDOW + Q_BLOCK, axis=1)
        kw_pos = q0 - WINDOW + jnp.arange(WINDOW + Q_BLOCK)
        return nsa_block(qi, gi, q0 + jnp.arange(Q_BLOCK), ck, cv, c_end,
                         ks_blk, vs_blk, kwi, vwi, kw_pos, cmp_to_sel)

    out = lax.map(one, (qb, gb, starts))
    return out.swapaxes(0, 1).reshape(B, S, H * Dh)


def nsa_sample(q, kc, vc, ks, vs, kw, vw, gates, pool_kc, pool_vc, pool_ks, pool_vs,
               win_k, win_v, page_table, cmp_prm):
    Bd, T = q.shape[:2]
    past = page_table.shape[1] * PAGE_SIZE

    def full(pool, new):
        old = pool[page_table].reshape(Bd, past, NSA_KV_GROUPS, HEAD_DIM)
        return jnp.concatenate([old, new], axis=1)

    ck, cv, c_end = nsa_summaries(full(pool_kc, kc), full(pool_vc, vc), *cmp_prm)
    ks_blk, vs_blk = sel_blocks(full(pool_ks, ks)), sel_blocks(full(pool_vs, vs))
    cmp_to_sel = cmp_sel_map(ck.shape[1], ks_blk.shape[2])
    wb = win_k.shape[1]
    kw_all = jnp.concatenate([win_k, kw], axis=1)
    vw_all = jnp.concatenate([win_v, vw], axis=1)
    kw_pos = past - wb + jnp.arange(wb + T)
    out = nsa_block(q, gates, past + jnp.arange(T), ck, cv, c_end, ks_blk, vs_blk,
                    kw_all, vw_all, kw_pos, cmp_to_sel)
    return out, kw_all[:, T:], vw_all[:, T:]


def rwkv_mix(Pr, shift_prev, S0, mu, w0, w_up, a0, a_up, g_up, k_k, k_a, r_k, ln_w, ln_b):
    B, T, _ = Pr.shape
    H, N = RWKV_HEADS, HEAD_DIM
    prev = jnp.concatenate([shift_prev[:, None, :].astype(Pr.dtype), Pr[:, :-1]], axis=1)
    X = Pr + mu * (prev - Pr)
    r, k, v, xw, xa, xg = split_cols(X, RWKV_COLS)
    w = -jax.nn.softplus(-(w0 + jnp.tanh(xw) @ w_up)) - 0.5
    a = jax.nn.sigmoid(a0 + xa @ a_up)
    g = jax.nn.sigmoid(xg) @ g_up
    heads = lambda z: z.astype(jnp.float32).reshape(B, T, H, N)
    kk = heads(k * k_k)
    kk = kk / jnp.maximum(jnp.linalg.norm(kk, axis=-1, keepdims=True), 1e-12)
    k = k * (1.0 + (a - 1.0) * k_a)
    r, k, v, a = heads(r), heads(k), heads(v), heads(a)
    decay = jnp.exp(-jnp.exp(heads(w)))

    def step(S, inp):
        r_t, d_t, k_t, v_t, kk_t, a_t = inp
        sa = jnp.einsum('bhvk,bhk->bhv', S, -kk_t)
        S = (S * d_t[:, :, None, :] + sa[..., None] * (kk_t * a_t)[:, :, None, :]
             + v_t[..., None] * k_t[:, :, None, :])
        return S, jnp.einsum('bhvk,bhk->bhv', S, r_t)

    xs = tuple(z.swapaxes(0, 1) for z in (r, decay, k, v, kk, a))
    S_T, y = lax.scan(step, S0.astype(jnp.float32), xs)
    y = y.swapaxes(0, 1)
    m = jnp.mean(y, axis=-1, keepdims=True)
    var = jnp.mean(jnp.square(y - m), axis=-1, keepdims=True)
    y = ((y - m) * lax.rsqrt(var + RWKV_GN_EPS)).reshape(B, T, H * N) * ln_w + ln_b
    bonus = (jnp.sum(r * k * r_k, axis=-1, keepdims=True) * v).reshape(B, T, H * N)
    return (y + bonus) * g, Pr[:, -1], S_T


def mem_kv(mem, g_norm, w_kv, g_k):
    B, M, _ = mem.shape
    kv = rms_norm(mem, g_norm) @ w_kv
    k = kv[..., :MEMQ].reshape(B, M, MEM_HEADS, HEAD_DIM)
    v = kv[..., MEMQ:].reshape(B, M, MEM_HEADS, HEAD_DIM)
    return rms_norm(k, g_k), v


def mem_attend(mq, km, vm, q_gain):
    B, T, _ = mq.shape
    q = rms_norm(mq.reshape(B, T, MEM_HEADS, HEAD_DIM), q_gain) * (HEAD_DIM ** -0.5)
    p = jax.nn.softmax(jnp.einsum('bthd,bmhd->bhtm', q, km).astype(jnp.float32), axis=-1)
    return jnp.einsum('bhtm,bmhd->bthd', p, vm).reshape(B, T, MEMQ)


def swiglu(x, wg, wu, wd):
    return (jax.nn.silu(x @ wg) * (x @ wu)) @ wd


def moe_dispatch(h, eidx, ew, w_gate, w_up, w_down):
    N, D = h.shape
    A = N * TOPK_FINE
    flat_e = eidx.reshape(A)
    order = jnp.argsort(flat_e)
    e_sorted = flat_e[order]
    tok_sorted = order // TOPK_FINE
    w_sorted = ew.reshape(A)[order]
    counts = jnp.bincount(flat_e, length=N_EXPERTS)
    padded = (counts + MOE_BLOCK - 1) // MOE_BLOCK * MOE_BLOCK
    pad_end = jnp.cumsum(padded)
    pad_start = pad_end - padded
    start = jnp.cumsum(counts) - counts
    dest = pad_start[e_sorted] + jnp.arange(A) - start[e_sorted]
    n_blocks = (A + N_EXPERTS * (MOE_BLOCK - 1)) // MOE_BLOCK
    buf = jnp.zeros((n_blocks * MOE_BLOCK, D), h.dtype).at[dest].set(h[tok_sorted])
    block_e = jnp.minimum(jnp.searchsorted(pad_end, jnp.arange(n_blocks) * MOE_BLOCK, side='right'),
                          N_EXPERTS - 1)

    def run(args):
        xb, e = args
        return swiglu(xb, w_gate[e], w_up[e], w_down[e])

    out = lax.map(run, (buf.reshape(n_blocks, MOE_BLOCK, D), block_e)).reshape(-1, D)
    y = out[dest] * w_sorted[:, None]
    return jax.ops.segment_sum(y, tok_sorted, num_segments=N)


def hier_moe(h, w_coarse, b_coarse, w_fine, b_fine, w_gate, w_up, w_down):
    N = h.shape[0]
    hf = h.astype(jnp.float32)
    lg = hf @ w_coarse + b_coarse
    grp = jnp.argmax(lg, axis=-1)
    p_grp = jnp.take_along_axis(jax.nn.softmax(lg, axis=-1), grp[:, None], axis=-1)
    lf = (hf @ w_fine + b_fine).reshape(N, N_GROUPS, E_PER_GROUP)
    lf = jnp.take_along_axis(lf, grp[:, None, None], axis=1)[:, 0]
    top_v, top_i = lax.top_k(lf, TOPK_FINE)
    ew = jax.nn.softmax(top_v, axis=-1) * p_grp
    eidx = (grp[:, None] * E_PER_GROUP + top_i).astype(jnp.int32)
    return moe_dispatch(h, eidx, ew, w_gate, w_up, w_down)


def setup_inputs(seed: int = 0) -> dict:
    key = jax.random.key(seed)
    keys = iter(jax.random.split(key, 64))

    def nrm(shape, scale=1.0):
        return jax.random.normal(next(keys), shape, jnp.float32) * scale

    def gain(shape):
        return 1.0 + nrm(shape, 0.02)

    D, G, Dh = D_MODEL, NSA_KV_GROUPS, HEAD_DIM
    n_pages = PAST_LEN // PAGE_SIZE
    n_used = DEC_BATCH * n_pages
    n_pool = n_used + (n_used + 3) // 4
    w_buf = min(WINDOW, PAST_LEN)
    page_table = jax.random.permutation(next(keys), n_pool)[:n_used].reshape(DEC_BATCH, n_pages).astype(jnp.int32)
    pool = (N_A, n_pool, PAGE_SIZE, G, Dh)
    return {
        'x_prompt': nrm((BATCH, SEQ, D)),
        'x_sample': nrm((DEC_BATCH, DEC_SEQ, D)),
        'cache_cmp_k': nrm(pool),
        'cache_cmp_v': nrm(pool),
        'cache_sel_k': nrm(pool),
        'cache_sel_v': nrm(pool),
        'cache_win_k': nrm((N_A, DEC_BATCH, w_buf, G, Dh)),
        'cache_win_v': nrm((N_A, DEC_BATCH, w_buf, G, Dh)),
        'cache_mem_k': nrm((DEPTH, DEC_BATCH, MEM_LEN, MEM_HEADS, Dh)),
        'cache_mem_v': nrm((DEPTH, DEC_BATCH, MEM_LEN, MEM_HEADS, Dh)),
        'state_rwkv': nrm((N_B, DEC_BATCH, RWKV_HEADS, Dh, Dh), 0.1),
        'state_shift': nrm((N_B, DEC_BATCH, SHIFT_W)),
        'page_table': page_table,
        'mem_prompt': nrm((BATCH, MEM_LEN, D)),
        'norm_mix': gain((DEPTH, D)),
        'norm_ffn': gain((DEPTH, D)),
        'norm_mem': gain((DEPTH, D)),
        'w_mem_kv': nrm((DEPTH, D, 2 * MEMQ), D ** -0.5),
        'mem_q_gain': gain((DEPTH, Dh)),
        'mem_k_gain': gain((DEPTH, Dh)),
        'w_in_a': nrm((N_A, D, C_A), D ** -0.5),
        'nsa_q_gain': gain((N_A, Dh)),
        'nsa_k_gain': gain((N_A, 3, Dh)),
        'cmp_pe': nrm((N_A, 2, CMP_LEN, Dh), 0.1),
        'cmp_w1': nrm((N_A, 2, CMP_LEN * Dh, CMP_HID), (CMP_LEN * Dh) ** -0.5),
        'cmp_b1': nrm((N_A, 2, CMP_HID), 0.01),
        'cmp_w2': nrm((N_A, 2, CMP_HID, Dh), CMP_HID ** -0.5),
        'w_in_b': nrm((N_B, D, C_B), D ** -0.5),
        'rwkv_mu': jax.random.uniform(next(keys), (N_B, SHIFT_W), jnp.float32),
        'rwkv_w0': -2.0 + nrm((N_B, RWKV_WIDTH), 0.5),
        'rwkv_w_up': nrm((N_B, DECAY_RANK, RWKV_WIDTH), 0.5 * DECAY_RANK ** -0.5),
        'rwkv_a0': nrm((N_B, RWKV_WIDTH), 0.1),
        'rwkv_a_up': nrm((N_B, ICLR_RANK, RWKV_WIDTH), 0.5 * ICLR_RANK ** -0.5),
        'rwkv_g_up': nrm((N_B, GATE_RANK, RWKV_WIDTH), GATE_RANK ** -0.5),
        'rwkv_k_k': 0.85 + nrm((N_B, RWKV_WIDTH), 0.05),
        'rwkv_k_a': 1.0 + nrm((N_B, RWKV_WIDTH), 0.05),
        'rwkv_r_k': nrm((N_B, RWKV_HEADS, Dh), 0.1),
        'rwkv_ln_w': gain((N_B, RWKV_WIDTH)),
        'rwkv_ln_b': nrm((N_B, RWKV_WIDTH), 0.01),
        'w_out': nrm((DEPTH, MIX_WIDTH, D), MIX_WIDTH ** -0.5),
        'moe_w_coarse': nrm((DEPTH, D, N_GROUPS), D ** -0.5),
        'moe_b_coarse': nrm((DEPTH, N_GROUPS), 0.01),
        'moe_w_fine': nrm((DEPTH, D, N_EXPERTS), D ** -0.5),
        'moe_b_fine': nrm((DEPTH, N_EXPERTS), 0.01),
        'moe_w_gate': nrm((DEPTH, N_EXPERTS, D, D_EXPERT), D ** -0.5),
        'moe_w_up': nrm((DEPTH, N_EXPERTS, D, D_EXPERT), D ** -0.5),
        'moe_w_down': nrm((DEPTH, N_EXPERTS, D_EXPERT, D), D_EXPERT ** -0.5),
    }


def reference(x_prompt, x_sample, cache_cmp_k, cache_cmp_v, cache_sel_k, cache_sel_v,
              cache_win_k, cache_win_v, cache_mem_k, cache_mem_v, state_rwkv, state_shift,
              page_table, mem_prompt,
              norm_mix, norm_ffn, norm_mem, w_mem_kv, mem_q_gain, mem_k_gain,
              w_in_a, nsa_q_gain, nsa_k_gain, cmp_pe, cmp_w1, cmp_b1, cmp_w2,
              w_in_b, rwkv_mu, rwkv_w0, rwkv_w_up, rwkv_a0, rwkv_a_up, rwkv_g_up,
              rwkv_k_k, rwkv_k_a, rwkv_r_k, rwkv_ln_w, rwkv_ln_b,
              w_out, moe_w_coarse, moe_b_coarse, moe_w_fine, moe_b_fine,
              moe_w_gate, moe_w_up, moe_w_down):
    xp, xs = x_prompt, x_sample
    B, S, _ = xp.shape
    Bd, T, _ = xs.shape
    pc_k, pc_v, ps_k, ps_v, pw_k, pw_v, pm_k, pm_v, pr_s, pr_x = ([] for _ in range(10))
    sc_k, sc_v, ss_k, ss_v, sw_k, sw_v, sr_s, sr_x = ([] for _ in range(8))
    for i in range(DEPTH):
        hp = rms_norm(xp, norm_mix[i])
        hs = rms_norm(xs, norm_mix[i])
        km_p, vm_p = mem_kv(mem_prompt, norm_mem[i], w_mem_kv[i], mem_k_gain[i])
        pm_k.append(km_p)
        pm_v.append(vm_p)
        if i % 2 == 0:
            ia = i // 2
            cmp_prm = (cmp_pe[ia], cmp_w1[ia], cmp_b1[ia], cmp_w2[ia], nsa_k_gain[ia, 0])
            q, kc, vc, ks, vs, kw, vw, gt, mq_p = nsa_project(hp, w_in_a[ia], nsa_q_gain[ia], nsa_k_gain[ia])
            mix_p = nsa_prompt(q, kc, vc, ks, vs, kw, vw, gt, cmp_prm)
            wl = min(WINDOW, S)
            pc_k.append(kc)
            pc_v.append(vc)
            ps_k.append(ks)
            ps_v.append(vs)
            pw_k.append(kw[:, S - wl:])
            pw_v.append(vw[:, S - wl:])
            q, kc, vc, ks, vs, kw, vw, gt, mq_s = nsa_project(hs, w_in_a[ia], nsa_q_gain[ia], nsa_k_gain[ia])
            mix_s, nwk, nwv = nsa_sample(q, kc, vc, ks, vs, kw, vw, gt,
                                         cache_cmp_k[ia], cache_cmp_v[ia], cache_sel_k[ia], cache_sel_v[ia],
                                         cache_win_k[ia], cache_win_v[ia], page_table, cmp_prm)
            sc_k.append(kc)
            sc_v.append(vc)
            ss_k.append(ks)
            ss_v.append(vs)
            sw_k.append(nwk)
            sw_v.append(nwv)
        else:
            ib = i // 2
            rw = (rwkv_mu[ib], rwkv_w0[ib], rwkv_w_up[ib], rwkv_a0[ib], rwkv_a_up[ib], rwkv_g_up[ib],
                  rwkv_k_k[ib], rwkv_k_a[ib], rwkv_r_k[ib], rwkv_ln_w[ib], rwkv_ln_b[ib])
            Pr, mq_p = split_cols(hp @ w_in_b[ib], B_COLS)
            mix_p, sh_p, st_p = rwkv_mix(Pr, jnp.zeros((B, SHIFT_W), Pr.dtype),
                                         jnp.zeros((B, RWKV_HEADS, HEAD_DIM, HEAD_DIM), jnp.float32), *rw)
            pr_s.append(st_p)
            pr_x.append(sh_p)
            Pr, mq_s = split_cols(hs @ w_in_b[ib], B_COLS)
            mix_s, sh_s, st_s = rwkv_mix(Pr, state_shift[ib], state_rwkv[ib], *rw)
            sr_s.append(st_s)
            sr_x.append(sh_s)
        mem_p = mem_attend(mq_p, km_p, vm_p, mem_q_gain[i])
        mem_s = mem_attend(mq_s, cache_mem_k[i], cache_mem_v[i], mem_q_gain[i])
        xp = xp + jnp.concatenate([mix_p, mem_p], axis=-1) @ w_out[i]
        xs = xs + jnp.concatenate([mix_s, mem_s], axis=-1) @ w_out[i]
        moe_prm = (moe_w_coarse[i], moe_b_coarse[i], moe_w_fine[i], moe_b_fine[i],
                   moe_w_gate[i], moe_w_up[i], moe_w_down[i])
        xp = xp + hier_moe(rms_norm(xp, norm_ffn[i]).reshape(B * S, -1), *moe_prm).reshape(B, S, -1)
        xs = xs + hier_moe(rms_norm(xs, norm_ffn[i]).reshape(Bd * T, -1), *moe_prm).reshape(Bd, T, -1)
    return (xp, xs,
            jnp.stack(pc_k), jnp.stack(pc_v), jnp.stack(ps_k), jnp.stack(ps_v),
            jnp.stack(pw_k), jnp.stack(pw_v), jnp.stack(pm_k), jnp.stack(pm_v),
            jnp.stack(pr_s), jnp.stack(pr_x),
            jnp.stack(sc_k), jnp.stack(sc_v), jnp.stack(ss_k), jnp.stack(ss_v),
            jnp.stack(sw_k), jnp.stack(sw_v), jnp.stack(sr_s), jnp.stack(sr_x))
```

```python
import functools

import jax
import jax.numpy as jnp
from jax import lax
from jax.experimental import pallas as pl
from jax.experimental.pallas import tpu as pltpu

F32 = jnp.float32
BF16 = jnp.bfloat16
HI = lax.Precision.HIGHEST

HEAD_DIM = 64
TOK_HEADS = 12
MEM_HEADS = 4
KV_GROUPS = 3
HEADS_PER_GROUP = TOK_HEADS // KV_GROUPS
QA = TOK_HEADS * HEAD_DIM
KVW = KV_GROUPS * HEAD_DIM
MEMQ = MEM_HEADS * HEAD_DIM
CMP_LEN = 32
CMP_STRIDE = 16
CMP_HID = 128
SEL_LEN = 64
SEL_TOPK = 16
WINDOW = 512
Q_BLOCK = 128
PAGE = 128
N_GROUPS = 4
E_PER_GROUP = 8
N_EXPERTS = 32
RWKV_COLS = (768, 768, 768, 64, 64, 128)
SHIFT_W = sum(RWKV_COLS)
RWKV_GN_EPS = 64e-5
RWKV_CHUNK = 64
RWKV_CHUNK_EXACT = True
NEG = -1e30
BIG = 1e30
LANES = 128
VMEM_LIMIT = 56 * 1024 * 1024


def _cparams(*sem):
    return pltpu.CompilerParams(dimension_semantics=sem, vmem_limit_bytes=VMEM_LIMIT)


def _dot_general(a, b, dims, exact):
    if exact:
        return lax.dot_general(a, b, (dims, ((), ())), precision=HI, preferred_element_type=F32)
    return lax.dot_general(a.astype(BF16), b.astype(BF16), (dims, ((), ())), preferred_element_type=F32)


def _dot(a, b, exact=False):
    return _dot_general(a, b, ((1,), (0,)), exact)


def _dot_nt(a, b, exact=False):
    return _dot_general(a, b, ((1,), (1,)), exact)


def _dot_tn(a, b, exact=False):
    return _dot_general(a, b, ((0,), (0,)), exact)


def _bf16_round(x):
    return x.astype(BF16).astype(F32)


def _rms(x, g, eps=1e-6):
    return x * lax.rsqrt(jnp.mean(x * x, axis=-1, keepdims=True) + eps) * g


def _masked_softmax(s, mask):
    s = jnp.where(mask, s, NEG)
    m = jnp.max(s, axis=-1, keepdims=True)
    e = jnp.where(mask, jnp.exp(s - m), 0.0)
    return e / jnp.maximum(jnp.sum(e, axis=-1, keepdims=True), 1e-30)


def _alibi_slope(h):
    return 2.0 ** (-8.0 * (h + 1) / TOK_HEADS)


def _topk_mask(score, k):
    lane = lax.broadcasted_iota(jnp.int32, score.shape, 1).astype(F32)
    sel = jnp.zeros(score.shape, F32)
    for _ in range(k):
        m = jnp.max(score, axis=-1, keepdims=True)
        idx = jnp.min(jnp.where(score == m, lane, 1e9), axis=-1, keepdims=True)
        hit = lane == idx
        sel = jnp.where(hit, 1.0, sel)
        score = jnp.where(hit, -jnp.inf, score)
    return sel


def _cmp_to_sel(n_cmp_rows, n_sel_lanes):
    n_i = lax.broadcasted_iota(jnp.int32, (n_cmp_rows, n_sel_lanes), 0)
    s_i = lax.broadcasted_iota(jnp.int32, (n_cmp_rows, n_sel_lanes), 1)
    c0 = n_i * CMP_STRIDE
    s0 = s_i * SEL_LEN
    return ((c0 < s0 + SEL_LEN) & (c0 + CMP_LEN > s0)).astype(F32)


def _block_expand(n_blk_rows, n_keys):
    b_i = lax.broadcasted_iota(jnp.int32, (n_blk_rows, n_keys), 0)
    k_i = lax.broadcasted_iota(jnp.int32, (n_blk_rows, n_keys), 1)
    return (jnp.right_shift(k_i, 6) == b_i).astype(BF16)


def _norm_proj_kernel(ops, x_ref, g_ref, *refs):
    n = len(ops)
    n_gain = sum(op == "hnorm" for op in ops)
    w_refs, gain_refs, o_refs = refs[:n], refs[n:n + n_gain], refs[n + n_gain:]
    h = _rms(x_ref[...], g_ref[...])
    gi = 0
    for op, w_ref, o_ref in zip(ops, w_refs, o_refs):
        z = _dot(h, w_ref[...])
        if op == "hnorm":
            gain = gain_refs[gi][...]
            gi += 1
            for hh in range(z.shape[1] // HEAD_DIM):
                seg = z[:, hh * HEAD_DIM:(hh + 1) * HEAD_DIM]
                o_ref[:, hh * HEAD_DIM:(hh + 1) * HEAD_DIM] = _rms(seg, gain)
        elif op == "sigmoid":
            o_ref[...] = jax.nn.sigmoid(z)
        else:
            o_ref[...] = z


def _norm_proj(x, g, weights, ops, gains, tm):
    n_rows, d = x.shape
    tm = min(tm, n_rows)
    assert n_rows % tm == 0
    in_specs = [pl.BlockSpec((tm, d), lambda i: (i, 0)), pl.BlockSpec((1, d), lambda i: (0, 0))]
    in_specs += [pl.BlockSpec(w.shape, lambda i: (0, 0)) for w in weights]
    in_specs += [pl.BlockSpec((1, HEAD_DIM), lambda i: (0, 0)) for _ in gains]
    out_shape = [jax.ShapeDtypeStruct((n_rows, w.shape[1]), F32) for w in weights]
    out_specs = [pl.BlockSpec((tm, w.shape[1]), lambda i: (i, 0)) for w in weights]
    return pl.pallas_call(
        functools.partial(_norm_proj_kernel, tuple(ops)),
        grid=(n_rows // tm,), in_specs=in_specs, out_specs=out_specs, out_shape=out_shape,
        compiler_params=_cparams("parallel"), name="norm_proj",
    )(x, g.reshape(1, d), *weights, *[gn.reshape(1, HEAD_DIM) for gn in gains])


def _compress_rows(rows_ref, pe_ref, w1_ref, b1_ref, w2_ref, gain, o_ref, u_ref, n_chunk):
    half = CMP_STRIDE * HEAD_DIM
    for g in range(KV_GROUPS):
        for l in range(CMP_STRIDE):
            src = l * KVW + g * HEAD_DIM
            u_ref[:, l * HEAD_DIM:(l + 1) * HEAD_DIM] = rows_ref[:, src:src + HEAD_DIM]
        u = u_ref[...]
        p0 = _dot(u + pe_ref[:, 0:half], w1_ref[0:half, :])
        p1 = _dot(u + pe_ref[:, half:2 * half], w1_ref[half:2 * half, :])
        hid = b1_ref[...] + p0 + pltpu.roll(p1, n_chunk - 1, 0)
        z = _dot(jax.nn.gelu(hid), w2_ref[...])
        if gain is not None:
            z = _rms(z, gain)
        o_ref[:, g * HEAD_DIM:(g + 1) * HEAD_DIM] = z


def _compress_prompt_kernel(k_ref, v_ref, pe_ref, w1_ref, b1_ref, w2_ref, gain_ref, ck_ref, cv_ref, u_ref,
                            *, n_chunk):
    _compress_rows(k_ref.at[0], pe_ref.at[0], w1_ref.at[0], b1_ref.at[0], w2_ref.at[0], gain_ref[...],
                   ck_ref.at[0], u_ref, n_chunk)
    _compress_rows(v_ref.at[0], pe_ref.at[1], w1_ref.at[1], b1_ref.at[1], w2_ref.at[1], None,
                   cv_ref.at[0], u_ref, n_chunk)


def _compress_weight_specs(nidx):
    zero = lambda *_: (0, 0, 0)
    del nidx
    return [pl.BlockSpec((2, 1, CMP_LEN * HEAD_DIM), zero),
            pl.BlockSpec((2, CMP_LEN * HEAD_DIM, CMP_HID), zero),
            pl.BlockSpec((2, 1, CMP_HID), zero),
            pl.BlockSpec((2, CMP_HID, HEAD_DIM), zero)]


def _compress_prompt(kc, vc, pe, w1, b1, w2, gain):
    b, s, _ = kc.shape
    n_chunk = s // CMP_STRIDE
    cw = CMP_STRIDE * KVW
    rows = pl.BlockSpec((1, n_chunk, cw), lambda i: (i, 0, 0))
    out = pl.BlockSpec((1, n_chunk, KVW), lambda i: (i, 0, 0))
    return pl.pallas_call(
        functools.partial(_compress_prompt_kernel, n_chunk=n_chunk),
        grid=(b,),
        in_specs=[rows, rows] + _compress_weight_specs(1) + [pl.BlockSpec((1, HEAD_DIM), lambda i: (0, 0))],
        out_specs=[out, out],
        out_shape=[jax.ShapeDtypeStruct((b, n_chunk, KVW), F32)] * 2,
        scratch_shapes=[pltpu.VMEM((n_chunk, CMP_STRIDE * HEAD_DIM), F32)],
        compiler_params=_cparams("parallel"), name="compress_prompt",
    )(kc.reshape(b, n_chunk, cw), vc.reshape(b, n_chunk, cw), pe.reshape(2, 1, -1), w1, b1.reshape(2, 1, -1), w2,
      gain.reshape(1, HEAD_DIM))


def _gather_pages(pt_ref, b, pool_ref, dst_ref, sem, n_pages):
    copies = []
    rpp = pool_ref.shape[1]
    for j in range(n_pages):
        cp = pltpu.make_async_copy(pool_ref.at[pt_ref[b, j]], dst_ref.at[pl.ds(j * rpp, rpp)], sem)
        cp.start()
        copies.append(cp)
    return copies


def _compress_sample_kernel(pt_ref, pk_ref, pv_ref, pe_ref, w1_ref, b1_ref, w2_ref, gain_ref,
                            ck_ref, cv_ref, kbuf, vbuf, u_ref, sem, *, n_pages):
    b = pl.program_id(0)
    n_chunk = n_pages * PAGE // CMP_STRIDE
    ck_copies = _gather_pages(pt_ref, b, pk_ref, kbuf, sem.at[0], n_pages)
    cv_copies = _gather_pages(pt_ref, b, pv_ref, vbuf, sem.at[1], n_pages)
    for cp in ck_copies:
        cp.wait()
    _compress_rows(kbuf, pe_ref.at[0], w1_ref.at[0], b1_ref.at[0], w2_ref.at[0], gain_ref[...],
                   ck_ref.at[0], u_ref, n_chunk)
    for cp in cv_copies:
        cp.wait()
    _compress_rows(vbuf, pe_ref.at[1], w1_ref.at[1], b1_ref.at[1], w2_ref.at[1], None,
                   cv_ref.at[0], u_ref, n_chunk)


def _compress_sample(page_table, pool_k, pool_v, pe, w1, b1, w2, gain):
    bd, n_pages = page_table.shape
    n_pool = pool_k.shape[0]
    n_chunk = n_pages * PAGE // CMP_STRIDE
    cw = CMP_STRIDE * KVW
    zero3 = lambda i, pt: (0, 0, 0)
    any_spec = pl.BlockSpec(memory_space=pl.ANY)
    out = pl.BlockSpec((1, n_chunk, KVW), lambda i, pt: (i, 0, 0))
    grid_spec = pltpu.PrefetchScalarGridSpec(
        num_scalar_prefetch=1, grid=(bd,),
        in_specs=[any_spec, any_spec,
                  pl.BlockSpec((2, 1, CMP_LEN * HEAD_DIM), zero3),
                  pl.BlockSpec((2, CMP_LEN * HEAD_DIM, CMP_HID), zero3),
                  pl.BlockSpec((2, 1, CMP_HID), zero3),
                  pl.BlockSpec((2, CMP_HID, HEAD_DIM), zero3),
                  pl.BlockSpec((1, HEAD_DIM), lambda i, pt: (0, 0))],
        out_specs=[out, out],
        scratch_shapes=[pltpu.VMEM((n_chunk, cw), F32), pltpu.VMEM((n_chunk, cw), F32),
                        pltpu.VMEM((n_chunk, CMP_STRIDE * HEAD_DIM), F32), pltpu.SemaphoreType.DMA((2,))])
    return pl.pallas_call(
        functools.partial(_compress_sample_kernel, n_pages=n_pages),
        grid_spec=grid_spec,
        out_shape=[jax.ShapeDtypeStruct((bd, n_chunk, KVW), F32)] * 2,
        compiler_params=_cparams("arbitrary"), name="compress_sample",
    )(page_table, pool_k.reshape(n_pool, PAGE // CMP_STRIDE, cw), pool_v.reshape(n_pool, PAGE // CMP_STRIDE, cw),
      pe.reshape(2, 1, -1), w1, b1.reshape(2, 1, -1), w2, gain.reshape(1, HEAD_DIM))


def _nsa_prompt_kernel(q_ref, gt_ref, ck_ref, cv_ref, ks_ref, vs_ref, kw_ref, vw_ref, o_ref, *, seq):
    tq = Q_BLOCK
    hpg = HEADS_PER_GROUP
    n_cmp = ck_ref.shape[1]
    n_sel = seq // SEL_LEN
    wlen = min(WINDOW + tq, seq)
    q0 = pl.program_id(1) * tq
    row = lax.broadcasted_iota(jnp.int32, (hpg * tq, 1), 0)
    t4 = q0 + jnp.bitwise_and(row, tq - 1)
    t1 = q0 + lax.broadcasted_iota(jnp.int32, (tq, 1), 0)
    c2s = _cmp_to_sel(n_cmp, LANES)
    expand = _block_expand(LANES, seq)
    blk = lax.broadcasted_iota(jnp.int32, (1, LANES), 1)
    cur = jnp.right_shift(t1, 6)
    forced = (blk == 0) | (blk == cur) | (blk == cur - 1)
    valid = blk * SEL_LEN <= t1
    c_end = lax.broadcasted_iota(jnp.int32, (1, n_cmp), 1) * CMP_STRIDE + (CMP_LEN - 1)
    pos_s = lax.broadcasted_iota(jnp.int32, (1, seq), 1)
    w_start = pl.multiple_of(jnp.clip(q0 - WINDOW, 0, seq - wlen), tq)
    pos_w = w_start + lax.broadcasted_iota(jnp.int32, (1, wlen), 1)
    gt = gt_ref[0]
    for g in range(KV_GROUPS):
        lanes = slice(g * HEAD_DIM, (g + 1) * HEAD_DIM)
        q4 = jnp.concatenate(
            [q_ref[0, :, (g * hpg + j) * HEAD_DIM:(g * hpg + j + 1) * HEAD_DIM] for j in range(hpg)],
            axis=0) * (HEAD_DIM ** -0.5)
        slope = jnp.full((hpg * tq, 1), _alibi_slope(g * hpg + hpg - 1), F32)
        for j in range(hpg - 2, -1, -1):
            slope = jnp.where(row < (j + 1) * tq, _alibi_slope(g * hpg + j), slope)
        dist = t4 - c_end
        p_c = _masked_softmax(_dot_nt(q4, ck_ref[0, :, lanes]) - slope * dist.astype(F32), dist >= 0)
        o_c = _dot(p_c, cv_ref[0, :, lanes])
        imp4 = _dot(p_c, c2s)
        imp = imp4[0:tq]
        for j in range(1, hpg):
            imp = imp + imp4[j * tq:(j + 1) * tq]
        score = jnp.where(valid, jnp.where(forced, BIG, imp), -BIG)
        score = jnp.where(blk < n_sel, score, -jnp.inf)
        key_sel = _dot(_topk_mask(score, min(SEL_TOPK, n_sel)), expand)
        key_sel4 = jnp.concatenate([key_sel] * hpg, axis=0)
        dist = t4 - pos_s
        p_s = _masked_softmax(_dot_nt(q4, ks_ref[0, :, lanes]) - slope * dist.astype(F32),
                              (key_sel4 > 0.5) & (dist >= 0))
        o_s = _dot(p_s, vs_ref[0, :, lanes])
        dist = t4 - pos_w
        p_w = _masked_softmax(_dot_nt(q4, kw_ref[0, pl.ds(w_start, wlen), lanes]) - slope * dist.astype(F32),
                              (dist >= 0) & (dist <= WINDOW))
        o_w = _dot(p_w, vw_ref[0, pl.ds(w_start, wlen), lanes])
        for j in range(hpg):
            h = g * hpg + j
            rows = slice(j * tq, (j + 1) * tq)
            o_ref[0, :, h * HEAD_DIM:(h + 1) * HEAD_DIM] = (
                gt[:, 3 * h:3 * h + 1] * o_c[rows] + gt[:, 3 * h + 1:3 * h + 2] * o_s[rows]
                + gt[:, 3 * h + 2:3 * h + 3] * o_w[rows])


def _nsa_prompt(q, gates, ck, cv, ks, vs, kw, vw):
    b, s, _ = q.shape
    n_cmp = ck.shape[1]
    qspec = lambda w: pl.BlockSpec((1, Q_BLOCK, w), lambda i, j: (i, j, 0))
    full = lambda n: pl.BlockSpec((1, n, KVW), lambda i, j: (i, 0, 0))
    return pl.pallas_call(
        functools.partial(_nsa_prompt_kernel, seq=s),
        grid=(b, s // Q_BLOCK),
        in_specs=[qspec(QA), qspec(LANES), full(n_cmp), full(n_cmp), full(s), full(s), full(s), full(s)],
        out_specs=qspec(QA),
        out_shape=jax.ShapeDtypeStruct((b, s, QA), F32),
        compiler_params=_cparams("parallel", "parallel"), name="nsa_prompt",
    )(q, gates, ck, cv, ks, vs, kw, vw)


def _nsa_sample_kernel(pt_ref, q_ref, gt_ref, ck_ref, cv_ref, ksn_ref, vsn_ref, kwn_ref, vwn_ref,
                       pks_ref, pvs_ref, wk_ref, wv_ref, o_ref, owk_ref, owv_ref, kbuf, vbuf, sem,
                       *, n_pages):
    b = pl.program_id(0)
    hpg = HEADS_PER_GROUP
    past = n_pages * PAGE
    n_cmp = ck_ref.shape[1]
    n_sel_past = past // SEL_LEN
    n_sel = n_sel_past + 1
    wb = wk_ref.shape[1]
    sel_lanes = 2 * LANES
    k_copies = _gather_pages(pt_ref, b, pks_ref, kbuf, sem.at[0], n_pages)
    v_copies = _gather_pages(pt_ref, b, pvs_ref, vbuf, sem.at[1], n_pages)

    row = lax.broadcasted_iota(jnp.int32, (8, 1), 0)
    gt = gt_ref[0]
    c_end = lax.broadcasted_iota(jnp.int32, (1, n_cmp), 1) * CMP_STRIDE + (CMP_LEN - 1)
    dist_c = past - c_end
    blk = lax.broadcasted_iota(jnp.int32, (1, sel_lanes), 1)
    cur = past // SEL_LEN
    forced = (blk == 0) | (blk == cur) | (blk == cur - 1)
    valid = blk * SEL_LEN <= past
    c2s = _cmp_to_sel(n_cmp, sel_lanes)

    def q_rows(g):
        q4 = jnp.concatenate(
            [q_ref[0, :, (g * hpg + j) * HEAD_DIM:(g * hpg + j + 1) * HEAD_DIM] for j in range(hpg)]
            + [jnp.zeros((8 - hpg, HEAD_DIM), F32)], axis=0) * (HEAD_DIM ** -0.5)
        slope = jnp.full((8, 1), _alibi_slope(g * hpg + hpg - 1), F32)
        for j in range(hpg - 2, -1, -1):
            slope = jnp.where(row < j + 1, _alibi_slope(g * hpg + j), slope)
        return q4, slope

    o_cs, imps = [], []
    for g in range(KV_GROUPS):
        lanes = slice(g * HEAD_DIM, (g + 1) * HEAD_DIM)
        q4, slope = q_rows(g)
        p_c = _masked_softmax(_dot_nt(q4, ck_ref[0, :, lanes]) - slope * dist_c.astype(F32), dist_c >= 0)
        o_cs.append(_dot(p_c, cv_ref[0, :, lanes]))
        imps.append(jnp.sum(jnp.where(row < hpg, _dot(p_c, c2s), 0.0), axis=0, keepdims=True))
    imp = jnp.concatenate(imps + [jnp.zeros((8 - KV_GROUPS, sel_lanes), F32)], axis=0)
    score = jnp.where(valid, jnp.where(forced, BIG, imp), -BIG)
    score = jnp.where(blk < n_sel, score, -jnp.inf)
    sel = _topk_mask(score, min(SEL_TOPK, n_sel))
    key_sel = _dot(sel[:, 0:n_sel_past], _block_expand(n_sel_past, past))
    sel_new = sel[:, n_sel_past:n_sel_past + 1]

    pos_s = lax.broadcasted_iota(jnp.int32, (1, past), 1)
    dist_s = (past - pos_s).astype(F32)
    pos_w = past - wb + lax.broadcasted_iota(jnp.int32, (1, wb), 1)
    dist_w = past - pos_w
    mask_w = (dist_w >= 0) & (dist_w <= WINDOW) & (pos_w >= 0)
    for cp in k_copies:
        cp.wait()
    for cp in v_copies:
        cp.wait()
    for g in range(KV_GROUPS):
        lanes = slice(g * HEAD_DIM, (g + 1) * HEAD_DIM)
        q4, slope = q_rows(g)
        q4r = _bf16_round(q4)
        s_p = jnp.where(key_sel[g:g + 1] > 0.5, _dot_nt(q4, kbuf[:, lanes]) - slope * dist_s, NEG)
        new_ok = sel_new[g:g + 1] > 0.5
        s_n = jnp.where(new_ok, jnp.sum(q4r * _bf16_round(ksn_ref[0, :, lanes]), axis=-1, keepdims=True), NEG)
        m = jnp.maximum(jnp.max(s_p, axis=-1, keepdims=True), s_n)
        e_p = jnp.where(key_sel[g:g + 1] > 0.5, jnp.exp(s_p - m), 0.0)
        e_n = jnp.where(new_ok, jnp.exp(s_n - m), 0.0)
        den = jnp.maximum(jnp.sum(e_p, axis=-1, keepdims=True) + e_n, 1e-30)
        o_s = _dot(e_p / den, vbuf[:, lanes]) + _bf16_round(e_n / den) * _bf16_round(vsn_ref[0, :, lanes])
        s_p = jnp.where(mask_w, _dot_nt(q4, wk_ref[0, :, lanes]) - slope * dist_w.astype(F32), NEG)
        s_n = jnp.sum(q4r * _bf16_round(kwn_ref[0, :, lanes]), axis=-1, keepdims=True)
        m = jnp.maximum(jnp.max(s_p, axis=-1, keepdims=True), s_n)
        e_p = jnp.where(mask_w, jnp.exp(s_p - m), 0.0)
        e_n = jnp.exp(s_n - m)
        den = jnp.maximum(jnp.sum(e_p, axis=-1, keepdims=True) + e_n, 1e-30)
        o_w = _dot(e_p / den, wv_ref[0, :, lanes]) + _bf16_round(e_n / den) * _bf16_round(vwn_ref[0, :, lanes])
        for j in range(hpg):
            h = g * hpg + j
            o_ref[0, :, h * HEAD_DIM:(h + 1) * HEAD_DIM] = (
                gt[:, 3 * h:3 * h + 1] * o_cs[g][j:j + 1] + gt[:, 3 * h + 1:3 * h + 2] * o_s[j:j + 1]
                + gt[:, 3 * h + 2:3 * h + 3] * o_w[j:j + 1])
    last = lax.broadcasted_iota(jnp.int32, (wb, 1), 0) == wb - 1
    owk_ref[0] = jnp.where(last, kwn_ref[0], pltpu.roll(wk_ref[0], wb - 1, 0))
    owv_ref[0] = jnp.where(last, vwn_ref[0], pltpu.roll(wv_ref[0], wb - 1, 0))


def _nsa_sample(page_table, q, gates, ck, cv, ks_new, vs_new, kw_new, vw_new, pool_ks, pool_vs, win_k, win_v):
    bd, n_pages = page_table.shape
    n_pool = pool_ks.shape[0]
    n_cmp = ck.shape[1]
    wb = win_k.shape[1]
    one = lambda w: pl.BlockSpec((1, 1, w), lambda i, pt: (i, 0, 0))
    rows = lambda n: pl.BlockSpec((1, n, KVW), lambda i, pt: (i, 0, 0))
    any_spec = pl.BlockSpec(memory_space=pl.ANY)
    grid_spec = pltpu.PrefetchScalarGridSpec(
        num_scalar_prefetch=1, grid=(bd,),
        in_specs=[one(QA), one(LANES), rows(n_cmp), rows(n_cmp), one(KVW), one(KVW), one(KVW), one(KVW),
                  any_spec, any_spec, rows(wb), rows(wb)],
        out_specs=[one(QA), rows(wb), rows(wb)],
        scratch_shapes=[pltpu.VMEM((n_pages * PAGE, KVW), F32), pltpu.VMEM((n_pages * PAGE, KVW), F32),
                        pltpu.SemaphoreType.DMA((2,))])
    r3 = lambda z: z.reshape(bd, 1, -1)
    return pl.pallas_call(
        functools.partial(_nsa_sample_kernel, n_pages=n_pages),
        grid_spec=grid_spec,
        out_shape=[jax.ShapeDtypeStruct((bd, 1, QA), F32), jax.ShapeDtypeStruct((bd, wb, KVW), F32),
                   jax.ShapeDtypeStruct((bd, wb, KVW), F32)],
        compiler_params=_cparams("arbitrary"), name="nsa_sample",
    )(page_table, r3(q), r3(gates), ck, cv, r3(ks_new), r3(vs_new), r3(kw_new), r3(vw_new),
      pool_ks.reshape(n_pool, PAGE, KVW), pool_vs.reshape(n_pool, PAGE, KVW),
      win_k.reshape(bd, wb, KVW), win_v.reshape(bd, wb, KVW))


def _mem_attn_kernel(q_ref, k_ref, v_ref, gain_ref, o_ref):
    tq = q_ref.shape[1]
    pad = max(8 - tq, 0)
    for h in range(MEM_HEADS):
        lanes = slice(h * HEAD_DIM, (h + 1) * HEAD_DIM)
        q = _rms(q_ref[0, :, lanes], gain_ref[...]) * (HEAD_DIM ** -0.5)
        if pad:
            q = jnp.concatenate([q, jnp.zeros((pad, HEAD_DIM), F32)], axis=0)
        s = _dot_nt(q, k_ref[0, :, lanes])
        e = jnp.exp(s - jnp.max(s, axis=-1, keepdims=True))
        p = e / jnp.sum(e, axis=-1, keepdims=True)
        o_ref[0, :, lanes] = _dot(p, v_ref[0, :, lanes])[0:tq]


def _mem_attn(mq, km, vm, gain, tq):
    b, t, _ = mq.shape
    m = km.shape[1]
    tq = min(tq, t)
    qspec = pl.BlockSpec((1, tq, MEMQ), lambda i, j: (i, j, 0))
    kspec = pl.BlockSpec((1, m, MEMQ), lambda i, j: (i, 0, 0))
    return pl.pallas_call(
        _mem_attn_kernel, grid=(b, t // tq),
        in_specs=[qspec, kspec, kspec, pl.BlockSpec((1, HEAD_DIM), lambda i, j: (0, 0))],
        out_specs=qspec, out_shape=jax.ShapeDtypeStruct((b, t, MEMQ), F32),
        compiler_params=_cparams("parallel", "parallel"), name="mem_attn",
    )(mq, km, vm, gain.reshape(1, HEAD_DIM))


def _outproj_router_kernel(x_ref, mix_ref, mem_ref, wo_ref, g_ref, wc_ref, bc_ref, wf_ref, bf_ref,
                           xo_ref, hn_ref, cw_ref):
    x = x_ref[...] + _dot(mix_ref[...], wo_ref[0:QA, :]) + _dot(mem_ref[...], wo_ref[QA:QA + MEMQ, :])
    xo_ref[...] = x
    hn = _rms(x, g_ref[...])
    hn_ref[...] = hn
    lane = lax.broadcasted_iota(jnp.int32, (1, LANES), 1).astype(F32)
    lg = jnp.where(lane < N_GROUPS, _dot(hn, wc_ref[...]) + bc_ref[...], -jnp.inf)
    m = jnp.max(lg, axis=-1, keepdims=True)
    grp = jnp.min(jnp.where(lg == m, lane, 1e9), axis=-1, keepdims=True)
    p_grp = 1.0 / jnp.sum(jnp.exp(lg - m), axis=-1, keepdims=True)
    in_grp = (lane >= grp * E_PER_GROUP) & (lane < (grp + 1.0) * E_PER_GROUP)
    lf = jnp.where(in_grp, _dot(hn, wf_ref[...]) + bf_ref[...], -jnp.inf)
    v1 = jnp.max(lf, axis=-1, keepdims=True)
    i1 = jnp.min(jnp.where(lf == v1, lane, 1e9), axis=-1, keepdims=True)
    lf2 = jnp.where(lane == i1, -jnp.inf, lf)
    v2 = jnp.max(lf2, axis=-1, keepdims=True)
    i2 = jnp.min(jnp.where(lf2 == v2, lane, 1e9), axis=-1, keepdims=True)
    e2 = jnp.exp(v2 - v1)
    den = 1.0 + e2
    cw_ref[...] = jnp.where(lane == i1, p_grp / den, 0.0) + jnp.where(lane == i2, p_grp * (e2 / den), 0.0)


def _outproj_router(x, mix, mem, w_out, g_ffn, w_coarse, b_coarse, w_fine, b_fine, tm):
    n, d = x.shape
    tm = min(tm, n)
    pad_w = lambda w: jnp.pad(w, ((0, 0), (0, LANES - w.shape[1])))
    pad_b = lambda v: jnp.pad(v, (0, LANES - v.shape[0])).reshape(1, LANES)
    row = lambda w: pl.BlockSpec((tm, w), lambda i: (i, 0))
    const = lambda r, c: pl.BlockSpec((r, c), lambda i: (0, 0))
    return pl.pallas_call(
        _outproj_router_kernel, grid=(n // tm,),
        in_specs=[row(d), row(QA), row(MEMQ), const(QA + MEMQ, d), const(1, d), const(d, LANES),
                  const(1, LANES), const(d, LANES), const(1, LANES)],
        out_specs=[row(d), row(d), row(LANES)],
        out_shape=[jax.ShapeDtypeStruct((n, d), F32), jax.ShapeDtypeStruct((n, d), F32),
                   jax.ShapeDtypeStruct((n, LANES), F32)],
        compiler_params=_cparams("parallel"), name="outproj_router",
    )(x, mix, mem, w_out, g_ffn.reshape(1, d), pad_w(w_coarse), pad_b(b_coarse), pad_w(w_fine), pad_b(b_fine))


def _moe_kernel(x_ref, hn_ref, cw_ref, wg_ref, wu_ref, wd_ref, o_ref):
    e = pl.program_id(1)

    @pl.when(e == 0)
    def _():
        o_ref[...] = x_ref[...]

    hb = hn_ref[...].astype(BF16)
    gate = _dot(hb, wg_ref[0])
    up = _dot(hb, wu_ref[0])
    lane = lax.broadcasted_iota(jnp.int32, (1, LANES), 1)
    c = jnp.sum(jnp.where(lane == e, cw_ref[...], 0.0), axis=-1, keepdims=True)
    o_ref[...] += _dot(gate * jax.nn.sigmoid(gate) * up, wd_ref[0]) * c


def _moe(x, hn, cw, w_gate, w_up, w_down, tm):
    n, d = x.shape
    tm = min(tm, n)
    n_exp, _, d_exp = w_gate.shape
    row = lambda w: pl.BlockSpec((tm, w), lambda i, e: (i, 0))
    return pl.pallas_call(
        _moe_kernel, grid=(n // tm, n_exp),
        in_specs=[row(d), row(d), row(LANES),
                  pl.BlockSpec((1, d, d_exp), lambda i, e: (e, 0, 0)),
                  pl.BlockSpec((1, d, d_exp), lambda i, e: (e, 0, 0)),
                  pl.BlockSpec((1, d_exp, d), lambda i, e: (e, 0, 0))],
        out_specs=row(d), out_shape=jax.ShapeDtypeStruct((n, d), F32),
        compiler_params=_cparams("parallel", "arbitrary"), name="moe",
    )(x, hn, cw, w_gate, w_up, w_down)


def _rwkv_prep_body(p, prev, mu_ref, w0_ref, wup_ref, a0_ref, aup_ref, gup_ref, kk_ref, ka_ref, outs):
    r_ref, ld_ref, k_ref, v_ref, kkn_ref, a_ref, g_ref = outs
    x = p + mu_ref[...] * (prev - p)
    c0, c1, c2, c3, c4 = 768, 1536, 2304, 2368, 2432
    r, k, v = x[:, 0:c0], x[:, c0:c1], x[:, c1:c2]
    xw, xa, xg = x[:, c2:c3], x[:, c3:c4], x[:, c4:SHIFT_W]
    z = -(w0_ref[...] + _dot(jnp.tanh(xw), wup_ref[...]))
    softplus = jnp.maximum(z, 0.0) + jnp.log(1.0 + jnp.exp(-jnp.abs(z)))
    w = -softplus - 0.5
    a = jax.nn.sigmoid(a0_ref[...] + _dot(xa, aup_ref[...]))
    r_ref[...] = r
    ld_ref[...] = -jnp.exp(w)
    k_ref[...] = k * (1.0 + (a - 1.0) * ka_ref[...])
    v_ref[...] = v
    a_ref[...] = a
    g_ref[...] = _dot(jax.nn.sigmoid(xg), gup_ref[...])
    kk = k * kk_ref[...]
    for h in range(TOK_HEADS):
        lanes = slice(h * HEAD_DIM, (h + 1) * HEAD_DIM)
        seg = kk[:, lanes]
        nrm = jnp.sqrt(jnp.sum(seg * seg, axis=-1, keepdims=True))
        kkn_ref[:, lanes] = seg / jnp.maximum(nrm, 1e-12)


def _rwkv_prep_prompt_kernel(p_ref, mu_ref, w0_ref, wup_ref, a0_ref, aup_ref, gup_ref, kk_ref, ka_ref,
                             *rest):
    outs, carry = rest[:7], rest[7]
    tm = p_ref.shape[1]

    @pl.when(pl.program_id(1) == 0)
    def _():
        carry[...] = jnp.zeros_like(carry)

    p = p_ref[0]
    first = lax.broadcasted_iota(jnp.int32, (tm, 1), 0) == 0
    prev = jnp.where(first, carry[...], pltpu.roll(p, 1, 0))
    carry[...] = p[tm - 1:tm]
    _rwkv_prep_body(p, prev, mu_ref, w0_ref, wup_ref, a0_ref, aup_ref, gup_ref, kk_ref, ka_ref,
                    [o.at[0] for o in outs])


def _rwkv_prep_sample_kernel(p_ref, prev_ref, mu_ref, w0_ref, wup_ref, a0_ref, aup_ref, gup_ref, kk_ref,
                             ka_ref, *outs):
    _rwkv_prep_body(p_ref[...], prev_ref[...], mu_ref, w0_ref, wup_ref, a0_ref, aup_ref, gup_ref, kk_ref,
                    ka_ref, outs)


def _rwkv_weight_args(mu, w0, w_up, a0, a_up, g_up, k_k, k_a):
    row = lambda v: v.reshape(1, -1)
    return (row(mu), row(w0), w_up, row(a0), a_up, g_up, row(k_k), row(k_a))


def _rwkv_prep_prompt(pr, wargs, tm):
    b, s, _ = pr.shape
    tm = min(tm, s)
    const = lambda a: pl.BlockSpec(a.shape, lambda i, j: (0, 0))
    ospec = pl.BlockSpec((1, tm, QA), lambda i, j: (i, j, 0))
    return pl.pallas_call(
        _rwkv_prep_prompt_kernel, grid=(b, s // tm),
        in_specs=[pl.BlockSpec((1, tm, SHIFT_W), lambda i, j: (i, j, 0))] + [const(a) for a in wargs],
        out_specs=[ospec] * 7, out_shape=[jax.ShapeDtypeStruct((b, s, QA), F32)] * 7,
        scratch_shapes=[pltpu.VMEM((1, SHIFT_W), F32)],
        compiler_params=_cparams("parallel", "arbitrary"), name="rwkv_prep_prompt",
    )(pr, *wargs)


def _rwkv_prep_sample(pr, prev, wargs):
    n = pr.shape[0]
    full = lambda a: pl.BlockSpec(a.shape, lambda i: (0, 0))
    return pl.pallas_call(
        _rwkv_prep_sample_kernel, grid=(1,),
        in_specs=[full(pr), full(prev)] + [full(a) for a in wargs],
        out_specs=[pl.BlockSpec((n, QA), lambda i: (0, 0))] * 7,
        out_shape=[jax.ShapeDtypeStruct((n, QA), F32)] * 7,
        compiler_params=_cparams("arbitrary"), name="rwkv_prep_sample",
    )(pr, prev, *wargs)


def _rwkv_finish(y, r, k, v, g, rk, lnw, lnb):
    m = jnp.mean(y, axis=-1, keepdims=True)
    var = jnp.mean(jnp.square(y - m), axis=-1, keepdims=True)
    yn = (y - m) * lax.rsqrt(var + RWKV_GN_EPS) * lnw + lnb
    bonus = jnp.sum(r * k * rk, axis=-1, keepdims=True) * v
    return (yn + bonus) * g


def _rwkv_chunk_kernel(r_ref, ld_ref, k_ref, v_ref, kk_ref, a_ref, g_ref, rk_ref, lnw_ref, lnb_ref,
                       o_ref, st_ref, z_ref):
    c = r_ref.shape[1]
    ci = pl.program_id(1)

    @pl.when(ci == 0)
    def _():
        z_ref[...] = jnp.zeros_like(z_ref)

    ri = lax.broadcasted_iota(jnp.int32, (c, c), 0)
    cj = lax.broadcasted_iota(jnp.int32, (c, c), 1)
    incl = ri >= cj
    strict = ri > cj
    n = HEAD_DIM
    eye = (lax.broadcasted_iota(jnp.int32, (n, n), 0) == lax.broadcasted_iota(jnp.int32, (n, n), 1)).astype(F32)
    ld = ld_ref[0]
    ex = RWKV_CHUNK_EXACT
    lc_all = _dot(incl.astype(F32), ld, exact=True)
    for h in range(TOK_HEADS):
        lanes = slice(h * HEAD_DIM, (h + 1) * HEAD_DIM)
        r, k, v, kk, a = r_ref[0, :, lanes], k_ref[0, :, lanes], v_ref[0, :, lanes], kk_ref[0, :, lanes], \
            a_ref[0, :, lanes]
        lc = lc_all[:, lanes]
        lx = lc - ld[:, lanes]
        l_end = lc[c - 1:c]
        b = kk * a
        e_neg = jnp.exp(-lc)
        e_rem = jnp.exp(l_end - lc)
        at = -kk * jnp.exp(lx)
        rt = r * jnp.exp(lc)
        ar = jnp.concatenate([at, rt], axis=0)
        bk = jnp.concatenate([b * e_neg, k * e_neg], axis=0)
        big = _dot_nt(ar, bk, ex)
        a_ab = jnp.where(strict, big[0:c, 0:c], 0.0)
        a_ak = jnp.where(strict, big[0:c, c:2 * c], 0.0)
        a_rb = jnp.where(incl, big[c:2 * c, 0:c], 0.0)
        a_rk = jnp.where(incl, big[c:2 * c, c:2 * c], 0.0)
        x = jnp.concatenate([at, _dot(a_ak, v, ex)], axis=1)
        npow = a_ab
        x = x + _dot(npow, x, ex)
        steps = 1
        while 2 * steps < c:
            npow = _dot(npow, npow, ex)
            x = x + _dot(npow, x, ex)
            steps *= 2
        w_u0 = x
        lower = jnp.concatenate([jnp.zeros((c, HEAD_DIM), F32), v], axis=1)
        wv = jnp.concatenate([w_u0, lower], axis=0)
        rq_y0 = _dot(jnp.concatenate([a_rb, a_rk], axis=1), wv, ex)
        bk_hat = jnp.concatenate([b * e_rem, k * e_rem], axis=0)
        m_n = _dot_tn(bk_hat, wv, ex)
        rq = rt + rq_y0[:, 0:HEAD_DIM]
        mm = m_n[:, 0:HEAD_DIM] + eye * jnp.exp(l_end)
        z = z_ref[h]
        yz = _dot(jnp.concatenate([rq, mm], axis=0), z, ex)
        y = yz[0:c] + rq_y0[:, HEAD_DIM:2 * HEAD_DIM]
        z_new = yz[c:c + HEAD_DIM] + m_n[:, HEAD_DIM:2 * HEAD_DIM]
        z_ref[h] = z_new
        o_ref[0, :, lanes] = _rwkv_finish(y, r, k, v, g_ref[0, :, lanes], rk_ref[:, lanes], lnw_ref[:, lanes],
                                          lnb_ref[:, lanes])

        @pl.when(ci == pl.num_programs(1) - 1)
        def _():
            st_ref[0, h] = z_new.T


def _rwkv_chunk(r, ld, k, v, kk, a, g, r_k, ln_w, ln_b):
    b, s, _ = r.shape
    c = RWKV_CHUNK
    tok = pl.BlockSpec((1, c, QA), lambda i, j: (i, j, 0))
    const = pl.BlockSpec((1, QA), lambda i, j: (0, 0))
    return pl.pallas_call(
        _rwkv_chunk_kernel, grid=(b, s // c),
        in_specs=[tok] * 7 + [const] * 3,
        out_specs=[tok, pl.BlockSpec((1, TOK_HEADS, HEAD_DIM, HEAD_DIM), lambda i, j: (i, 0, 0, 0))],
        out_shape=[jax.ShapeDtypeStruct((b, s, QA), F32),
                   jax.ShapeDtypeStruct((b, TOK_HEADS, HEAD_DIM, HEAD_DIM), F32)],
        scratch_shapes=[pltpu.VMEM((TOK_HEADS, HEAD_DIM, HEAD_DIM), F32)],
        compiler_params=_cparams("parallel", "arbitrary"), name="rwkv_chunk",
    )(r, ld, k, v, kk, a, g, r_k.reshape(1, QA), ln_w.reshape(1, QA), ln_b.reshape(1, QA))


def _rwkv_step_kernel(r_ref, ld_ref, k_ref, v_ref, kk_ref, a_ref, g_ref, rk_ref, lnw_ref, lnb_ref, s_ref,
                      o_ref, so_ref):
    n = HEAD_DIM
    eye = (lax.broadcasted_iota(jnp.int32, (n, n), 0) == lax.broadcasted_iota(jnp.int32, (n, n), 1)).astype(F32)
    for h in range(TOK_HEADS):
        lanes = slice(h * n, (h + 1) * n)
        r, k, v, kk, a = r_ref[0, :, lanes], k_ref[0, :, lanes], v_ref[0, :, lanes], kk_ref[0, :, lanes], \
            a_ref[0, :, lanes]
        d = jnp.exp(ld_ref[0, :, lanes])
        s = s_ref[0, h]
        sa = jnp.sum(_bf16_round(s) * _bf16_round(-kk), axis=1, keepdims=True)
        v_col = jnp.sum(eye * v, axis=1, keepdims=True)
        s_new = s * d + sa * (kk * a) + v_col * k
        so_ref[0, h] = s_new
        y_col = jnp.sum(_bf16_round(s_new) * _bf16_round(r), axis=1, keepdims=True)
        y = jnp.sum(eye * y_col, axis=0, keepdims=True)
        o_ref[0, :, lanes] = _rwkv_finish(y, r, k, v, g_ref[0, :, lanes], rk_ref[:, lanes], lnw_ref[:, lanes],
                                          lnb_ref[:, lanes])


def _rwkv_step(r, ld, k, v, kk, a, g, r_k, ln_w, ln_b, state):
    n = r.shape[0]
    tok = pl.BlockSpec((1, 1, QA), lambda i: (i, 0, 0))
    const = pl.BlockSpec((1, QA), lambda i: (0, 0))
    st = pl.BlockSpec((1, TOK_HEADS, HEAD_DIM, HEAD_DIM), lambda i: (i, 0, 0, 0))
    r3 = lambda z: z.reshape(n, 1, QA)
    return pl.pallas_call(
        _rwkv_step_kernel, grid=(n,),
        in_specs=[tok] * 7 + [const] * 3 + [st],
        out_specs=[tok, st],
        out_shape=[jax.ShapeDtypeStruct((n, 1, QA), F32), jax.ShapeDtypeStruct(state.shape, F32)],
        compiler_params=_cparams("parallel"), name="rwkv_step",
    )(r3(r), r3(ld), r3(k), r3(v), r3(kk), r3(a), r3(g), r_k.reshape(1, QA), ln_w.reshape(1, QA),
      ln_b.reshape(1, QA), state)


def _split_w_in_a(w):
    offs = [0, QA]
    for _ in range(6):
        offs.append(offs[-1] + KVW)
    offs.append(offs[-1] + 3 * TOK_HEADS)
    offs.append(offs[-1] + MEMQ)
    pieces = [w[:, offs[i]:offs[i + 1]] for i in range(9)]
    pieces[7] = jnp.pad(pieces[7], ((0, 0), (0, LANES - 3 * TOK_HEADS)))
    return pieces


_A_OPS = ("hnorm", "none", "none", "hnorm", "none", "hnorm", "none", "sigmoid", "none")


def _ffn(x, mix, mem, w_out, g_ffn, wc, bc, wf, bf, w_gate, w_up, w_down, tm_proj, tm_moe):
    x_new, hn, cw = _outproj_router(x, mix, mem, w_out, g_ffn, wc, bc, wf, bf, tm_proj)
    return _moe(x_new, hn, cw, w_gate, w_up, w_down, tm_moe)


def kernel(x_prompt, x_sample, cache_cmp_k, cache_cmp_v, cache_sel_k, cache_sel_v, cache_win_k, cache_win_v, cache_mem_k, cache_mem_v, state_rwkv, state_shift, page_table, mem_prompt, norm_mix, norm_ffn, norm_mem, w_mem_kv, mem_q_gain, mem_k_gain, w_in_a, nsa_q_gain, nsa_k_gain, cmp_pe, cmp_w1, cmp_b1, cmp_w2, w_in_b, rwkv_mu, rwkv_w0, rwkv_w_up, rwkv_a0, rwkv_a_up, rwkv_g_up, rwkv_k_k, rwkv_k_a, rwkv_r_k, rwkv_ln_w, rwkv_ln_b, w_out, moe_w_coarse, moe_b_coarse, moe_w_fine, moe_b_fine, moe_w_gate, moe_w_up, moe_w_down):
    b, s, d = x_prompt.shape
    bd = x_sample.shape[0]
    depth = norm_mix.shape[0]
    m_len = mem_prompt.shape[1]
    wl = min(WINDOW, s)
    xp = x_prompt.reshape(b * s, d)
    xs = x_sample.reshape(bd, d)
    mem2 = mem_prompt.reshape(b * m_len, d)
    outs = {name: [] for name in ("pc_k", "pc_v", "ps_k", "ps_v", "pw_k", "pw_v", "pm_k", "pm_v", "pr_s", "pr_x",
                                  "sc_k", "sc_v", "ss_k", "ss_v", "sw_k", "sw_v", "sr_s", "sr_x")}
    for i in range(depth):
        km_p, vm_p = _norm_proj(mem2, norm_mem[i], [w_mem_kv[i][:, :MEMQ], w_mem_kv[i][:, MEMQ:]],
                                ("hnorm", "none"), [mem_k_gain[i]], 256)
        km_p, vm_p = km_p.reshape(b, m_len, MEMQ), vm_p.reshape(b, m_len, MEMQ)
        outs["pm_k"].append(km_p.reshape(b, m_len, MEM_HEADS, HEAD_DIM))
        outs["pm_v"].append(vm_p.reshape(b, m_len, MEM_HEADS, HEAD_DIM))
        if i % 2 == 0:
            ia = i // 2
            pieces = _split_w_in_a(w_in_a[ia])
            gains = [nsa_q_gain[ia], nsa_k_gain[ia, 1], nsa_k_gain[ia, 2]]
            cmp_args = (cmp_pe[ia], cmp_w1[ia], cmp_b1[ia], cmp_w2[ia], nsa_k_gain[ia, 0])
            q, kc, vc, ks, vs, kw, vw, gt, mq_p = _norm_proj(xp, norm_mix[i], pieces, _A_OPS, gains, 256)
            r3 = lambda z: z.reshape(b, s, -1)
            kc, vc, ks, vs, kw, vw = (r3(z) for z in (kc, vc, ks, vs, kw, vw))
            ck, cv = _compress_prompt(kc, vc, *cmp_args)
            mix_p = _nsa_prompt(r3(q), r3(gt), ck, cv, ks, vs, kw, vw).reshape(b * s, QA)
            r5 = lambda z: z.reshape(b, -1, KV_GROUPS, HEAD_DIM)
            for name, z in (("pc_k", kc), ("pc_v", vc), ("ps_k", ks), ("ps_v", vs),
                            ("pw_k", kw[:, s - wl:]), ("pw_v", vw[:, s - wl:])):
                outs[name].append(r5(z))
            q, kc, vc, ks, vs, kw, vw, gt, mq_s = _norm_proj(xs, norm_mix[i], pieces, _A_OPS, gains, 256)
            ck, cv = _compress_sample(page_table, cache_cmp_k[ia], cache_cmp_v[ia], *cmp_args)
            mix_s, nwk, nwv = _nsa_sample(page_table, q, gt, ck, cv, ks, vs, kw, vw,
                                          cache_sel_k[ia], cache_sel_v[ia], cache_win_k[ia], cache_win_v[ia])
            mix_s = mix_s.reshape(bd, QA)
            r5 = lambda z: z.reshape(bd, -1, KV_GROUPS, HEAD_DIM)
            for name, z in (("sc_k", kc), ("sc_v", vc), ("ss_k", ks), ("ss_v", vs), ("sw_k", nwk), ("sw_v", nwv)):
                outs[name].append(r5(z))
        else:
            ib = i // 2
            pieces = [w_in_b[ib][:, :SHIFT_W], w_in_b[ib][:, SHIFT_W:]]
            wargs = _rwkv_weight_args(rwkv_mu[ib], rwkv_w0[ib], rwkv_w_up[ib], rwkv_a0[ib], rwkv_a_up[ib],
                                      rwkv_g_up[ib], rwkv_k_k[ib], rwkv_k_a[ib])
            fin = (rwkv_r_k[ib].reshape(-1), rwkv_ln_w[ib], rwkv_ln_b[ib])
            pr, mq_p = _norm_proj(xp, norm_mix[i], pieces, ("none", "none"), [], 256)
            pr = pr.reshape(b, s, SHIFT_W)
            prep = _rwkv_prep_prompt(pr, wargs, 256)
            mix_p, st_p = _rwkv_chunk(*prep, *fin)
            mix_p = mix_p.reshape(b * s, QA)
            outs["pr_s"].append(st_p)
            outs["pr_x"].append(pr[:, s - 1])
            pr, mq_s = _norm_proj(xs, norm_mix[i], pieces, ("none", "none"), [], 256)
            prep = _rwkv_prep_sample(pr, state_shift[ib], wargs)
            mix_s, st_s = _rwkv_step(*prep, *fin, state_rwkv[ib])
            mix_s = mix_s.reshape(bd, QA)
            outs["sr_s"].append(st_s)
            outs["sr_x"].append(pr)
        mem_p = _mem_attn(mq_p.reshape(b, s, MEMQ), km_p, vm_p, mem_q_gain[i], 256).reshape(b * s, MEMQ)
        mem_s = _mem_attn(mq_s.reshape(bd, 1, MEMQ), cache_mem_k[i].reshape(bd, -1, MEMQ),
                          cache_mem_v[i].reshape(bd, -1, MEMQ), mem_q_gain[i], 1).reshape(bd, MEMQ)
        ffn_w = (w_out[i], norm_ffn[i], moe_w_coarse[i], moe_b_coarse[i], moe_w_fine[i], moe_b_fine[i],
                 moe_w_gate[i], moe_w_up[i], moe_w_down[i])
        xp = _ffn(xp, mix_p, mem_p, *ffn_w, 256, 1024)
        xs = _ffn(xs, mix_s, mem_s, *ffn_w, 256, 1024)
    order = ("pc_k", "pc_v", "ps_k", "ps_v", "pw_k", "pw_v", "pm_k", "pm_v", "pr_s", "pr_x",
             "sc_k", "sc_v", "ss_k", "ss_v", "sw_k", "sw_v", "sr_s", "sr_x")
    return (xp.reshape(b, s, d), xs.reshape(bd, 1, d)) + tuple(jnp.stack(outs[name]) for name in order)
```

```python
import functools

import jax
import jax.numpy as jnp
from jax import lax
from jax.experimental import pallas as pl
from jax.experimental.pallas import tpu as pltpu

F32 = jnp.float32
BF16 = jnp.bfloat16
HI = lax.Precision.HIGHEST

HEAD_DIM = 64
TOK_HEADS = 12
MEM_HEADS = 4
KV_GROUPS = 3
HEADS_PER_GROUP = TOK_HEADS // KV_GROUPS
QA = TOK_HEADS * HEAD_DIM
KVW = KV_GROUPS * HEAD_DIM
MEMQ = MEM_HEADS * HEAD_DIM
CMP_LEN = 32
CMP_STRIDE = 16
CMP_HID = 128
SEL_LEN = 64
SEL_TOPK = 16
WINDOW = 512
Q_BLOCK = 128
NSA_KEY_STEP = 512
PAGE = 128
N_GROUPS = 4
E_PER_GROUP = 8
N_EXPERTS = 32
RWKV_COLS = (768, 768, 768, 64, 64, 128)
SHIFT_W = sum(RWKV_COLS)
RWKV_GN_EPS = 64e-5
RWKV_CHUNK = 64
RWKV_CHUNK_EXACT = False
NEG = -1e30
BIG = 1e30
LANES = 128
VMEM_LIMIT = 56 * 1024 * 1024


def _cparams(*sem):
    return pltpu.CompilerParams(dimension_semantics=sem, vmem_limit_bytes=VMEM_LIMIT)


def _dot_general(a, b, dims, exact):
    if exact:
        return lax.dot_general(a, b, (dims, ((), ())), precision=HI, preferred_element_type=F32)
    return lax.dot_general(a.astype(BF16), b.astype(BF16), (dims, ((), ())), preferred_element_type=F32)


def _dot(a, b, exact=False):
    return _dot_general(a, b, ((1,), (0,)), exact)


def _dot_nt(a, b, exact=False):
    return _dot_general(a, b, ((1,), (1,)), exact)


def _dot_tn(a, b, exact=False):
    return _dot_general(a, b, ((0,), (0,)), exact)


def _bf16_round(x):
    return x.astype(BF16).astype(F32)


def _rms(x, g, eps=1e-6):
    return x * lax.rsqrt(jnp.mean(x * x, axis=-1, keepdims=True) + eps) * g


def _masked_softmax(s, mask):
    s = jnp.where(mask, s, NEG)
    m = jnp.max(s, axis=-1, keepdims=True)
    e = jnp.where(mask, jnp.exp(s - m), 0.0)
    return e / jnp.maximum(jnp.sum(e, axis=-1, keepdims=True), 1e-30)


def _alibi_slope(h):
    return 2.0 ** (-8.0 * (h + 1) / TOK_HEADS)


def _topk_mask(score, k):
    lane = lax.broadcasted_iota(jnp.int32, score.shape, 1).astype(F32)
    sel = jnp.zeros(score.shape, F32)
    for _ in range(k):
        m = jnp.max(score, axis=-1, keepdims=True)
        idx = jnp.min(jnp.where(score == m, lane, 1e9), axis=-1, keepdims=True)
        hit = lane == idx
        sel = jnp.where(hit, 1.0, sel)
        score = jnp.where(hit, -jnp.inf, score)
    return sel


def _topk_rows_t(score_t, k):
    n_blk = score_t.shape[0]
    idx = lax.broadcasted_iota(jnp.int32, score_t.shape, 0)
    ahead = jnp.zeros(score_t.shape, F32)
    for j in range(n_blk):
        row = score_t[j:j + 1]
        ahead = ahead + jnp.where((row > score_t) | ((row == score_t) & (idx > j)), 1.0, 0.0)
    return jnp.where(ahead < k, 1.0, 0.0)


def _sel_to_cmp(n_sel_rows, n_cmp_lanes):
    s0 = lax.broadcasted_iota(jnp.int32, (n_sel_rows, n_cmp_lanes), 0) * SEL_LEN
    c0 = lax.broadcasted_iota(jnp.int32, (n_sel_rows, n_cmp_lanes), 1) * CMP_STRIDE
    return ((c0 < s0 + SEL_LEN) & (c0 + CMP_LEN > s0)).astype(F32)


def _cmp_to_sel(n_cmp_rows, n_sel_lanes):
    n_i = lax.broadcasted_iota(jnp.int32, (n_cmp_rows, n_sel_lanes), 0)
    s_i = lax.broadcasted_iota(jnp.int32, (n_cmp_rows, n_sel_lanes), 1)
    c0 = n_i * CMP_STRIDE
    s0 = s_i * SEL_LEN
    return ((c0 < s0 + SEL_LEN) & (c0 + CMP_LEN > s0)).astype(F32)


def _block_expand(n_blk_rows, n_keys):
    b_i = lax.broadcasted_iota(jnp.int32, (n_blk_rows, n_keys), 0)
    k_i = lax.broadcasted_iota(jnp.int32, (n_blk_rows, n_keys), 1)
    return (jnp.right_shift(k_i, 6) == b_i).astype(BF16)


def _norm_proj_kernel(ops, x_ref, g_ref, *refs):
    n = len(ops)
    n_gain = sum(op == "hnorm" for op in ops)
    w_refs, gain_refs, o_refs = refs[:n], refs[n:n + n_gain], refs[n + n_gain:]
    h = _rms(x_ref[...], g_ref[...])
    gi = 0
    for op, w_ref, o_ref in zip(ops, w_refs, o_refs):
        z = _dot(h, w_ref[...])
        if op == "hnorm":
            gain = gain_refs[gi][...]
            gi += 1
            for hh in range(z.shape[1] // HEAD_DIM):
                seg = z[:, hh * HEAD_DIM:(hh + 1) * HEAD_DIM]
                o_ref[:, hh * HEAD_DIM:(hh + 1) * HEAD_DIM] = _rms(seg, gain)
        elif op == "sigmoid":
            o_ref[...] = jax.nn.sigmoid(z)
        else:
            o_ref[...] = z


def _norm_proj(x, g, weights, ops, gains, tm):
    n_rows, d = x.shape
    tm = min(tm, n_rows)
    assert n_rows % tm == 0
    in_specs = [pl.BlockSpec((tm, d), lambda i: (i, 0)), pl.BlockSpec((1, d), lambda i: (0, 0))]
    in_specs += [pl.BlockSpec(w.shape, lambda i: (0, 0)) for w in weights]
    in_specs += [pl.BlockSpec((1, HEAD_DIM), lambda i: (0, 0)) for _ in gains]
    out_shape = [jax.ShapeDtypeStruct((n_rows, w.shape[1]), F32) for w in weights]
    out_specs = [pl.BlockSpec((tm, w.shape[1]), lambda i: (i, 0)) for w in weights]
    return pl.pallas_call(
        functools.partial(_norm_proj_kernel, tuple(ops)),
        grid=(n_rows // tm,), in_specs=in_specs, out_specs=out_specs, out_shape=out_shape,
        compiler_params=_cparams("parallel"), name="norm_proj",
    )(x, g.reshape(1, d), *weights, *[gn.reshape(1, HEAD_DIM) for gn in gains])


def _compress_rows(rows_ref, pe_ref, w1_ref, b1_ref, w2_ref, gain, o_ref, u_ref, n_chunk):
    half = CMP_STRIDE * HEAD_DIM
    for g in range(KV_GROUPS):
        for l in range(CMP_STRIDE):
            src = l * KVW + g * HEAD_DIM
            u_ref[:, l * HEAD_DIM:(l + 1) * HEAD_DIM] = rows_ref[:, src:src + HEAD_DIM]
        u = u_ref[...]
        p0 = _dot(u + pe_ref[:, 0:half], w1_ref[0:half, :])
        p1 = _dot(u + pe_ref[:, half:2 * half], w1_ref[half:2 * half, :])
        hid = b1_ref[...] + p0 + pltpu.roll(p1, n_chunk - 1, 0)
        z = _dot(jax.nn.gelu(hid), w2_ref[...])
        if gain is not None:
            z = _rms(z, gain)
        o_ref[:, g * HEAD_DIM:(g + 1) * HEAD_DIM] = z


def _compress_prompt_kernel(k_ref, v_ref, pe_ref, w1_ref, b1_ref, w2_ref, gain_ref, ck_ref, cv_ref, u_ref,
                            *, n_chunk):
    _compress_rows(k_ref.at[0], pe_ref.at[0], w1_ref.at[0], b1_ref.at[0], w2_ref.at[0], gain_ref[...],
                   ck_ref.at[0], u_ref, n_chunk)
    _compress_rows(v_ref.at[0], pe_ref.at[1], w1_ref.at[1], b1_ref.at[1], w2_ref.at[1], None,
                   cv_ref.at[0], u_ref, n_chunk)


def _compress_weight_specs(nidx):
    zero = lambda *_: (0, 0, 0)
    del nidx
    return [pl.BlockSpec((2, 1, CMP_LEN * HEAD_DIM), zero),
            pl.BlockSpec((2, CMP_LEN * HEAD_DIM, CMP_HID), zero),
            pl.BlockSpec((2, 1, CMP_HID), zero),
            pl.BlockSpec((2, CMP_HID, HEAD_DIM), zero)]


def _compress_prompt(kc, vc, pe, w1, b1, w2, gain):
    b, s, _ = kc.shape
    n_chunk = s // CMP_STRIDE
    cw = CMP_STRIDE * KVW
    rows = pl.BlockSpec((1, n_chunk, cw), lambda i: (i, 0, 0))
    out = pl.BlockSpec((1, n_chunk, KVW), lambda i: (i, 0, 0))
    return pl.pallas_call(
        functools.partial(_compress_prompt_kernel, n_chunk=n_chunk),
        grid=(b,),
        in_specs=[rows, rows] + _compress_weight_specs(1) + [pl.BlockSpec((1, HEAD_DIM), lambda i: (0, 0))],
        out_specs=[out, out],
        out_shape=[jax.ShapeDtypeStruct((b, n_chunk, KVW), F32)] * 2,
        scratch_shapes=[pltpu.VMEM((n_chunk, CMP_STRIDE * HEAD_DIM), F32)],
        compiler_params=_cparams("parallel"), name="compress_prompt",
    )(kc.reshape(b, n_chunk, cw), vc.reshape(b, n_chunk, cw), pe.reshape(2, 1, -1), w1, b1.reshape(2, 1, -1), w2,
      gain.reshape(1, HEAD_DIM))


def _gather_pages(pt_ref, b, pool_ref, dst_ref, sem, n_pages):
    copies = []
    rpp = pool_ref.shape[1]
    for j in range(n_pages):
        cp = pltpu.make_async_copy(pool_ref.at[pt_ref[b, j]], dst_ref.at[pl.ds(j * rpp, rpp)], sem)
        cp.start()
        copies.append(cp)
    return copies


def _compress_sample_kernel(pt_ref, pk_ref, pv_ref, pe_ref, w1_ref, b1_ref, w2_ref, gain_ref,
                            ck_ref, cv_ref, kbuf, vbuf, u_ref, sem, *, n_pages):
    b = pl.program_id(0)
    n_chunk = n_pages * PAGE // CMP_STRIDE
    ck_copies = _gather_pages(pt_ref, b, pk_ref, kbuf, sem.at[0], n_pages)
    cv_copies = _gather_pages(pt_ref, b, pv_ref, vbuf, sem.at[1], n_pages)
    for cp in ck_copies:
        cp.wait()
    _compress_rows(kbuf, pe_ref.at[0], w1_ref.at[0], b1_ref.at[0], w2_ref.at[0], gain_ref[...],
                   ck_ref.at[0], u_ref, n_chunk)
    for cp in cv_copies:
        cp.wait()
    _compress_rows(vbuf, pe_ref.at[1], w1_ref.at[1], b1_ref.at[1], w2_ref.at[1], None,
                   cv_ref.at[0], u_ref, n_chunk)


def _compress_sample(page_table, pool_k, pool_v, pe, w1, b1, w2, gain):
    bd, n_pages = page_table.shape
    n_pool = pool_k.shape[0]
    n_chunk = n_pages * PAGE // CMP_STRIDE
    cw = CMP_STRIDE * KVW
    zero3 = lambda i, pt: (0, 0, 0)
    any_spec = pl.BlockSpec(memory_space=pl.ANY)
    out = pl.BlockSpec((1, n_chunk, KVW), lambda i, pt: (i, 0, 0))
    grid_spec = pltpu.PrefetchScalarGridSpec(
        num_scalar_prefetch=1, grid=(bd,),
        in_specs=[any_spec, any_spec,
                  pl.BlockSpec((2, 1, CMP_LEN * HEAD_DIM), zero3),
                  pl.BlockSpec((2, CMP_LEN * HEAD_DIM, CMP_HID), zero3),
                  pl.BlockSpec((2, 1, CMP_HID), zero3),
                  pl.BlockSpec((2, CMP_HID, HEAD_DIM), zero3),
                  pl.BlockSpec((1, HEAD_DIM), lambda i, pt: (0, 0))],
        out_specs=[out, out],
        scratch_shapes=[pltpu.VMEM((n_chunk, cw), F32), pltpu.VMEM((n_chunk, cw), F32),
                        pltpu.VMEM((n_chunk, CMP_STRIDE * HEAD_DIM), F32), pltpu.SemaphoreType.DMA((2,))])
    return pl.pallas_call(
        functools.partial(_compress_sample_kernel, n_pages=n_pages),
        grid_spec=grid_spec,
        out_shape=[jax.ShapeDtypeStruct((bd, n_chunk, KVW), F32)] * 2,
        compiler_params=_cparams("arbitrary"), name="compress_sample",
    )(page_table, pool_k.reshape(n_pool, PAGE // CMP_STRIDE, cw), pool_v.reshape(n_pool, PAGE // CMP_STRIDE, cw),
      pe.reshape(2, 1, -1), w1, b1.reshape(2, 1, -1), w2, gain.reshape(1, HEAD_DIM))


def _softmax_rows(s):
    e = jnp.exp(s - jnp.max(s, axis=-1, keepdims=True))
    return e / jnp.sum(e, axis=-1, keepdims=True)


def _nsa_prompt_block(q_ref, gt_ref, ck_ref, cv_ref, ks_ref, vs_ref, kw_ref, vw_ref, o_ref, *, seq, n_keys):
    tq = Q_BLOCK
    hpg = HEADS_PER_GROUP
    n_cmp = ck_ref.shape[1]
    n_sel = seq // SEL_LEN
    wlen = min(WINDOW + tq, seq)
    q0 = pl.program_id(1) * tq
    t1 = q0 + lax.broadcasted_iota(jnp.int32, (tq, 1), 0)
    s2c = _sel_to_cmp(n_sel, n_cmp)
    expand = _block_expand(n_sel, n_keys)
    blk = lax.broadcasted_iota(jnp.int32, (n_sel, 1), 0)
    t_row = q0 + lax.broadcasted_iota(jnp.int32, (1, tq), 1)
    cur = jnp.right_shift(t_row, 6)
    forced = (blk == 0) | (blk == cur) | (blk == cur - 1)
    valid = blk * SEL_LEN <= t_row
    dist_c = t1 - (lax.broadcasted_iota(jnp.int32, (1, n_cmp), 1) * CMP_STRIDE + (CMP_LEN - 1))
    mask_c = dist_c >= 0
    dist_cf = dist_c.astype(F32)
    dist_s = t1 - lax.broadcasted_iota(jnp.int32, (1, n_keys), 1)
    causal_s = dist_s >= 0
    dist_sf = dist_s.astype(F32)
    w_start = pl.multiple_of(jnp.clip(q0 - WINDOW, 0, seq - wlen), tq)
    dist_w = t1 - (w_start + lax.broadcasted_iota(jnp.int32, (1, wlen), 1))
    bias_w = jnp.where((dist_w >= 0) & (dist_w <= WINDOW), 0.0, NEG)
    dist_wf = dist_w.astype(F32)
    gt = gt_ref[0]
    for g in range(KV_GROUPS):
        lanes = slice(g * HEAD_DIM, (g + 1) * HEAD_DIM)
        heads = [g * hpg + j for j in range(hpg)]
        rows = [slice(j * tq, (j + 1) * tq) for j in range(hpg)]
        q4 = jnp.concatenate([q_ref[0, :, h * HEAD_DIM:(h + 1) * HEAD_DIM] for h in heads],
                             axis=0) * (HEAD_DIM ** -0.5)
        s_c = _dot_nt(q4, ck_ref[0, :, lanes])
        p_c = [_masked_softmax(s_c[rows[j]] - _alibi_slope(heads[j]) * dist_cf, mask_c) for j in range(hpg)]
        cv = cv_ref[0, :, lanes]
        o_c = [_dot(p, cv) for p in p_c]
        imps = [_dot_nt(s2c, p) for p in p_c]
        imp = imps[0]
        for j in range(1, hpg):
            imp = imp + imps[j]
        score = jnp.where(valid, jnp.where(forced, BIG, imp), -BIG)
        key_sel = _dot_tn(_topk_rows_t(score, min(SEL_TOPK, n_sel)), expand)
        bias_s = jnp.where((key_sel > 0.5) & causal_s, 0.0, NEG)
        s_s = _dot_nt(q4, ks_ref[0, 0:n_keys, lanes])
        vs = vs_ref[0, 0:n_keys, lanes]
        o_s = [_dot(_softmax_rows(s_s[rows[j]] - _alibi_slope(heads[j]) * dist_sf + bias_s), vs)
               for j in range(hpg)]
        s_w = _dot_nt(q4, kw_ref[0, pl.ds(w_start, wlen), lanes])
        vw = vw_ref[0, pl.ds(w_start, wlen), lanes]
        o_w = [_dot(_softmax_rows(s_w[rows[j]] - _alibi_slope(heads[j]) * dist_wf + bias_w), vw)
               for j in range(hpg)]
        for j, h in enumerate(heads):
            o_ref[0, :, h * HEAD_DIM:(h + 1) * HEAD_DIM] = (
                gt[:, 3 * h:3 * h + 1] * o_c[j] + gt[:, 3 * h + 1:3 * h + 2] * o_s[j]
                + gt[:, 3 * h + 2:3 * h + 3] * o_w[j])


def _nsa_prompt_kernel(*refs, seq, key_step):
    per_class = key_step // Q_BLOCK
    cls = pl.program_id(1) // per_class
    for c in range(seq // key_step):
        pl.when(cls == c)(functools.partial(_nsa_prompt_block, *refs, seq=seq, n_keys=(c + 1) * key_step))


def _nsa_prompt(q, gates, ck, cv, ks, vs, kw, vw):
    b, s, _ = q.shape
    n_cmp = ck.shape[1]
    qspec = lambda w: pl.BlockSpec((1, Q_BLOCK, w), lambda i, j: (i, j, 0))
    full = lambda n: pl.BlockSpec((1, n, KVW), lambda i, j: (i, 0, 0))
    return pl.pallas_call(
        functools.partial(_nsa_prompt_kernel, seq=s, key_step=min(NSA_KEY_STEP, s)),
        grid=(b, s // Q_BLOCK),
        in_specs=[qspec(QA), qspec(LANES), full(n_cmp), full(n_cmp), full(s), full(s), full(s), full(s)],
        out_specs=qspec(QA),
        out_shape=jax.ShapeDtypeStruct((b, s, QA), F32),
        compiler_params=_cparams("parallel", "parallel"), name="nsa_prompt",
    )(q, gates, ck, cv, ks, vs, kw, vw)


def _gather_pages_t(pt_ref, b, pool_ref, dst_ref, sem, n_pages):
    copies = []
    for j in range(n_pages):
        cp = pltpu.make_async_copy(pool_ref.at[pt_ref[b, j]], dst_ref.at[:, :, pl.ds(j * PAGE, PAGE)], sem)
        cp.start()
        copies.append(cp)
    return copies


def _nsa_sample_kernel(pt_ref, q_ref, gt_ref, ck_ref, cv_ref, ksn_ref, vsn_ref, kwn_ref, vwn_ref,
                       pks_ref, pvs_ref, wk_ref, wv_ref, o_ref, owk_ref, owv_ref, kbuf, vbuf, sem,
                       *, n_pages):
    b = pl.program_id(0)
    hpg = HEADS_PER_GROUP
    past = n_pages * PAGE
    n_cmp = ck_ref.shape[1]
    n_sel_past = past // SEL_LEN
    n_sel = n_sel_past + 1
    wb = wk_ref.shape[3]
    sel_lanes = 2 * LANES
    k_copies = _gather_pages_t(pt_ref, b, pks_ref, kbuf, sem.at[0], n_pages)
    v_copies = _gather_pages_t(pt_ref, b, pvs_ref, vbuf, sem.at[1], n_pages)

    row = lax.broadcasted_iota(jnp.int32, (8, 1), 0)
    n = HEAD_DIM
    eye = (lax.broadcasted_iota(jnp.int32, (n, n), 0) == lax.broadcasted_iota(jnp.int32, (n, n), 1)).astype(F32)
    gt = gt_ref[0]
    c_end = lax.broadcasted_iota(jnp.int32, (1, n_cmp), 1) * CMP_STRIDE + (CMP_LEN - 1)
    dist_c = past - c_end
    blk = lax.broadcasted_iota(jnp.int32, (1, sel_lanes), 1)
    cur = past // SEL_LEN
    forced = (blk == 0) | (blk == cur) | (blk == cur - 1)
    valid = blk * SEL_LEN <= past
    c2s = _cmp_to_sel(n_cmp, sel_lanes)

    def q_rows(g):
        q4 = jnp.concatenate(
            [q_ref[0, :, (g * hpg + j) * HEAD_DIM:(g * hpg + j + 1) * HEAD_DIM] for j in range(hpg)]
            + [jnp.zeros((8 - hpg, HEAD_DIM), F32)], axis=0) * (HEAD_DIM ** -0.5)
        slope = jnp.full((8, 1), _alibi_slope(g * hpg + hpg - 1), F32)
        for j in range(hpg - 2, -1, -1):
            slope = jnp.where(row < j + 1, _alibi_slope(g * hpg + j), slope)
        return q4, slope

    o_cs, imps = [], []
    for g in range(KV_GROUPS):
        lanes = slice(g * HEAD_DIM, (g + 1) * HEAD_DIM)
        q4, slope = q_rows(g)
        p_c = _masked_softmax(_dot_nt(q4, ck_ref[0, :, lanes]) - slope * dist_c.astype(F32), dist_c >= 0)
        o_cs.append(_dot(p_c, cv_ref[0, :, lanes]))
        imps.append(jnp.sum(jnp.where(row < hpg, _dot(p_c, c2s), 0.0), axis=0, keepdims=True))
    imp = jnp.concatenate(imps + [jnp.zeros((8 - KV_GROUPS, sel_lanes), F32)], axis=0)
    score = jnp.where(valid, jnp.where(forced, BIG, imp), -BIG)
    score = jnp.where(blk < n_sel, score, -jnp.inf)
    sel = _topk_mask(score, min(SEL_TOPK, n_sel))
    key_sel = _dot(sel[:, 0:n_sel_past], _block_expand(n_sel_past, past))
    sel_new = sel[:, n_sel_past:n_sel_past + 1]

    pos_s = lax.broadcasted_iota(jnp.int32, (1, past), 1)
    dist_s = (past - pos_s).astype(F32)
    lane_w = lax.broadcasted_iota(jnp.int32, (1, wb), 1)
    pos_w = past - wb + lane_w
    dist_w = past - pos_w
    mask_w = (dist_w >= 0) & (dist_w <= WINDOW) & (pos_w >= 0)
    for cp in k_copies:
        cp.wait()
    for cp in v_copies:
        cp.wait()
    for g in range(KV_GROUPS):
        lanes = slice(g * HEAD_DIM, (g + 1) * HEAD_DIM)
        q4, slope = q_rows(g)
        q4r = _bf16_round(q4)
        s_p = jnp.where(key_sel[g:g + 1] > 0.5, _dot(q4, kbuf[g]) - slope * dist_s, NEG)
        new_ok = sel_new[g:g + 1] > 0.5
        s_n = jnp.where(new_ok, jnp.sum(q4r * _bf16_round(ksn_ref[0, :, lanes]), axis=-1, keepdims=True), NEG)
        m = jnp.maximum(jnp.max(s_p, axis=-1, keepdims=True), s_n)
        e_p = jnp.where(key_sel[g:g + 1] > 0.5, jnp.exp(s_p - m), 0.0)
        e_n = jnp.where(new_ok, jnp.exp(s_n - m), 0.0)
        den = jnp.maximum(jnp.sum(e_p, axis=-1, keepdims=True) + e_n, 1e-30)
        o_s = _dot_nt(e_p / den, vbuf[g]) + _bf16_round(e_n / den) * _bf16_round(vsn_ref[0, :, lanes])
        s_p = jnp.where(mask_w, _dot(q4, wk_ref[0, g]) - slope * dist_w.astype(F32), NEG)
        s_n = jnp.sum(q4r * _bf16_round(kwn_ref[0, :, lanes]), axis=-1, keepdims=True)
        m = jnp.maximum(jnp.max(s_p, axis=-1, keepdims=True), s_n)
        e_p = jnp.where(mask_w, jnp.exp(s_p - m), 0.0)
        e_n = jnp.exp(s_n - m)
        den = jnp.maximum(jnp.sum(e_p, axis=-1, keepdims=True) + e_n, 1e-30)
        o_w = _dot_nt(e_p / den, wv_ref[0, g]) + _bf16_round(e_n / den) * _bf16_round(vwn_ref[0, :, lanes])
        for j in range(hpg):
            h = g * hpg + j
            o_ref[0, :, h * HEAD_DIM:(h + 1) * HEAD_DIM] = (
                gt[:, 3 * h:3 * h + 1] * o_cs[g][j:j + 1] + gt[:, 3 * h + 1:3 * h + 2] * o_s[j:j + 1]
                + gt[:, 3 * h + 2:3 * h + 3] * o_w[j:j + 1])
        k_col = jnp.sum(eye * kwn_ref[0, :, lanes], axis=1, keepdims=True)
        v_col = jnp.sum(eye * vwn_ref[0, :, lanes], axis=1, keepdims=True)
        owk_ref[0, g] = jnp.where(lane_w == wb - 1, k_col, pltpu.roll(wk_ref[0, g], wb - 1, 1))
        owv_ref[0, g] = jnp.where(lane_w == wb - 1, v_col, pltpu.roll(wv_ref[0, g], wb - 1, 1))


def _nsa_sample(page_table, q, gates, ck, cv, ks_new, vs_new, kw_new, vw_new, pool_ks, pool_vs, win_k, win_v):
    bd, n_pages = page_table.shape
    n_cmp = ck.shape[1]
    wb = win_k.shape[1]
    one = lambda w: pl.BlockSpec((1, 1, w), lambda i, pt: (i, 0, 0))
    rows = lambda m: pl.BlockSpec((1, m, KVW), lambda i, pt: (i, 0, 0))
    win = pl.BlockSpec((1, KV_GROUPS, HEAD_DIM, wb), lambda i, pt: (i, 0, 0, 0))
    any_spec = pl.BlockSpec(memory_space=pl.ANY)
    grid_spec = pltpu.PrefetchScalarGridSpec(
        num_scalar_prefetch=1, grid=(bd,),
        in_specs=[one(QA), one(LANES), rows(n_cmp), rows(n_cmp), one(KVW), one(KVW), one(KVW), one(KVW),
                  any_spec, any_spec, win, win],
        out_specs=[one(QA), win, win],
        scratch_shapes=[pltpu.VMEM((KV_GROUPS, HEAD_DIM, n_pages * PAGE), F32),
                        pltpu.VMEM((KV_GROUPS, HEAD_DIM, n_pages * PAGE), F32),
                        pltpu.SemaphoreType.DMA((2,))])
    r3 = lambda z: z.reshape(bd, 1, -1)
    pos_minor = lambda z: jnp.transpose(z, (0, 2, 3, 1))
    win_shape = jax.ShapeDtypeStruct((bd, KV_GROUPS, HEAD_DIM, wb), F32)
    mix, nwk, nwv = pl.pallas_call(
        functools.partial(_nsa_sample_kernel, n_pages=n_pages),
        grid_spec=grid_spec,
        out_shape=[jax.ShapeDtypeStruct((bd, 1, QA), F32), win_shape, win_shape],
        compiler_params=_cparams("arbitrary"), name="nsa_sample",
    )(page_table, r3(q), r3(gates), ck, cv, r3(ks_new), r3(vs_new), r3(kw_new), r3(vw_new),
      pos_minor(pool_ks), pos_minor(pool_vs), pos_minor(win_k), pos_minor(win_v))
    back = lambda z: jnp.transpose(z, (0, 3, 1, 2))
    return mix, back(nwk), back(nwv)


def _mem_attn_kernel(q_ref, k_ref, v_ref, gain_ref, o_ref):
    tq = q_ref.shape[1]
    pad = max(8 - tq, 0)
    for h in range(MEM_HEADS):
        lanes = slice(h * HEAD_DIM, (h + 1) * HEAD_DIM)
        q = _rms(q_ref[0, :, lanes], gain_ref[...]) * (HEAD_DIM ** -0.5)
        if pad:
            q = jnp.concatenate([q, jnp.zeros((pad, HEAD_DIM), F32)], axis=0)
        s = _dot_nt(q, k_ref[0, :, lanes])
        e = jnp.exp(s - jnp.max(s, axis=-1, keepdims=True))
        p = e / jnp.sum(e, axis=-1, keepdims=True)
        o_ref[0, :, lanes] = _dot(p, v_ref[0, :, lanes])[0:tq]


def _mem_attn(mq, km, vm, gain, tq):
    b, t, _ = mq.shape
    m = km.shape[1]
    tq = min(tq, t)
    qspec = pl.BlockSpec((1, tq, MEMQ), lambda i, j: (i, j, 0))
    kspec = pl.BlockSpec((1, m, MEMQ), lambda i, j: (i, 0, 0))
    return pl.pallas_call(
        _mem_attn_kernel, grid=(b, t // tq),
        in_specs=[qspec, kspec, kspec, pl.BlockSpec((1, HEAD_DIM), lambda i, j: (0, 0))],
        out_specs=qspec, out_shape=jax.ShapeDtypeStruct((b, t, MEMQ), F32),
        compiler_params=_cparams("parallel", "parallel"), name="mem_attn",
    )(mq, km, vm, gain.reshape(1, HEAD_DIM))


def _outproj_router_kernel(x_ref, mix_ref, mem_ref, wo_ref, g_ref, wc_ref, bc_ref, wf_ref, bf_ref,
                           xo_ref, hn_ref, cw_ref):
    x = x_ref[...] + _dot(mix_ref[...], wo_ref[0:QA, :]) + _dot(mem_ref[...], wo_ref[QA:QA + MEMQ, :])
    xo_ref[...] = x
    hn = _rms(x, g_ref[...])
    hn_ref[...] = hn
    lane = lax.broadcasted_iota(jnp.int32, (1, LANES), 1).astype(F32)
    lg = jnp.where(lane < N_GROUPS, _dot(hn, wc_ref[...]) + bc_ref[...], -jnp.inf)
    m = jnp.max(lg, axis=-1, keepdims=True)
    grp = jnp.min(jnp.where(lg == m, lane, 1e9), axis=-1, keepdims=True)
    p_grp = 1.0 / jnp.sum(jnp.exp(lg - m), axis=-1, keepdims=True)
    in_grp = (lane >= grp * E_PER_GROUP) & (lane < (grp + 1.0) * E_PER_GROUP)
    lf = jnp.where(in_grp, _dot(hn, wf_ref[...]) + bf_ref[...], -jnp.inf)
    v1 = jnp.max(lf, axis=-1, keepdims=True)
    i1 = jnp.min(jnp.where(lf == v1, lane, 1e9), axis=-1, keepdims=True)
    lf2 = jnp.where(lane == i1, -jnp.inf, lf)
    v2 = jnp.max(lf2, axis=-1, keepdims=True)
    i2 = jnp.min(jnp.where(lf2 == v2, lane, 1e9), axis=-1, keepdims=True)
    e2 = jnp.exp(v2 - v1)
    den = 1.0 + e2
    cw_ref[...] = jnp.where(lane == i1, p_grp / den, 0.0) + jnp.where(lane == i2, p_grp * (e2 / den), 0.0)


def _outproj_router(x, mix, mem, w_out, g_ffn, w_coarse, b_coarse, w_fine, b_fine, tm):
    n, d = x.shape
    tm = min(tm, n)
    pad_w = lambda w: jnp.pad(w, ((0, 0), (0, LANES - w.shape[1])))
    pad_b = lambda v: jnp.pad(v, (0, LANES - v.shape[0])).reshape(1, LANES)
    row = lambda w: pl.BlockSpec((tm, w), lambda i: (i, 0))
    const = lambda r, c: pl.BlockSpec((r, c), lambda i: (0, 0))
    return pl.pallas_call(
        _outproj_router_kernel, grid=(n // tm,),
        in_specs=[row(d), row(QA), row(MEMQ), const(QA + MEMQ, d), const(1, d), const(d, LANES),
                  const(1, LANES), const(d, LANES), const(1, LANES)],
        out_specs=[row(d), row(d), row(LANES)],
        out_shape=[jax.ShapeDtypeStruct((n, d), F32), jax.ShapeDtypeStruct((n, d), F32),
                   jax.ShapeDtypeStruct((n, LANES), F32)],
        compiler_params=_cparams("parallel"), name="outproj_router",
    )(x, mix, mem, w_out, g_ffn.reshape(1, d), pad_w(w_coarse), pad_b(b_coarse), pad_w(w_fine), pad_b(b_fine))


def _moe_kernel(x_ref, hn_ref, cw_ref, wg_ref, wu_ref, wd_ref, o_ref):
    e = pl.program_id(1)

    @pl.when(e == 0)
    def _():
        o_ref[...] = x_ref[...]

    hb = hn_ref[...].astype(BF16)
    gate = _dot(hb, wg_ref[0])
    up = _dot(hb, wu_ref[0])
    lane = lax.broadcasted_iota(jnp.int32, (1, LANES), 1)
    c = jnp.sum(jnp.where(lane == e, cw_ref[...], 0.0), axis=-1, keepdims=True)
    o_ref[...] += _dot(gate * jax.nn.sigmoid(gate) * up, wd_ref[0]) * c


def _moe(x, hn, cw, w_gate, w_up, w_down, tm):
    n, d = x.shape
    tm = min(tm, n)
    n_exp, _, d_exp = w_gate.shape
    row = lambda w: pl.BlockSpec((tm, w), lambda i, e: (i, 0))
    return pl.pallas_call(
        _moe_kernel, grid=(n // tm, n_exp),
        in_specs=[row(d), row(d), row(LANES),
                  pl.BlockSpec((1, d, d_exp), lambda i, e: (e, 0, 0)),
                  pl.BlockSpec((1, d, d_exp), lambda i, e: (e, 0, 0)),
                  pl.BlockSpec((1, d_exp, d), lambda i, e: (e, 0, 0))],
        out_specs=row(d), out_shape=jax.ShapeDtypeStruct((n, d), F32),
        compiler_params=_cparams("parallel", "arbitrary"), name="moe",
    )(x, hn, cw, w_gate, w_up, w_down)


def _rwkv_prep_body(p, prev, mu_ref, w0_ref, wup_ref, a0_ref, aup_ref, gup_ref, kk_ref, ka_ref, outs):
    r_ref, ld_ref, k_ref, v_ref, kkn_ref, a_ref, g_ref = outs
    x = p + mu_ref[...] * (prev - p)
    c0, c1, c2, c3, c4 = 768, 1536, 2304, 2368, 2432
    r, k, v = x[:, 0:c0], x[:, c0:c1], x[:, c1:c2]
    xw, xa, xg = x[:, c2:c3], x[:, c3:c4], x[:, c4:SHIFT_W]
    z = -(w0_ref[...] + _dot(jnp.tanh(xw), wup_ref[...]))
    softplus = jnp.maximum(z, 0.0) + jnp.log(1.0 + jnp.exp(-jnp.abs(z)))
    w = -softplus - 0.5
    a = jax.nn.sigmoid(a0_ref[...] + _dot(xa, aup_ref[...]))
    r_ref[...] = r
    ld_ref[...] = -jnp.exp(w)
    k_ref[...] = k * (1.0 + (a - 1.0) * ka_ref[...])
    v_ref[...] = v
    a_ref[...] = a
    g_ref[...] = _dot(jax.nn.sigmoid(xg), gup_ref[...])
    kk = k * kk_ref[...]
    for h in range(TOK_HEADS):
        lanes = slice(h * HEAD_DIM, (h + 1) * HEAD_DIM)
        seg = kk[:, lanes]
        nrm = jnp.sqrt(jnp.sum(seg * seg, axis=-1, keepdims=True))
        kkn_ref[:, lanes] = seg / jnp.maximum(nrm, 1e-12)


def _rwkv_prep_prompt_kernel(p_ref, mu_ref, w0_ref, wup_ref, a0_ref, aup_ref, gup_ref, kk_ref, ka_ref,
                             *rest):
    outs, carry = rest[:7], rest[7]
    tm = p_ref.shape[1]

    @pl.when(pl.program_id(1) == 0)
    def _():
        carry[...] = jnp.zeros_like(carry)

    p = p_ref[0]
    first = lax.broadcasted_iota(jnp.int32, (tm, 1), 0) == 0
    prev = jnp.where(first, carry[...], pltpu.roll(p, 1, 0))
    carry[...] = p[tm - 1:tm]
    _rwkv_prep_body(p, prev, mu_ref, w0_ref, wup_ref, a0_ref, aup_ref, gup_ref, kk_ref, ka_ref,
                    [o.at[0] for o in outs])


def _rwkv_prep_sample_kernel(p_ref, prev_ref, mu_ref, w0_ref, wup_ref, a0_ref, aup_ref, gup_ref, kk_ref,
                             ka_ref, *outs):
    _rwkv_prep_body(p_ref[...], prev_ref[...], mu_ref, w0_ref, wup_ref, a0_ref, aup_ref, gup_ref, kk_ref,
                    ka_ref, outs)


def _rwkv_weight_args(mu, w0, w_up, a0, a_up, g_up, k_k, k_a):
    row = lambda v: v.reshape(1, -1)
    return (row(mu), row(w0), w_up, row(a0), a_up, g_up, row(k_k), row(k_a))


def _rwkv_prep_prompt(pr, wargs, tm):
    b, s, _ = pr.shape
    tm = min(tm, s)
    const = lambda a: pl.BlockSpec(a.shape, lambda i, j: (0, 0))
    ospec = pl.BlockSpec((1, tm, QA), lambda i, j: (i, j, 0))
    return pl.pallas_call(
        _rwkv_prep_prompt_kernel, grid=(b, s // tm),
        in_specs=[pl.BlockSpec((1, tm, SHIFT_W), lambda i, j: (i, j, 0))] + [const(a) for a in wargs],
        out_specs=[ospec] * 7, out_shape=[jax.ShapeDtypeStruct((b, s, QA), F32)] * 7,
        scratch_shapes=[pltpu.VMEM((1, SHIFT_W), F32)],
        compiler_params=_cparams("parallel", "arbitrary"), name="rwkv_prep_prompt",
    )(pr, *wargs)


def _rwkv_prep_sample(pr, prev, wargs):
    n = pr.shape[0]
    full = lambda a: pl.BlockSpec(a.shape, lambda i: (0, 0))
    return pl.pallas_call(
        _rwkv_prep_sample_kernel, grid=(1,),
        in_specs=[full(pr), full(prev)] + [full(a) for a in wargs],
        out_specs=[pl.BlockSpec((n, QA), lambda i: (0, 0))] * 7,
        out_shape=[jax.ShapeDtypeStruct((n, QA), F32)] * 7,
        compiler_params=_cparams("arbitrary"), name="rwkv_prep_sample",
    )(pr, prev, *wargs)


def _rwkv_finish(y, r, k, v, g, rk, lnw, lnb):
    m = jnp.mean(y, axis=-1, keepdims=True)
    var = jnp.mean(jnp.square(y - m), axis=-1, keepdims=True)
    yn = (y - m) * lax.rsqrt(var + RWKV_GN_EPS) * lnw + lnb
    bonus = jnp.sum(r * k * rk, axis=-1, keepdims=True) * v
    return (yn + bonus) * g


def _rwkv_chunk_kernel(r_ref, ld_ref, k_ref, v_ref, kk_ref, a_ref, g_ref, rk_ref, lnw_ref, lnb_ref,
                       o_ref, st_ref, z_ref):
    c = r_ref.shape[1]
    ci = pl.program_id(1)

    @pl.when(ci == 0)
    def _():
        z_ref[...] = jnp.zeros_like(z_ref)

    ri = lax.broadcasted_iota(jnp.int32, (c, c), 0)
    cj = lax.broadcasted_iota(jnp.int32, (c, c), 1)
    incl = ri >= cj
    strict = ri > cj
    n = HEAD_DIM
    eye = (lax.broadcasted_iota(jnp.int32, (n, n), 0) == lax.broadcasted_iota(jnp.int32, (n, n), 1)).astype(F32)
    ex = RWKV_CHUNK_EXACT
    heads = range(TOK_HEADS)
    hl = [slice(h * n, (h + 1) * n) for h in heads]
    ld, r_all, k_all, v_all, kk_all = ld_ref[0], r_ref[0], k_ref[0], v_ref[0], kk_ref[0]
    lc = _dot(incl.astype(F32), ld, exact=True)
    l_end = lc[c - 1:c]
    b_all = kk_all * a_ref[0]
    e_neg = jnp.exp(-lc)
    e_rem = jnp.exp(l_end - lc)
    at_all = -kk_all * jnp.exp(lc - ld)
    rt_all = r_all * jnp.exp(lc)
    bt_all, kt_all = b_all * e_neg, k_all * e_neg
    bh_all, kh_all = b_all * e_rem, k_all * e_rem
    p_end = jnp.exp(l_end)
    v = [v_all[:, s] for s in hl]
    big = [_dot_nt(jnp.concatenate([at_all[:, s], rt_all[:, s]], axis=0),
                   jnp.concatenate([bt_all[:, s], kt_all[:, s]], axis=0), ex) for s in hl]
    a_ab = [jnp.where(strict, m[0:c, 0:c], 0.0) for m in big]
    a_rbk = [jnp.concatenate([jnp.where(incl, m[c:2 * c, 0:c], 0.0), jnp.where(incl, m[c:2 * c, c:2 * c], 0.0)],
                             axis=1) for m in big]
    akv = [_dot(jnp.where(strict, big[h][0:c, c:2 * c], 0.0), v[h], ex) for h in heads]
    x = [jnp.concatenate([at_all[:, hl[h]], akv[h]], axis=1) for h in heads]
    npow = a_ab
    x = [x[h] + _dot(npow[h], x[h], ex) for h in heads]
    steps = 1
    while 2 * steps < c:
        npow = [_dot(m, m, ex) for m in npow]
        x = [x[h] + _dot(npow[h], x[h], ex) for h in heads]
        steps *= 2
    zeros = jnp.zeros((c, n), F32)
    wv = [jnp.concatenate([x[h], jnp.concatenate([zeros, v[h]], axis=1)], axis=0) for h in heads]
    rq_y0 = [_dot(a_rbk[h], wv[h], ex) for h in heads]
    m_n = [_dot_tn(jnp.concatenate([bh_all[:, hl[h]], kh_all[:, hl[h]]], axis=0), wv[h], ex)
           for h in heads]
    yz = [_dot(jnp.concatenate([rt_all[:, hl[h]] + rq_y0[h][:, 0:n],
                                m_n[h][:, 0:n] + eye * p_end[:, hl[h]]], axis=0), z_ref[h], ex)
          for h in heads]
    for h in heads:
        z_ref[h] = yz[h][c:c + n] + m_n[h][:, n:2 * n]
        y = yz[h][0:c] + rq_y0[h][:, n:2 * n]
        o_ref[0, :, hl[h]] = _rwkv_finish(y, r_all[:, hl[h]], k_all[:, hl[h]], v[h], g_ref[0, :, hl[h]],
                                          rk_ref[:, hl[h]], lnw_ref[:, hl[h]], lnb_ref[:, hl[h]])

    @pl.when(ci == pl.num_programs(1) - 1)
    def _():
        for h in range(TOK_HEADS):
            st_ref[0, h] = z_ref[h].T


def _rwkv_chunk(r, ld, k, v, kk, a, g, r_k, ln_w, ln_b):
    b, s, _ = r.shape
    c = RWKV_CHUNK
    tok = pl.BlockSpec((1, c, QA), lambda i, j: (i, j, 0))
    const = pl.BlockSpec((1, QA), lambda i, j: (0, 0))
    return pl.pallas_call(
        _rwkv_chunk_kernel, grid=(b, s // c),
        in_specs=[tok] * 7 + [const] * 3,
        out_specs=[tok, pl.BlockSpec((1, TOK_HEADS, HEAD_DIM, HEAD_DIM), lambda i, j: (i, 0, 0, 0))],
        out_shape=[jax.ShapeDtypeStruct((b, s, QA), F32),
                   jax.ShapeDtypeStruct((b, TOK_HEADS, HEAD_DIM, HEAD_DIM), F32)],
        scratch_shapes=[pltpu.VMEM((TOK_HEADS, HEAD_DIM, HEAD_DIM), F32)],
        compiler_params=_cparams("parallel", "arbitrary"), name="rwkv_chunk",
    )(r, ld, k, v, kk, a, g, r_k.reshape(1, QA), ln_w.reshape(1, QA), ln_b.reshape(1, QA))


def _rwkv_step_kernel(r_ref, ld_ref, k_ref, v_ref, kk_ref, a_ref, g_ref, rk_ref, lnw_ref, lnb_ref, s_ref,
                      o_ref, so_ref):
    n = HEAD_DIM
    eye = (lax.broadcasted_iota(jnp.int32, (n, n), 0) == lax.broadcasted_iota(jnp.int32, (n, n), 1)).astype(F32)
    for h in range(TOK_HEADS):
        lanes = slice(h * n, (h + 1) * n)
        r, k, v, kk, a = r_ref[0, :, lanes], k_ref[0, :, lanes], v_ref[0, :, lanes], kk_ref[0, :, lanes], \
            a_ref[0, :, lanes]
        d = jnp.exp(ld_ref[0, :, lanes])
        s = s_ref[0, h]
        sa = jnp.sum(_bf16_round(s) * _bf16_round(-kk), axis=1, keepdims=True)
        v_col = jnp.sum(eye * v, axis=1, keepdims=True)
        s_new = s * d + sa * (kk * a) + v_col * k
        so_ref[0, h] = s_new
        y_col = jnp.sum(_bf16_round(s_new) * _bf16_round(r), axis=1, keepdims=True)
        y = jnp.sum(eye * y_col, axis=0, keepdims=True)
        o_ref[0, :, lanes] = _rwkv_finish(y, r, k, v, g_ref[0, :, lanes], rk_ref[:, lanes], lnw_ref[:, lanes],
                                          lnb_ref[:, lanes])


def _rwkv_step(r, ld, k, v, kk, a, g, r_k, ln_w, ln_b, state):
    n = r.shape[0]
    tok = pl.BlockSpec((1, 1, QA), lambda i: (i, 0, 0))
    const = pl.BlockSpec((1, QA), lambda i: (0, 0))
    st = pl.BlockSpec((1, TOK_HEADS, HEAD_DIM, HEAD_DIM), lambda i: (i, 0, 0, 0))
    r3 = lambda z: z.reshape(n, 1, QA)
    return pl.pallas_call(
        _rwkv_step_kernel, grid=(n,),
        in_specs=[tok] * 7 + [const] * 3 + [st],
        out_specs=[tok, st],
        out_shape=[jax.ShapeDtypeStruct((n, 1, QA), F32), jax.ShapeDtypeStruct(state.shape, F32)],
        compiler_params=_cparams("parallel"), name="rwkv_step",
    )(r3(r), r3(ld), r3(k), r3(v), r3(kk), r3(a), r3(g), r_k.reshape(1, QA), ln_w.reshape(1, QA),
      ln_b.reshape(1, QA), state)


def _split_w_in_a(w):
    offs = [0, QA]
    for _ in range(6):
        offs.append(offs[-1] + KVW)
    offs.append(offs[-1] + 3 * TOK_HEADS)
    offs.append(offs[-1] + MEMQ)
    pieces = [w[:, offs[i]:offs[i + 1]] for i in range(9)]
    pieces[7] = jnp.pad(pieces[7], ((0, 0), (0, LANES - 3 * TOK_HEADS)))
    return pieces


_A_OPS = ("hnorm", "none", "none", "hnorm", "none", "hnorm", "none", "sigmoid", "none")


def _ffn(x, mix, mem, w_out, g_ffn, wc, bc, wf, bf, w_gate, w_up, w_down, tm_proj, tm_moe):
    x_new, hn, cw = _outproj_router(x, mix, mem, w_out, g_ffn, wc, bc, wf, bf, tm_proj)
    return _moe(x_new, hn, cw, w_gate, w_up, w_down, tm_moe)


def kernel(x_prompt, x_sample, cache_cmp_k, cache_cmp_v, cache_sel_k, cache_sel_v, cache_win_k, cache_win_v, cache_mem_k, cache_mem_v, state_rwkv, state_shift, page_table, mem_prompt, norm_mix, norm_ffn, norm_mem, w_mem_kv, mem_q_gain, mem_k_gain, w_in_a, nsa_q_gain, nsa_k_gain, cmp_pe, cmp_w1, cmp_b1, cmp_w2, w_in_b, rwkv_mu, rwkv_w0, rwkv_w_up, rwkv_a0, rwkv_a_up, rwkv_g_up, rwkv_k_k, rwkv_k_a, rwkv_r_k, rwkv_ln_w, rwkv_ln_b, w_out, moe_w_coarse, moe_b_coarse, moe_w_fine, moe_b_fine, moe_w_gate, moe_w_up, moe_w_down):
    b, s, d = x_prompt.shape
    bd = x_sample.shape[0]
    depth = norm_mix.shape[0]
    m_len = mem_prompt.shape[1]
    wl = min(WINDOW, s)
    xp = x_prompt.reshape(b * s, d)
    xs = x_sample.reshape(bd, d)
    mem2 = mem_prompt.reshape(b * m_len, d)
    outs = {name: [] for name in ("pc_k", "pc_v", "ps_k", "ps_v", "pw_k", "pw_v", "pm_k", "pm_v", "pr_s", "pr_x",
                                  "sc_k", "sc_v", "ss_k", "ss_v", "sw_k", "sw_v", "sr_s", "sr_x")}
    for i in range(depth):
        km_p, vm_p = _norm_proj(mem2, norm_mem[i], [w_mem_kv[i][:, :MEMQ], w_mem_kv[i][:, MEMQ:]],
                                ("hnorm", "none"), [mem_k_gain[i]], 256)
        km_p, vm_p = km_p.reshape(b, m_len, MEMQ), vm_p.reshape(b, m_len, MEMQ)
        outs["pm_k"].append(km_p.reshape(b, m_len, MEM_HEADS, HEAD_DIM))
        outs["pm_v"].append(vm_p.reshape(b, m_len, MEM_HEADS, HEAD_DIM))
        if i % 2 == 0:
            ia = i // 2
            pieces = _split_w_in_a(w_in_a[ia])
            gains = [nsa_q_gain[ia], nsa_k_gain[ia, 1], nsa_k_gain[ia, 2]]
            cmp_args = (cmp_pe[ia], cmp_w1[ia], cmp_b1[ia], cmp_w2[ia], nsa_k_gain[ia, 0])
            q, kc, vc, ks, vs, kw, vw, gt, mq_p = _norm_proj(xp, norm_mix[i], pieces, _A_OPS, gains, 256)
            r3 = lambda z: z.reshape(b, s, -1)
            kc, vc, ks, vs, kw, vw = (r3(z) for z in (kc, vc, ks, vs, kw, vw))
            ck, cv = _compress_prompt(kc, vc, *cmp_args)
            mix_p = _nsa_prompt(r3(q), r3(gt), ck, cv, ks, vs, kw, vw).reshape(b * s, QA)
            r5 = lambda z: z.reshape(b, -1, KV_GROUPS, HEAD_DIM)
            for name, z in (("pc_k", kc), ("pc_v", vc), ("ps_k", ks), ("ps_v", vs),
                            ("pw_k", kw[:, s - wl:]), ("pw_v", vw[:, s - wl:])):
                outs[name].append(r5(z))
            q, kc, vc, ks, vs, kw, vw, gt, mq_s = _norm_proj(xs, norm_mix[i], pieces, _A_OPS, gains, 256)
            ck, cv = _compress_sample(page_table, cache_cmp_k[ia], cache_cmp_v[ia], *cmp_args)
            mix_s, nwk, nwv = _nsa_sample(page_table, q, gt, ck, cv, ks, vs, kw, vw,
                                          cache_sel_k[ia], cache_sel_v[ia], cache_win_k[ia], cache_win_v[ia])
            mix_s = mix_s.reshape(bd, QA)
            r5 = lambda z: z.reshape(bd, -1, KV_GROUPS, HEAD_DIM)
            for name, z in (("sc_k", kc), ("sc_v", vc), ("ss_k", ks), ("ss_v", vs), ("sw_k", nwk), ("sw_v", nwv)):
                outs[name].append(r5(z))
        else:
            ib = i // 2
            pieces = [w_in_b[ib][:, :SHIFT_W], w_in_b[ib][:, SHIFT_W:]]
            wargs = _rwkv_weight_args(rwkv_mu[ib], rwkv_w0[ib], rwkv_w_up[ib], rwkv_a0[ib], rwkv_a_up[ib],
                                      rwkv_g_up[ib], rwkv_k_k[ib], rwkv_k_a[ib])
            fin = (rwkv_r_k[ib].reshape(-1), rwkv_ln_w[ib], rwkv_ln_b[ib])
            pr, mq_p = _norm_proj(xp, norm_mix[i], pieces, ("none", "none"), [], 256)
            pr = pr.reshape(b, s, SHIFT_W)
            prep = _rwkv_prep_prompt(pr, wargs, 256)
            mix_p, st_p = _rwkv_chunk(*prep, *fin)
            mix_p = mix_p.reshape(b * s, QA)
            outs["pr_s"].append(st_p)
            outs["pr_x"].append(pr[:, s - 1])
            pr, mq_s = _norm_proj(xs, norm_mix[i], pieces, ("none", "none"), [], 256)
            prep = _rwkv_prep_sample(pr, state_shift[ib], wargs)
            mix_s, st_s = _rwkv_step(*prep, *fin, state_rwkv[ib])
            mix_s = mix_s.reshape(bd, QA)
            outs["sr_s"].append(st_s)
            outs["sr_x"].append(pr)
        mem_p = _mem_attn(mq_p.reshape(b, s, MEMQ), km_p, vm_p, mem_q_gain[i], 256).reshape(b * s, MEMQ)
        mem_s = _mem_attn(mq_s.reshape(bd, 1, MEMQ), cache_mem_k[i].reshape(bd, -1, MEMQ),
                          cache_mem_v[i].reshape(bd, -1, MEMQ), mem_q_gain[i], 1).reshape(bd, MEMQ)
        ffn_w = (w_out[i], norm_ffn[i], moe_w_coarse[i], moe_b_coarse[i], moe_w_fine[i], moe_b_fine[i],
                 moe_w_gate[i], moe_w_up[i], moe_w_down[i])
        xp = _ffn(xp, mix_p, mem_p, *ffn_w, 256, 1024)
        xs = _ffn(xs, mix_s, mem_s, *ffn_w, 256, 1024)
    order = ("pc_k", "pc_v", "ps_k", "ps_v", "pw_k", "pw_v", "pm_k", "pm_v", "pr_s", "pr_x",
             "sc_k", "sc_v", "ss_k", "ss_v", "sw_k", "sw_v", "sr_s", "sr_x")
    return (xp.reshape(b, s, d), xs.reshape(bd, 1, d)) + tuple(jnp.stack(outs[name]) for name in order)
```

```python
import functools

import jax
import jax.numpy as jnp
from jax import lax
from jax.experimental import pallas as pl
from jax.experimental.pallas import tpu as pltpu

F32 = jnp.float32
BF16 = jnp.bfloat16
HI = lax.Precision.HIGHEST

HEAD_DIM = 64
TOK_HEADS = 12
MEM_HEADS = 4
KV_GROUPS = 3
HEADS_PER_GROUP = TOK_HEADS // KV_GROUPS
QA = TOK_HEADS * HEAD_DIM
KVW = KV_GROUPS * HEAD_DIM
MEMQ = MEM_HEADS * HEAD_DIM
CMP_LEN = 32
CMP_STRIDE = 16
CMP_HID = 128
SEL_LEN = 64
SEL_TOPK = 16
WINDOW = 512
Q_BLOCK = 128
NSA_KEY_STEP = 512
PAGE = 128
N_GROUPS = 4
E_PER_GROUP = 8
N_EXPERTS = 32
RWKV_COLS = (768, 768, 768, 64, 64, 128)
SHIFT_W = sum(RWKV_COLS)
RWKV_GN_EPS = 64e-5
RWKV_CHUNK = 64
RWKV_CHUNK_EXACT = False
NEG = -1e30
BIG = 1e30
LANES = 128
VMEM_LIMIT = 56 * 1024 * 1024


def _cparams(*sem):
    return pltpu.CompilerParams(dimension_semantics=sem, vmem_limit_bytes=VMEM_LIMIT)


def _dot_general(a, b, dims, exact):
    if exact:
        return lax.dot_general(a, b, (dims, ((), ())), precision=HI, preferred_element_type=F32)
    return lax.dot_general(a.astype(BF16), b.astype(BF16), (dims, ((), ())), preferred_element_type=F32)


def _dot(a, b, exact=False):
    return _dot_general(a, b, ((1,), (0,)), exact)


def _dot_nt(a, b, exact=False):
    return _dot_general(a, b, ((1,), (1,)), exact)


def _dot_tn(a, b, exact=False):
    return _dot_general(a, b, ((0,), (0,)), exact)


def _bf16_round(x):
    return x.astype(BF16).astype(F32)


def _rms(x, g, eps=1e-6):
    return x * lax.rsqrt(jnp.mean(x * x, axis=-1, keepdims=True) + eps) * g


def _masked_softmax(s, mask):
    s = jnp.where(mask, s, NEG)
    m = jnp.max(s, axis=-1, keepdims=True)
    e = jnp.where(mask, jnp.exp(s - m), 0.0)
    return e / jnp.maximum(jnp.sum(e, axis=-1, keepdims=True), 1e-30)


def _alibi_slope(h):
    return 2.0 ** (-8.0 * (h + 1) / TOK_HEADS)


def _topk_mask(score, k):
    lane = lax.broadcasted_iota(jnp.int32, score.shape, 1).astype(F32)
    sel = jnp.zeros(score.shape, F32)
    for _ in range(k):
        m = jnp.max(score, axis=-1, keepdims=True)
        idx = jnp.min(jnp.where(score == m, lane, 1e9), axis=-1, keepdims=True)
        hit = lane == idx
        sel = jnp.where(hit, 1.0, sel)
        score = jnp.where(hit, -jnp.inf, score)
    return sel


def _topk_rows_t(score_t, k):
    n_blk = score_t.shape[0]
    idx = lax.broadcasted_iota(jnp.int32, score_t.shape, 0)
    ahead = jnp.zeros(score_t.shape, F32)
    for j in range(n_blk):
        row = score_t[j:j + 1]
        ahead = ahead + jnp.where((row > score_t) | ((row == score_t) & (idx > j)), 1.0, 0.0)
    return jnp.where(ahead < k, 1.0, 0.0)


def _sel_to_cmp(n_sel_rows, n_cmp_lanes):
    s0 = lax.broadcasted_iota(jnp.int32, (n_sel_rows, n_cmp_lanes), 0) * SEL_LEN
    c0 = lax.broadcasted_iota(jnp.int32, (n_sel_rows, n_cmp_lanes), 1) * CMP_STRIDE
    return ((c0 < s0 + SEL_LEN) & (c0 + CMP_LEN > s0)).astype(F32)


def _cmp_to_sel(n_cmp_rows, n_sel_lanes):
    n_i = lax.broadcasted_iota(jnp.int32, (n_cmp_rows, n_sel_lanes), 0)
    s_i = lax.broadcasted_iota(jnp.int32, (n_cmp_rows, n_sel_lanes), 1)
    c0 = n_i * CMP_STRIDE
    s0 = s_i * SEL_LEN
    return ((c0 < s0 + SEL_LEN) & (c0 + CMP_LEN > s0)).astype(F32)


def _block_expand(n_blk_rows, n_keys):
    b_i = lax.broadcasted_iota(jnp.int32, (n_blk_rows, n_keys), 0)
    k_i = lax.broadcasted_iota(jnp.int32, (n_blk_rows, n_keys), 1)
    return (jnp.right_shift(k_i, 6) == b_i).astype(BF16)


def _norm_proj_kernel(ops, x_ref, g_ref, *refs):
    n = len(ops)
    n_gain = sum(op == "hnorm" for op in ops)
    w_refs, gain_refs, o_refs = refs[:n], refs[n:n + n_gain], refs[n + n_gain:]
    h = _rms(x_ref[...], g_ref[...])
    gi = 0
    for op, w_ref, o_ref in zip(ops, w_refs, o_refs):
        z = _dot(h, w_ref[...])
        if op == "hnorm":
            gain = gain_refs[gi][...]
            gi += 1
            for hh in range(z.shape[1] // HEAD_DIM):
                seg = z[:, hh * HEAD_DIM:(hh + 1) * HEAD_DIM]
                o_ref[:, hh * HEAD_DIM:(hh + 1) * HEAD_DIM] = _rms(seg, gain)
        elif op == "sigmoid":
            o_ref[...] = jax.nn.sigmoid(z)
        else:
            o_ref[...] = z


def _norm_proj(x, g, weights, ops, gains, tm):
    n_rows, d = x.shape
    tm = min(tm, n_rows)
    assert n_rows % tm == 0
    in_specs = [pl.BlockSpec((tm, d), lambda i: (i, 0)), pl.BlockSpec((1, d), lambda i: (0, 0))]
    in_specs += [pl.BlockSpec(w.shape, lambda i: (0, 0)) for w in weights]
    in_specs += [pl.BlockSpec((1, HEAD_DIM), lambda i: (0, 0)) for _ in gains]
    out_shape = [jax.ShapeDtypeStruct((n_rows, w.shape[1]), F32) for w in weights]
    out_specs = [pl.BlockSpec((tm, w.shape[1]), lambda i: (i, 0)) for w in weights]
    return pl.pallas_call(
        functools.partial(_norm_proj_kernel, tuple(ops)),
        grid=(n_rows // tm,), in_specs=in_specs, out_specs=out_specs, out_shape=out_shape,
        compiler_params=_cparams("parallel"), name="norm_proj",
    )(x, g.reshape(1, d), *weights, *[gn.reshape(1, HEAD_DIM) for gn in gains])


def _compress_rows(rows_ref, pe_ref, w1_ref, b1_ref, w2_ref, gain, o_ref, u_ref, n_chunk):
    for g in range(KV_GROUPS):
        for l in range(CMP_STRIDE):
            src = l * KVW + g * HEAD_DIM
            u_ref[:, l * HEAD_DIM:(l + 1) * HEAD_DIM] = rows_ref[:, src:src + HEAD_DIM]
        _compress_mlp(u_ref, pe_ref, w1_ref, b1_ref, w2_ref, gain, o_ref, g, n_chunk)


def _compress_mlp(u_ref, pe_ref, w1_ref, b1_ref, w2_ref, gain, o_ref, g, n_chunk):
    half = CMP_STRIDE * HEAD_DIM
    u = u_ref[...]
    p0 = _dot(u + pe_ref[:, 0:half], w1_ref[0:half, :])
    p1 = _dot(u + pe_ref[:, half:2 * half], w1_ref[half:2 * half, :])
    hid = b1_ref[...] + p0 + pltpu.roll(p1, n_chunk - 1, 0)
    z = _dot(jax.nn.gelu(hid), w2_ref[...])
    if gain is not None:
        z = _rms(z, gain)
    o_ref[:, g * HEAD_DIM:(g + 1) * HEAD_DIM] = z


def _compress_cols(xt_ref, pe_ref, w1_ref, b1_ref, w2_ref, gain, o_ref, xs_ref, u_ref, n_chunk):
    for first, groups in ((0, (0, 1)), (HEAD_DIM, (2,))):
        xs_ref[...] = xt_ref[first:first + 2 * HEAD_DIM, :].T
        for g in groups:
            lane0 = g * HEAD_DIM - first
            for l in range(CMP_STRIDE):
                rows = xs_ref[pl.ds(l, n_chunk, stride=CMP_STRIDE), :]
                u_ref[:, l * HEAD_DIM:(l + 1) * HEAD_DIM] = rows[:, lane0:lane0 + HEAD_DIM]
            _compress_mlp(u_ref, pe_ref, w1_ref, b1_ref, w2_ref, gain, o_ref, g, n_chunk)


def _compress_prompt_kernel(k_ref, v_ref, pe_ref, w1_ref, b1_ref, w2_ref, gain_ref, ck_ref, cv_ref, u_ref,
                            *, n_chunk):
    _compress_rows(k_ref.at[0], pe_ref.at[0], w1_ref.at[0], b1_ref.at[0], w2_ref.at[0], gain_ref[...],
                   ck_ref.at[0], u_ref, n_chunk)
    _compress_rows(v_ref.at[0], pe_ref.at[1], w1_ref.at[1], b1_ref.at[1], w2_ref.at[1], None,
                   cv_ref.at[0], u_ref, n_chunk)


def _compress_weight_specs(nidx):
    zero = lambda *_: (0, 0, 0)
    del nidx
    return [pl.BlockSpec((2, 1, CMP_LEN * HEAD_DIM), zero),
            pl.BlockSpec((2, CMP_LEN * HEAD_DIM, CMP_HID), zero),
            pl.BlockSpec((2, 1, CMP_HID), zero),
            pl.BlockSpec((2, CMP_HID, HEAD_DIM), zero)]


def _compress_prompt(kc, vc, pe, w1, b1, w2, gain):
    b, s, _ = kc.shape
    n_chunk = s // CMP_STRIDE
    cw = CMP_STRIDE * KVW
    rows = pl.BlockSpec((1, n_chunk, cw), lambda i: (i, 0, 0))
    out = pl.BlockSpec((1, n_chunk, KVW), lambda i: (i, 0, 0))
    return pl.pallas_call(
        functools.partial(_compress_prompt_kernel, n_chunk=n_chunk),
        grid=(b,),
        in_specs=[rows, rows] + _compress_weight_specs(1) + [pl.BlockSpec((1, HEAD_DIM), lambda i: (0, 0))],
        out_specs=[out, out],
        out_shape=[jax.ShapeDtypeStruct((b, n_chunk, KVW), F32)] * 2,
        scratch_shapes=[pltpu.VMEM((n_chunk, CMP_STRIDE * HEAD_DIM), F32)],
        compiler_params=_cparams("parallel"), name="compress_prompt",
    )(kc.reshape(b, n_chunk, cw), vc.reshape(b, n_chunk, cw), pe.reshape(2, 1, -1), w1, b1.reshape(2, 1, -1), w2,
      gain.reshape(1, HEAD_DIM))


def _compress_sample_kernel(pt_ref, pk_ref, pv_ref, pe_ref, w1_ref, b1_ref, w2_ref, gain_ref,
                            ck_ref, cv_ref, kbuf, vbuf, xs_ref, u_ref, sem, *, n_pages):
    b = pl.program_id(0)
    n_chunk = n_pages * PAGE // CMP_STRIDE

    def gather(pool_ref, dst_ref, dma_sem):
        copies = []
        for j in range(n_pages):
            cp = pltpu.make_async_copy(pool_ref.at[pt_ref[b, j]], dst_ref.at[:, pl.ds(j * PAGE, PAGE)], dma_sem)
            cp.start()
            copies.append(cp)
        return copies

    ck_copies = gather(pk_ref, kbuf, sem.at[0])
    cv_copies = gather(pv_ref, vbuf, sem.at[1])
    for cp in ck_copies:
        cp.wait()
    _compress_cols(kbuf, pe_ref.at[0], w1_ref.at[0], b1_ref.at[0], w2_ref.at[0], gain_ref[...],
                   ck_ref.at[0], xs_ref, u_ref, n_chunk)
    for cp in cv_copies:
        cp.wait()
    _compress_cols(vbuf, pe_ref.at[1], w1_ref.at[1], b1_ref.at[1], w2_ref.at[1], None,
                   cv_ref.at[0], xs_ref, u_ref, n_chunk)


def _compress_sample(page_table, pool_k, pool_v, pe, w1, b1, w2, gain):
    bd, n_pages = page_table.shape
    n_pool = pool_k.shape[0]
    n_chunk = n_pages * PAGE // CMP_STRIDE
    past = n_pages * PAGE
    pos_minor = lambda z: jnp.transpose(z, (0, 2, 3, 1)).reshape(n_pool, KVW, PAGE)
    zero3 = lambda i, pt: (0, 0, 0)
    any_spec = pl.BlockSpec(memory_space=pl.ANY)
    out = pl.BlockSpec((1, n_chunk, KVW), lambda i, pt: (i, 0, 0))
    grid_spec = pltpu.PrefetchScalarGridSpec(
        num_scalar_prefetch=1, grid=(bd,),
        in_specs=[any_spec, any_spec,
                  pl.BlockSpec((2, 1, CMP_LEN * HEAD_DIM), zero3),
                  pl.BlockSpec((2, CMP_LEN * HEAD_DIM, CMP_HID), zero3),
                  pl.BlockSpec((2, 1, CMP_HID), zero3),
                  pl.BlockSpec((2, CMP_HID, HEAD_DIM), zero3),
                  pl.BlockSpec((1, HEAD_DIM), lambda i, pt: (0, 0))],
        out_specs=[out, out],
        scratch_shapes=[pltpu.VMEM((KVW, past), F32), pltpu.VMEM((KVW, past), F32),
                        pltpu.VMEM((past, LANES), F32),
                        pltpu.VMEM((n_chunk, CMP_STRIDE * HEAD_DIM), F32), pltpu.SemaphoreType.DMA((2,))])
    return pl.pallas_call(
        functools.partial(_compress_sample_kernel, n_pages=n_pages),
        grid_spec=grid_spec,
        out_shape=[jax.ShapeDtypeStruct((bd, n_chunk, KVW), F32)] * 2,
        compiler_params=_cparams("arbitrary"), name="compress_sample",
    )(page_table, pos_minor(pool_k), pos_minor(pool_v),
      pe.reshape(2, 1, -1), w1, b1.reshape(2, 1, -1), w2, gain.reshape(1, HEAD_DIM))


def _softmax_rows(s):
    e = jnp.exp(s - jnp.max(s, axis=-1, keepdims=True))
    return e / jnp.sum(e, axis=-1, keepdims=True)


def _nsa_prompt_block(q_ref, gt_ref, ck_ref, cv_ref, ks_ref, vs_ref, kw_ref, vw_ref, o_ref,
                      *, seq, n_keys, first_block):
    tq = Q_BLOCK
    hpg = HEADS_PER_GROUP
    n_cmp = ck_ref.shape[1]
    n_sel = seq // SEL_LEN
    wlen = min(WINDOW + tq, seq)
    q0 = (pl.program_id(1) + first_block) * tq
    t1 = q0 + lax.broadcasted_iota(jnp.int32, (tq, 1), 0)
    s2c = _sel_to_cmp(n_sel, n_cmp)
    expand = _block_expand(n_sel, n_keys)
    blk = lax.broadcasted_iota(jnp.int32, (n_sel, 1), 0)
    t_row = q0 + lax.broadcasted_iota(jnp.int32, (1, tq), 1)
    cur = jnp.right_shift(t_row, 6)
    forced = (blk == 0) | (blk == cur) | (blk == cur - 1)
    valid = blk * SEL_LEN <= t_row
    dist_c = t1 - (lax.broadcasted_iota(jnp.int32, (1, n_cmp), 1) * CMP_STRIDE + (CMP_LEN - 1))
    mask_c = dist_c >= 0
    dist_cf = dist_c.astype(F32)
    dist_s = t1 - lax.broadcasted_iota(jnp.int32, (1, n_keys), 1)
    causal_s = dist_s >= 0
    dist_sf = dist_s.astype(F32)
    w_start = pl.multiple_of(jnp.clip(q0 - WINDOW, 0, seq - wlen), tq)
    dist_w = t1 - (w_start + lax.broadcasted_iota(jnp.int32, (1, wlen), 1))
    bias_w = jnp.where((dist_w >= 0) & (dist_w <= WINDOW), 0.0, NEG)
    dist_wf = dist_w.astype(F32)
    gt = gt_ref[0]
    for g in range(KV_GROUPS):
        lanes = slice(g * HEAD_DIM, (g + 1) * HEAD_DIM)
        heads = [g * hpg + j for j in range(hpg)]
        rows = [slice(j * tq, (j + 1) * tq) for j in range(hpg)]
        q4 = jnp.concatenate([q_ref[0, :, h * HEAD_DIM:(h + 1) * HEAD_DIM] for h in heads],
                             axis=0) * (HEAD_DIM ** -0.5)
        s_c = _dot_nt(q4, ck_ref[0, :, lanes])
        p_c = [_masked_softmax(s_c[rows[j]] - _alibi_slope(heads[j]) * dist_cf, mask_c) for j in range(hpg)]
        cv = cv_ref[0, :, lanes]
        o_c = [_dot(p, cv) for p in p_c]
        imps = [_dot_nt(s2c, p) for p in p_c]
        imp = imps[0]
        for j in range(1, hpg):
            imp = imp + imps[j]
        score = jnp.where(valid, jnp.where(forced, BIG, imp), -BIG)
        key_sel = _dot_tn(_topk_rows_t(score, min(SEL_TOPK, n_sel)), expand)
        bias_s = jnp.where((key_sel > 0.5) & causal_s, 0.0, NEG)
        s_s = _dot_nt(q4, ks_ref[0, 0:n_keys, lanes])
        vs = vs_ref[0, 0:n_keys, lanes]
        o_s = [_dot(_softmax_rows(s_s[rows[j]] - _alibi_slope(heads[j]) * dist_sf + bias_s), vs)
               for j in range(hpg)]
        s_w = _dot_nt(q4, kw_ref[0, pl.ds(w_start, wlen), lanes])
        vw = vw_ref[0, pl.ds(w_start, wlen), lanes]
        o_w = [_dot(_softmax_rows(s_w[rows[j]] - _alibi_slope(heads[j]) * dist_wf + bias_w), vw)
               for j in range(hpg)]
        for j, h in enumerate(heads):
            o_ref[0, :, h * HEAD_DIM:(h + 1) * HEAD_DIM] = (
                gt[:, 3 * h:3 * h + 1] * o_c[j] + gt[:, 3 * h + 1:3 * h + 2] * o_s[j]
                + gt[:, 3 * h + 2:3 * h + 3] * o_w[j])


def _nsa_prompt(q, gates, ck, cv, ks, vs, kw, vw):
    b, s, _ = q.shape
    n_cmp = ck.shape[1]
    key_step = min(NSA_KEY_STEP, s)
    per_class = key_step // Q_BLOCK
    full = lambda n: pl.BlockSpec((1, n, KVW), lambda i, j: (i, 0, 0))
    outs = []
    for c in range(s // key_step):
        n_keys = (c + 1) * key_step
        first = c * per_class
        qspec = lambda w, first=first: pl.BlockSpec((1, Q_BLOCK, w), lambda i, j: (i, j + first, 0))
        outs.append(pl.pallas_call(
            functools.partial(_nsa_prompt_block, seq=s, n_keys=n_keys, first_block=first),
            grid=(b, per_class),
            in_specs=[qspec(QA), qspec(LANES), full(n_cmp), full(n_cmp), full(n_keys), full(n_keys), full(s), full(s)],
            out_specs=pl.BlockSpec((1, Q_BLOCK, QA), lambda i, j: (i, j, 0)),
            out_shape=jax.ShapeDtypeStruct((b, key_step, QA), F32),
            compiler_params=_cparams("parallel", "parallel"), name=f"nsa_prompt_{n_keys}",
        )(q, gates, ck, cv, ks, vs, kw, vw))
    return jnp.concatenate(outs, axis=1)


def _gather_pages_t(pt_ref, b, pool_ref, dst_ref, sem, n_pages):
    copies = []
    for j in range(n_pages):
        cp = pltpu.make_async_copy(pool_ref.at[pt_ref[b, j]], dst_ref.at[:, :, pl.ds(j * PAGE, PAGE)], sem)
        cp.start()
        copies.append(cp)
    return copies


def _nsa_sample_kernel(pt_ref, q_ref, gt_ref, ck_ref, cv_ref, ksn_ref, vsn_ref, kwn_ref, vwn_ref,
                       pks_ref, pvs_ref, wk_ref, wv_ref, o_ref, owk_ref, owv_ref, kbuf, vbuf, sem,
                       *, n_pages):
    b = pl.program_id(0)
    hpg = HEADS_PER_GROUP
    past = n_pages * PAGE
    n_cmp = ck_ref.shape[1]
    n_sel_past = past // SEL_LEN
    n_sel = n_sel_past + 1
    wb = wk_ref.shape[3]
    sel_lanes = 2 * LANES
    k_copies = _gather_pages_t(pt_ref, b, pks_ref, kbuf, sem.at[0], n_pages)
    v_copies = _gather_pages_t(pt_ref, b, pvs_ref, vbuf, sem.at[1], n_pages)

    row = lax.broadcasted_iota(jnp.int32, (8, 1), 0)
    n = HEAD_DIM
    eye = (lax.broadcasted_iota(jnp.int32, (n, n), 0) == lax.broadcasted_iota(jnp.int32, (n, n), 1)).astype(F32)
    gt = gt_ref[0]
    c_end = lax.broadcasted_iota(jnp.int32, (1, n_cmp), 1) * CMP_STRIDE + (CMP_LEN - 1)
    dist_c = past - c_end
    blk = lax.broadcasted_iota(jnp.int32, (1, sel_lanes), 1)
    cur = past // SEL_LEN
    forced = (blk == 0) | (blk == cur) | (blk == cur - 1)
    valid = blk * SEL_LEN <= past
    c2s = _cmp_to_sel(n_cmp, sel_lanes)

    def q_rows(g):
        q4 = jnp.concatenate(
            [q_ref[0, :, (g * hpg + j) * HEAD_DIM:(g * hpg + j + 1) * HEAD_DIM] for j in range(hpg)]
            + [jnp.zeros((8 - hpg, HEAD_DIM), F32)], axis=0) * (HEAD_DIM ** -0.5)
        slope = jnp.full((8, 1), _alibi_slope(g * hpg + hpg - 1), F32)
        for j in range(hpg - 2, -1, -1):
            slope = jnp.where(row < j + 1, _alibi_slope(g * hpg + j), slope)
        return q4, slope

    o_cs, imps = [], []
    for g in range(KV_GROUPS):
        lanes = slice(g * HEAD_DIM, (g + 1) * HEAD_DIM)
        q4, slope = q_rows(g)
        p_c = _masked_softmax(_dot_nt(q4, ck_ref[0, :, lanes]) - slope * dist_c.astype(F32), dist_c >= 0)
        o_cs.append(_dot(p_c, cv_ref[0, :, lanes]))
        imps.append(jnp.sum(jnp.where(row < hpg, _dot(p_c, c2s), 0.0), axis=0, keepdims=True))
    imp = jnp.concatenate(imps + [jnp.zeros((8 - KV_GROUPS, sel_lanes), F32)], axis=0)
    score = jnp.where(valid, jnp.where(forced, BIG, imp), -BIG)
    score = jnp.where(blk < n_sel, score, -jnp.inf)
    sel = _topk_mask(score, min(SEL_TOPK, n_sel))
    key_sel = _dot(sel[:, 0:n_sel_past], _block_expand(n_sel_past, past))
    sel_new = sel[:, n_sel_past:n_sel_past + 1]

    pos_s = lax.broadcasted_iota(jnp.int32, (1, past), 1)
    dist_s = (past - pos_s).astype(F32)
    lane_w = lax.broadcasted_iota(jnp.int32, (1, wb), 1)
    pos_w = past - wb + lane_w
    dist_w = past - pos_w
    mask_w = (dist_w >= 0) & (dist_w <= WINDOW) & (pos_w >= 0)
    for cp in k_copies:
        cp.wait()
    for cp in v_copies:
        cp.wait()
    for g in range(KV_GROUPS):
        lanes = slice(g * HEAD_DIM, (g + 1) * HEAD_DIM)
        q4, slope = q_rows(g)
        q4r = _bf16_round(q4)
        s_p = jnp.where(key_sel[g:g + 1] > 0.5, _dot(q4, kbuf[g]) - slope * dist_s, NEG)
        new_ok = sel_new[g:g + 1] > 0.5
        s_n = jnp.where(new_ok, jnp.sum(q4r * _bf16_round(ksn_ref[0, :, lanes]), axis=-1, keepdims=True), NEG)
        m = jnp.maximum(jnp.max(s_p, axis=-1, keepdims=True), s_n)
        e_p = jnp.where(key_sel[g:g + 1] > 0.5, jnp.exp(s_p - m), 0.0)
        e_n = jnp.where(new_ok, jnp.exp(s_n - m), 0.0)
        den = jnp.maximum(jnp.sum(e_p, axis=-1, keepdims=True) + e_n, 1e-30)
        o_s = _dot_nt(e_p / den, vbuf[g]) + _bf16_round(e_n / den) * _bf16_round(vsn_ref[0, :, lanes])
        s_p = jnp.where(mask_w, _dot(q4, wk_ref[0, g]) - slope * dist_w.astype(F32), NEG)
        s_n = jnp.sum(q4r * _bf16_round(kwn_ref[0, :, lanes]), axis=-1, keepdims=True)
        m = jnp.maximum(jnp.max(s_p, axis=-1, keepdims=True), s_n)
        e_p = jnp.where(mask_w, jnp.exp(s_p - m), 0.0)
        e_n = jnp.exp(s_n - m)
        den = jnp.maximum(jnp.sum(e_p, axis=-1, keepdims=True) + e_n, 1e-30)
        o_w = _dot_nt(e_p / den, wv_ref[0, g]) + _bf16_round(e_n / den) * _bf16_round(vwn_ref[0, :, lanes])
        for j in range(hpg):
            h = g * hpg + j
            o_ref[0, :, h * HEAD_DIM:(h + 1) * HEAD_DIM] = (
                gt[:, 3 * h:3 * h + 1] * o_cs[g][j:j + 1] + gt[:, 3 * h + 1:3 * h + 2] * o_s[j:j + 1]
                + gt[:, 3 * h + 2:3 * h + 3] * o_w[j:j + 1])
        k_col = jnp.sum(eye * kwn_ref[0, :, lanes], axis=1, keepdims=True)
        v_col = jnp.sum(eye * vwn_ref[0, :, lanes], axis=1, keepdims=True)
        owk_ref[0, g] = jnp.where(lane_w == wb - 1, k_col, pltpu.roll(wk_ref[0, g], wb - 1, 1))
        owv_ref[0, g] = jnp.where(lane_w == wb - 1, v_col, pltpu.roll(wv_ref[0, g], wb - 1, 1))


def _nsa_sample(page_table, q, gates, ck, cv, ks_new, vs_new, kw_new, vw_new, pool_ks, pool_vs, win_k, win_v):
    bd, n_pages = page_table.shape
    n_cmp = ck.shape[1]
    wb = win_k.shape[1]
    one = lambda w: pl.BlockSpec((1, 1, w), lambda i, pt: (i, 0, 0))
    rows = lambda m: pl.BlockSpec((1, m, KVW), lambda i, pt: (i, 0, 0))
    win = pl.BlockSpec((1, KV_GROUPS, HEAD_DIM, wb), lambda i, pt: (i, 0, 0, 0))
    any_spec = pl.BlockSpec(memory_space=pl.ANY)
    grid_spec = pltpu.PrefetchScalarGridSpec(
        num_scalar_prefetch=1, grid=(bd,),
        in_specs=[one(QA), one(LANES), rows(n_cmp), rows(n_cmp), one(KVW), one(KVW), one(KVW), one(KVW),
                  any_spec, any_spec, win, win],
        out_specs=[one(QA), win, win],
        scratch_shapes=[pltpu.VMEM((KV_GROUPS, HEAD_DIM, n_pages * PAGE), F32),
                        pltpu.VMEM((KV_GROUPS, HEAD_DIM, n_pages * PAGE), F32),
                        pltpu.SemaphoreType.DMA((2,))])
    r3 = lambda z: z.reshape(bd, 1, -1)
    pos_minor = lambda z: jnp.transpose(z, (0, 2, 3, 1))
    win_shape = jax.ShapeDtypeStruct((bd, KV_GROUPS, HEAD_DIM, wb), F32)
    mix, nwk, nwv = pl.pallas_call(
        functools.partial(_nsa_sample_kernel, n_pages=n_pages),
        grid_spec=grid_spec,
        out_shape=[jax.ShapeDtypeStruct((bd, 1, QA), F32), win_shape, win_shape],
        compiler_params=_cparams("arbitrary"), name="nsa_sample",
    )(page_table, r3(q), r3(gates), ck, cv, r3(ks_new), r3(vs_new), r3(kw_new), r3(vw_new),
      pos_minor(pool_ks), pos_minor(pool_vs), pos_minor(win_k), pos_minor(win_v))
    back = lambda z: jnp.transpose(z, (0, 3, 1, 2))
    return mix, back(nwk), back(nwv)


def _mem_attn_kernel(q_ref, k_ref, v_ref, gain_ref, o_ref):
    tq = q_ref.shape[1]
    pad = max(8 - tq, 0)
    for h in range(MEM_HEADS):
        lanes = slice(h * HEAD_DIM, (h + 1) * HEAD_DIM)
        q = _rms(q_ref[0, :, lanes], gain_ref[...]) * (HEAD_DIM ** -0.5)
        if pad:
            q = jnp.concatenate([q, jnp.zeros((pad, HEAD_DIM), F32)], axis=0)
        s = _dot_nt(q, k_ref[0, :, lanes])
        e = jnp.exp(s - jnp.max(s, axis=-1, keepdims=True))
        p = e / jnp.sum(e, axis=-1, keepdims=True)
        o_ref[0, :, lanes] = _dot(p, v_ref[0, :, lanes])[0:tq]


def _mem_attn(mq, km, vm, gain, tq):
    b, t, _ = mq.shape
    m = km.shape[1]
    tq = min(tq, t)
    qspec = pl.BlockSpec((1, tq, MEMQ), lambda i, j: (i, j, 0))
    kspec = pl.BlockSpec((1, m, MEMQ), lambda i, j: (i, 0, 0))
    return pl.pallas_call(
        _mem_attn_kernel, grid=(b, t // tq),
        in_specs=[qspec, kspec, kspec, pl.BlockSpec((1, HEAD_DIM), lambda i, j: (0, 0))],
        out_specs=qspec, out_shape=jax.ShapeDtypeStruct((b, t, MEMQ), F32),
        compiler_params=_cparams("parallel", "parallel"), name="mem_attn",
    )(mq, km, vm, gain.reshape(1, HEAD_DIM))


def _outproj_router_kernel(x_ref, mix_ref, mem_ref, wo_ref, g_ref, wc_ref, bc_ref, wf_ref, bf_ref,
                           xo_ref, hn_ref, cw_ref):
    x = x_ref[...] + _dot(mix_ref[...], wo_ref[0:QA, :]) + _dot(mem_ref[...], wo_ref[QA:QA + MEMQ, :])
    xo_ref[...] = x
    hn = _rms(x, g_ref[...])
    hn_ref[...] = hn
    lane = lax.broadcasted_iota(jnp.int32, (1, LANES), 1).astype(F32)
    lg = jnp.where(lane < N_GROUPS, _dot(hn, wc_ref[...]) + bc_ref[...], -jnp.inf)
    m = jnp.max(lg, axis=-1, keepdims=True)
    grp = jnp.min(jnp.where(lg == m, lane, 1e9), axis=-1, keepdims=True)
    p_grp = 1.0 / jnp.sum(jnp.exp(lg - m), axis=-1, keepdims=True)
    in_grp = (lane >= grp * E_PER_GROUP) & (lane < (grp + 1.0) * E_PER_GROUP)
    lf = jnp.where(in_grp, _dot(hn, wf_ref[...]) + bf_ref[...], -jnp.inf)
    v1 = jnp.max(lf, axis=-1, keepdims=True)
    i1 = jnp.min(jnp.where(lf == v1, lane, 1e9), axis=-1, keepdims=True)
    lf2 = jnp.where(lane == i1, -jnp.inf, lf)
    v2 = jnp.max(lf2, axis=-1, keepdims=True)
    i2 = jnp.min(jnp.where(lf2 == v2, lane, 1e9), axis=-1, keepdims=True)
    e2 = jnp.exp(v2 - v1)
    den = 1.0 + e2
    cw_ref[...] = jnp.where(lane == i1, p_grp / den, 0.0) + jnp.where(lane == i2, p_grp * (e2 / den), 0.0)


def _outproj_router(x, mix, mem, w_out, g_ffn, w_coarse, b_coarse, w_fine, b_fine, tm):
    n, d = x.shape
    tm = min(tm, n)
    pad_w = lambda w: jnp.pad(w, ((0, 0), (0, LANES - w.shape[1])))
    pad_b = lambda v: jnp.pad(v, (0, LANES - v.shape[0])).reshape(1, LANES)
    row = lambda w: pl.BlockSpec((tm, w), lambda i: (i, 0))
    const = lambda r, c: pl.BlockSpec((r, c), lambda i: (0, 0))
    return pl.pallas_call(
        _outproj_router_kernel, grid=(n // tm,),
        in_specs=[row(d), row(QA), row(MEMQ), const(QA + MEMQ, d), const(1, d), const(d, LANES),
                  const(1, LANES), const(d, LANES), const(1, LANES)],
        out_specs=[row(d), row(d), row(LANES)],
        out_shape=[jax.ShapeDtypeStruct((n, d), F32), jax.ShapeDtypeStruct((n, d), F32),
                   jax.ShapeDtypeStruct((n, LANES), F32)],
        compiler_params=_cparams("parallel"), name="outproj_router",
    )(x, mix, mem, w_out, g_ffn.reshape(1, d), pad_w(w_coarse), pad_b(b_coarse), pad_w(w_fine), pad_b(b_fine))


def _moe_kernel(x_ref, hn_ref, cw_ref, wg_ref, wu_ref, wd_ref, o_ref):
    e = pl.program_id(1)

    @pl.when(e == 0)
    def _():
        o_ref[...] = x_ref[...]

    hb = hn_ref[...].astype(BF16)
    gate = _dot(hb, wg_ref[0])
    up = _dot(hb, wu_ref[0])
    lane = lax.broadcasted_iota(jnp.int32, (1, LANES), 1)
    c = jnp.sum(jnp.where(lane == e, cw_ref[...], 0.0), axis=-1, keepdims=True)
    o_ref[...] += _dot(gate * jax.nn.sigmoid(gate) * up, wd_ref[0]) * c


def _moe(x, hn, cw, w_gate, w_up, w_down, layer, tm):
    n, d = x.shape
    tm = min(tm, n)
    _, n_exp, _, d_exp = w_gate.shape
    row = lambda w: pl.BlockSpec((tm, w), lambda i, e: (i, 0))
    return pl.pallas_call(
        _moe_kernel, grid=(n // tm, n_exp),
        in_specs=[row(d), row(d), row(LANES),
                  pl.BlockSpec((None, 1, d, d_exp), lambda i, e: (layer, e, 0, 0)),
                  pl.BlockSpec((None, 1, d, d_exp), lambda i, e: (layer, e, 0, 0)),
                  pl.BlockSpec((None, 1, d_exp, d), lambda i, e: (layer, e, 0, 0))],
        out_specs=row(d), out_shape=jax.ShapeDtypeStruct((n, d), F32),
        compiler_params=_cparams("parallel", "arbitrary"), name="moe",
    )(x, hn, cw, w_gate, w_up, w_down)


def _rwkv_prep_body(p, prev, mu_ref, w0_ref, wup_ref, a0_ref, aup_ref, gup_ref, kk_ref, ka_ref, outs):
    r_ref, ld_ref, k_ref, v_ref, kkn_ref, a_ref, g_ref = outs
    x = p + mu_ref[...] * (prev - p)
    c0, c1, c2, c3, c4 = 768, 1536, 2304, 2368, 2432
    r, k, v = x[:, 0:c0], x[:, c0:c1], x[:, c1:c2]
    xw, xa, xg = x[:, c2:c3], x[:, c3:c4], x[:, c4:SHIFT_W]
    z = -(w0_ref[...] + _dot(jnp.tanh(xw), wup_ref[...]))
    softplus = jnp.maximum(z, 0.0) + jnp.log(1.0 + jnp.exp(-jnp.abs(z)))
    w = -softplus - 0.5
    a = jax.nn.sigmoid(a0_ref[...] + _dot(xa, aup_ref[...]))
    r_ref[...] = r
    ld_ref[...] = -jnp.exp(w)
    k_ref[...] = k * (1.0 + (a - 1.0) * ka_ref[...])
    v_ref[...] = v
    a_ref[...] = a
    g_ref[...] = _dot(jax.nn.sigmoid(xg), gup_ref[...])
    kk = k * kk_ref[...]
    for h in range(TOK_HEADS):
        lanes = slice(h * HEAD_DIM, (h + 1) * HEAD_DIM)
        seg = kk[:, lanes]
        nrm = jnp.sqrt(jnp.sum(seg * seg, axis=-1, keepdims=True))
        kkn_ref[:, lanes] = seg / jnp.maximum(nrm, 1e-12)


def _rwkv_prep_prompt_kernel(p_ref, mu_ref, w0_ref, wup_ref, a0_ref, aup_ref, gup_ref, kk_ref, ka_ref,
                             *rest):
    outs, carry = rest[:7], rest[7]
    tm = p_ref.shape[1]

    @pl.when(pl.program_id(1) == 0)
    def _():
        carry[...] = jnp.zeros_like(carry)

    p = p_ref[0]
    first = lax.broadcasted_iota(jnp.int32, (tm, 1), 0) == 0
    prev = jnp.where(first, carry[...], pltpu.roll(p, 1, 0))
    carry[...] = p[tm - 1:tm]
    _rwkv_prep_body(p, prev, mu_ref, w0_ref, wup_ref, a0_ref, aup_ref, gup_ref, kk_ref, ka_ref,
                    [o.at[0] for o in outs])


def _rwkv_prep_sample_kernel(p_ref, prev_ref, mu_ref, w0_ref, wup_ref, a0_ref, aup_ref, gup_ref, kk_ref,
                             ka_ref, *outs):
    _rwkv_prep_body(p_ref[...], prev_ref[...], mu_ref, w0_ref, wup_ref, a0_ref, aup_ref, gup_ref, kk_ref,
                    ka_ref, outs)


def _rwkv_weight_args(mu, w0, w_up, a0, a_up, g_up, k_k, k_a):
    row = lambda v: v.reshape(1, -1)
    return (row(mu), row(w0), w_up, row(a0), a_up, g_up, row(k_k), row(k_a))


def _rwkv_prep_prompt(pr, wargs, tm):
    b, s, _ = pr.shape
    tm = min(tm, s)
    const = lambda a: pl.BlockSpec(a.shape, lambda i, j: (0, 0))
    ospec = pl.BlockSpec((1, tm, QA), lambda i, j: (i, j, 0))
    return pl.pallas_call(
        _rwkv_prep_prompt_kernel, grid=(b, s // tm),
        in_specs=[pl.BlockSpec((1, tm, SHIFT_W), lambda i, j: (i, j, 0))] + [const(a) for a in wargs],
        out_specs=[ospec] * 7, out_shape=[jax.ShapeDtypeStruct((b, s, QA), F32)] * 7,
        scratch_shapes=[pltpu.VMEM((1, SHIFT_W), F32)],
        compiler_params=_cparams("parallel", "arbitrary"), name="rwkv_prep_prompt",
    )(pr, *wargs)


def _rwkv_prep_sample(pr, prev, wargs):
    n = pr.shape[0]
    full = lambda a: pl.BlockSpec(a.shape, lambda i: (0, 0))
    return pl.pallas_call(
        _rwkv_prep_sample_kernel, grid=(1,),
        in_specs=[full(pr), full(prev)] + [full(a) for a in wargs],
        out_specs=[pl.BlockSpec((n, QA), lambda i: (0, 0))] * 7,
        out_shape=[jax.ShapeDtypeStruct((n, QA), F32)] * 7,
        compiler_params=_cparams("arbitrary"), name="rwkv_prep_sample",
    )(pr, prev, *wargs)


def _rwkv_finish(y, r, k, v, g, rk, lnw, lnb):
    m = jnp.mean(y, axis=-1, keepdims=True)
    var = jnp.mean(jnp.square(y - m), axis=-1, keepdims=True)
    yn = (y - m) * lax.rsqrt(var + RWKV_GN_EPS) * lnw + lnb
    bonus = jnp.sum(r * k * rk, axis=-1, keepdims=True) * v
    return (yn + bonus) * g


def _rwkv_chunk_kernel(r_ref, ld_ref, k_ref, v_ref, kk_ref, a_ref, g_ref, rk_ref, lnw_ref, lnb_ref,
                       o_ref, st_ref, z_ref):
    c = r_ref.shape[1]
    ci = pl.program_id(1)

    @pl.when(ci == 0)
    def _():
        z_ref[...] = jnp.zeros_like(z_ref)

    ri = lax.broadcasted_iota(jnp.int32, (c, c), 0)
    cj = lax.broadcasted_iota(jnp.int32, (c, c), 1)
    incl = ri >= cj
    strict = ri > cj
    n = HEAD_DIM
    eye = (lax.broadcasted_iota(jnp.int32, (n, n), 0) == lax.broadcasted_iota(jnp.int32, (n, n), 1)).astype(F32)
    ex = RWKV_CHUNK_EXACT
    heads = range(TOK_HEADS)
    hl = [slice(h * n, (h + 1) * n) for h in heads]
    ld, r_all, k_all, v_all, kk_all = ld_ref[0], r_ref[0], k_ref[0], v_ref[0], kk_ref[0]
    lc = _dot(incl.astype(F32), ld, exact=True)
    l_end = lc[c - 1:c]
    b_all = kk_all * a_ref[0]
    e_neg = jnp.exp(-lc)
    e_rem = jnp.exp(l_end - lc)
    at_all = -kk_all * jnp.exp(lc - ld)
    rt_all = r_all * jnp.exp(lc)
    bt_all, kt_all = b_all * e_neg, k_all * e_neg
    bh_all, kh_all = b_all * e_rem, k_all * e_rem
    p_end = jnp.exp(l_end)
    v = [v_all[:, s] for s in hl]
    big = [_dot_nt(jnp.concatenate([at_all[:, s], rt_all[:, s]], axis=0),
                   jnp.concatenate([bt_all[:, s], kt_all[:, s]], axis=0), ex) for s in hl]
    a_ab = [jnp.where(strict, m[0:c, 0:c], 0.0) for m in big]
    a_rbk = [jnp.concatenate([jnp.where(incl, m[c:2 * c, 0:c], 0.0), jnp.where(incl, m[c:2 * c, c:2 * c], 0.0)],
                             axis=1) for m in big]
    akv = [_dot(jnp.where(strict, big[h][0:c, c:2 * c], 0.0), v[h], ex) for h in heads]
    x = [jnp.concatenate([at_all[:, hl[h]], akv[h]], axis=1) for h in heads]
    npow = a_ab
    x = [x[h] + _dot(npow[h], x[h], ex) for h in heads]
    steps = 1
    while 2 * steps < c:
        npow = [_dot(m, m, ex) for m in npow]
        x = [x[h] + _dot(npow[h], x[h], ex) for h in heads]
        steps *= 2
    zeros = jnp.zeros((c, n), F32)
    wv = [jnp.concatenate([x[h], jnp.concatenate([zeros, v[h]], axis=1)], axis=0) for h in heads]
    rq_y0 = [_dot(a_rbk[h], wv[h], ex) for h in heads]
    m_n = [_dot_tn(jnp.concatenate([bh_all[:, hl[h]], kh_all[:, hl[h]]], axis=0), wv[h], ex)
           for h in heads]
    yz = [_dot(jnp.concatenate([rt_all[:, hl[h]] + rq_y0[h][:, 0:n],
                                m_n[h][:, 0:n] + eye * p_end[:, hl[h]]], axis=0), z_ref[h], ex)
          for h in heads]
    for h in heads:
        z_ref[h] = yz[h][c:c + n] + m_n[h][:, n:2 * n]
        y = yz[h][0:c] + rq_y0[h][:, n:2 * n]
        o_ref[0, :, hl[h]] = _rwkv_finish(y, r_all[:, hl[h]], k_all[:, hl[h]], v[h], g_ref[0, :, hl[h]],
                                          rk_ref[:, hl[h]], lnw_ref[:, hl[h]], lnb_ref[:, hl[h]])

    @pl.when(ci == pl.num_programs(1) - 1)
    def _():
        for h in range(TOK_HEADS):
            st_ref[0, h] = z_ref[h].T


def _rwkv_chunk(r, ld, k, v, kk, a, g, r_k, ln_w, ln_b):
    b, s, _ = r.shape
    c = RWKV_CHUNK
    tok = pl.BlockSpec((1, c, QA), lambda i, j: (i, j, 0))
    const = pl.BlockSpec((1, QA), lambda i, j: (0, 0))
    return pl.pallas_call(
        _rwkv_chunk_kernel, grid=(b, s // c),
        in_specs=[tok] * 7 + [const] * 3,
        out_specs=[tok, pl.BlockSpec((1, TOK_HEADS, HEAD_DIM, HEAD_DIM), lambda i, j: (i, 0, 0, 0))],
        out_shape=[jax.ShapeDtypeStruct((b, s, QA), F32),
                   jax.ShapeDtypeStruct((b, TOK_HEADS, HEAD_DIM, HEAD_DIM), F32)],
        scratch_shapes=[pltpu.VMEM((TOK_HEADS, HEAD_DIM, HEAD_DIM), F32)],
        compiler_params=_cparams("parallel", "arbitrary"), name="rwkv_chunk",
    )(r, ld, k, v, kk, a, g, r_k.reshape(1, QA), ln_w.reshape(1, QA), ln_b.reshape(1, QA))


def _rwkv_step_kernel(r_ref, ld_ref, k_ref, v_ref, kk_ref, a_ref, g_ref, rk_ref, lnw_ref, lnb_ref, s_ref,
                      o_ref, so_ref):
    n = HEAD_DIM
    eye = (lax.broadcasted_iota(jnp.int32, (n, n), 0) == lax.broadcasted_iota(jnp.int32, (n, n), 1)).astype(F32)
    for h in range(TOK_HEADS):
        lanes = slice(h * n, (h + 1) * n)
        r, k, v, kk, a = r_ref[0, :, lanes], k_ref[0, :, lanes], v_ref[0, :, lanes], kk_ref[0, :, lanes], \
            a_ref[0, :, lanes]
        d = jnp.exp(ld_ref[0, :, lanes])
        s = s_ref[0, h]
        sa = jnp.sum(_bf16_round(s) * _bf16_round(-kk), axis=1, keepdims=True)
        v_col = jnp.sum(eye * v, axis=1, keepdims=True)
        s_new = s * d + sa * (kk * a) + v_col * k
        so_ref[0, h] = s_new
        y_col = jnp.sum(_bf16_round(s_new) * _bf16_round(r), axis=1, keepdims=True)
        y = jnp.sum(eye * y_col, axis=0, keepdims=True)
        o_ref[0, :, lanes] = _rwkv_finish(y, r, k, v, g_ref[0, :, lanes], rk_ref[:, lanes], lnw_ref[:, lanes],
                                          lnb_ref[:, lanes])


def _rwkv_step(r, ld, k, v, kk, a, g, r_k, ln_w, ln_b, state):
    n = r.shape[0]
    tok = pl.BlockSpec((1, 1, QA), lambda i: (i, 0, 0))
    const = pl.BlockSpec((1, QA), lambda i: (0, 0))
    st = pl.BlockSpec((1, TOK_HEADS, HEAD_DIM, HEAD_DIM), lambda i: (i, 0, 0, 0))
    r3 = lambda z: z.reshape(n, 1, QA)
    return pl.pallas_call(
        _rwkv_step_kernel, grid=(n,),
        in_specs=[tok] * 7 + [const] * 3 + [st],
        out_specs=[tok, st],
        out_shape=[jax.ShapeDtypeStruct((n, 1, QA), F32), jax.ShapeDtypeStruct(state.shape, F32)],
        compiler_params=_cparams("parallel"), name="rwkv_step",
    )(r3(r), r3(ld), r3(k), r3(v), r3(kk), r3(a), r3(g), r_k.reshape(1, QA), ln_w.reshape(1, QA),
      ln_b.reshape(1, QA), state)


def _split_w_in_a(w):
    offs = [0, QA]
    for _ in range(6):
        offs.append(offs[-1] + KVW)
    offs.append(offs[-1] + 3 * TOK_HEADS)
    offs.append(offs[-1] + MEMQ)
    pieces = [w[:, offs[i]:offs[i + 1]] for i in range(9)]
    pieces[7] = jnp.pad(pieces[7], ((0, 0), (0, LANES - 3 * TOK_HEADS)))
    return pieces


_A_OPS = ("hnorm", "none", "none", "hnorm", "none", "hnorm", "none", "sigmoid", "none")


def _ffn(x, mix, mem, w_out, g_ffn, wc, bc, wf, bf, w_gate, w_up, w_down, layer, tm_proj, tm_moe):
    x_new, hn, cw = _outproj_router(x, mix, mem, w_out, g_ffn, wc, bc, wf, bf, tm_proj)
    return _moe(x_new, hn, cw, w_gate, w_up, w_down, layer, tm_moe)


def kernel(x_prompt, x_sample, cache_cmp_k, cache_cmp_v, cache_sel_k, cache_sel_v, cache_win_k, cache_win_v, cache_mem_k, cache_mem_v, state_rwkv, state_shift, page_table, mem_prompt, norm_mix, norm_ffn, norm_mem, w_mem_kv, mem_q_gain, mem_k_gain, w_in_a, nsa_q_gain, nsa_k_gain, cmp_pe, cmp_w1, cmp_b1, cmp_w2, w_in_b, rwkv_mu, rwkv_w0, rwkv_w_up, rwkv_a0, rwkv_a_up, rwkv_g_up, rwkv_k_k, rwkv_k_a, rwkv_r_k, rwkv_ln_w, rwkv_ln_b, w_out, moe_w_coarse, moe_b_coarse, moe_w_fine, moe_b_fine, moe_w_gate, moe_w_up, moe_w_down):
    b, s, d = x_prompt.shape
    bd = x_sample.shape[0]
    depth = norm_mix.shape[0]
    m_len = mem_prompt.shape[1]
    wl = min(WINDOW, s)
    xp = x_prompt.reshape(b * s, d)
    xs = x_sample.reshape(bd, d)
    mem2 = mem_prompt.reshape(b * m_len, d)
    outs = {name: [] for name in ("pc_k", "pc_v", "ps_k", "ps_v", "pw_k", "pw_v", "pm_k", "pm_v", "pr_s", "pr_x",
                                  "sc_k", "sc_v", "ss_k", "ss_v", "sw_k", "sw_v", "sr_s", "sr_x")}
    for i in range(depth):
        km_p, vm_p = _norm_proj(mem2, norm_mem[i], [w_mem_kv[i][:, :MEMQ], w_mem_kv[i][:, MEMQ:]],
                                ("hnorm", "none"), [mem_k_gain[i]], 256)
        km_p, vm_p = km_p.reshape(b, m_len, MEMQ), vm_p.reshape(b, m_len, MEMQ)
        outs["pm_k"].append(km_p.reshape(b, m_len, MEM_HEADS, HEAD_DIM))
        outs["pm_v"].append(vm_p.reshape(b, m_len, MEM_HEADS, HEAD_DIM))
        if i % 2 == 0:
            ia = i // 2
            pieces = _split_w_in_a(w_in_a[ia])
            gains = [nsa_q_gain[ia], nsa_k_gain[ia, 1], nsa_k_gain[ia, 2]]
            cmp_args = (cmp_pe[ia], cmp_w1[ia], cmp_b1[ia], cmp_w2[ia], nsa_k_gain[ia, 0])
            q, kc, vc, ks, vs, kw, vw, gt, mq_p = _norm_proj(xp, norm_mix[i], pieces, _A_OPS, gains, 256)
            r3 = lambda z: z.reshape(b, s, -1)
            kc, vc, ks, vs, kw, vw = (r3(z) for z in (kc, vc, ks, vs, kw, vw))
            ck, cv = _compress_prompt(kc, vc, *cmp_args)
            mix_p = _nsa_prompt(r3(q), r3(gt), ck, cv, ks, vs, kw, vw).reshape(b * s, QA)
            r5 = lambda z: z.reshape(b, -1, KV_GROUPS, HEAD_DIM)
            for name, z in (("pc_k", kc), ("pc_v", vc), ("ps_k", ks), ("ps_v", vs),
                            ("pw_k", kw[:, s - wl:]), ("pw_v", vw[:, s - wl:])):
                outs[name].append(r5(z))
            q, kc, vc, ks, vs, kw, vw, gt, mq_s = _norm_proj(xs, norm_mix[i], pieces, _A_OPS, gains, 256)
            ck, cv = _compress_sample(page_table, cache_cmp_k[ia], cache_cmp_v[ia], *cmp_args)
            mix_s, nwk, nwv = _nsa_sample(page_table, q, gt, ck, cv, ks, vs, kw, vw,
                                          cache_sel_k[ia], cache_sel_v[ia], cache_win_k[ia], cache_win_v[ia])
            mix_s = mix_s.reshape(bd, QA)
            r5 = lambda z: z.reshape(bd, -1, KV_GROUPS, HEAD_DIM)
            for name, z in (("sc_k", kc), ("sc_v", vc), ("ss_k", ks), ("ss_v", vs), ("sw_k", nwk), ("sw_v", nwv)):
                outs[name].append(r5(z))
        else:
            ib = i // 2
            pieces = [w_in_b[ib][:, :SHIFT_W], w_in_b[ib][:, SHIFT_W:]]
            wargs = _rwkv_weight_args(rwkv_mu[ib], rwkv_w0[ib], rwkv_w_up[ib], rwkv_a0[ib], rwkv_a_up[ib],
                                      rwkv_g_up[ib], rwkv_k_k[ib], rwkv_k_a[ib])
            fin = (rwkv_r_k[ib].reshape(-1), rwkv_ln_w[ib], rwkv_ln_b[ib])
            pr, mq_p = _norm_proj(xp, norm_mix[i], pieces, ("none", "none"), [], 256)
            pr = pr.reshape(b, s, SHIFT_W)
            prep = _rwkv_prep_prompt(pr, wargs, 256)
            mix_p, st_p = _rwkv_chunk(*prep, *fin)
            mix_p = mix_p.reshape(b * s, QA)
            outs["pr_s"].append(st_p)
            outs["pr_x"].append(pr[:, s - 1])
            pr, mq_s = _norm_proj(xs, norm_mix[i], pieces, ("none", "none"), [], 256)
            prep = _rwkv_prep_sample(pr, state_shift[ib], wargs)
            mix_s, st_s = _rwkv_step(*prep, *fin, state_rwkv[ib])
            mix_s = mix_s.reshape(bd, QA)
            outs["sr_s"].append(st_s)
            outs["sr_x"].append(pr)
        mem_p = _mem_attn(mq_p.reshape(b, s, MEMQ), km_p, vm_p, mem_q_gain[i], 256).reshape(b * s, MEMQ)
        mem_s = _mem_attn(mq_s.reshape(bd, 1, MEMQ), cache_mem_k[i].reshape(bd, -1, MEMQ),
                          cache_mem_v[i].reshape(bd, -1, MEMQ), mem_q_gain[i], 1).reshape(bd, MEMQ)
        ffn_w = (w_out[i], norm_ffn[i], moe_w_coarse[i], moe_b_coarse[i], moe_w_fine[i], moe_b_fine[i],
                 moe_w_gate, moe_w_up, moe_w_down, i)
        xp = _ffn(xp, mix_p, mem_p, *ffn_w, 256, 1024)
        xs = _ffn(xs, mix_s, mem_s, *ffn_w, 256, 1024)
    order = ("pc_k", "pc_v", "ps_k", "ps_v", "pw_k", "pw_v", "pm_k", "pm_v", "pr_s", "pr_x",
             "sc_k", "sc_v", "ss_k", "ss_v", "sw_k", "sw_v", "sr_s", "sr_x")
    return (xp.reshape(b, s, d), xs.reshape(bd, 1, d)) + tuple(jnp.stack(outs[name]) for name in order)
```

```python
import functools

import jax
import jax.numpy as jnp
from jax import lax
from jax.experimental import pallas as pl
from jax.experimental.pallas import tpu as pltpu

F32 = jnp.float32
BF16 = jnp.bfloat16
HI = lax.Precision.HIGHEST

HEAD_DIM = 64
TOK_HEADS = 12
MEM_HEADS = 4
KV_GROUPS = 3
HEADS_PER_GROUP = TOK_HEADS // KV_GROUPS
QA = TOK_HEADS * HEAD_DIM
KVW = KV_GROUPS * HEAD_DIM
MEMQ = MEM_HEADS * HEAD_DIM
CMP_LEN = 32
CMP_STRIDE = 16
CMP_HID = 128
SEL_LEN = 64
SEL_TOPK = 16
WINDOW = 512
Q_BLOCK = 128
NSA_KEY_STEP = 512
PAGE = 128
N_GROUPS = 4
E_PER_GROUP = 8
N_EXPERTS = 32
MOE_TOKEN_TILE = 512
MOE_ROW_TILE = 256
RWKV_COLS = (768, 768, 768, 64, 64, 128)
SHIFT_W = sum(RWKV_COLS)
RWKV_GN_EPS = 64e-5
RWKV_CHUNK = 64
RWKV_ROWS_PER_STEP = 2
RWKV_CHUNK_EXACT = False
NEG = -1e30
BIG = 1e30
LANES = 128
VMEM_LIMIT = 56 * 1024 * 1024


def _cparams(*sem):
    return pltpu.CompilerParams(dimension_semantics=sem, vmem_limit_bytes=VMEM_LIMIT)


def _dot_general(a, b, dims, exact):
    if exact:
        return lax.dot_general(a, b, (dims, ((), ())), precision=HI, preferred_element_type=F32)
    return lax.dot_general(a.astype(BF16), b.astype(BF16), (dims, ((), ())), preferred_element_type=F32)


def _dot(a, b, exact=False):
    return _dot_general(a, b, ((1,), (0,)), exact)


def _dot_nt(a, b, exact=False):
    return _dot_general(a, b, ((1,), (1,)), exact)


def _dot_tn(a, b, exact=False):
    return _dot_general(a, b, ((0,), (0,)), exact)


def _bf16_round(x):
    return x.astype(BF16).astype(F32)


def _rms(x, g, eps=1e-6):
    return x * lax.rsqrt(jnp.mean(x * x, axis=-1, keepdims=True) + eps) * g


def _masked_softmax(s, mask):
    s = jnp.where(mask, s, NEG)
    m = jnp.max(s, axis=-1, keepdims=True)
    e = jnp.where(mask, jnp.exp(s - m), 0.0)
    return e / jnp.maximum(jnp.sum(e, axis=-1, keepdims=True), 1e-30)


def _alibi_slope(h):
    return 2.0 ** (-8.0 * (h + 1) / TOK_HEADS)


def _topk_mask(score, k):
    lane = lax.broadcasted_iota(jnp.int32, score.shape, 1).astype(F32)
    sel = jnp.zeros(score.shape, F32)
    for _ in range(k):
        m = jnp.max(score, axis=-1, keepdims=True)
        idx = jnp.min(jnp.where(score == m, lane, 1e9), axis=-1, keepdims=True)
        hit = lane == idx
        sel = jnp.where(hit, 1.0, sel)
        score = jnp.where(hit, -jnp.inf, score)
    return sel


def _topk_rows_t(score_t, k):
    n_blk = score_t.shape[0]
    idx = lax.broadcasted_iota(jnp.int32, score_t.shape, 0)
    ahead = jnp.zeros(score_t.shape, F32)
    for j in range(n_blk):
        row = score_t[j:j + 1]
        ahead = ahead + jnp.where((row > score_t) | ((row == score_t) & (idx > j)), 1.0, 0.0)
    return jnp.where(ahead < k, 1.0, 0.0)


def _sel_to_cmp(n_sel_rows, n_cmp_lanes):
    s0 = lax.broadcasted_iota(jnp.int32, (n_sel_rows, n_cmp_lanes), 0) * SEL_LEN
    c0 = lax.broadcasted_iota(jnp.int32, (n_sel_rows, n_cmp_lanes), 1) * CMP_STRIDE
    return ((c0 < s0 + SEL_LEN) & (c0 + CMP_LEN > s0)).astype(F32)


def _cmp_to_sel(n_cmp_rows, n_sel_lanes):
    n_i = lax.broadcasted_iota(jnp.int32, (n_cmp_rows, n_sel_lanes), 0)
    s_i = lax.broadcasted_iota(jnp.int32, (n_cmp_rows, n_sel_lanes), 1)
    c0 = n_i * CMP_STRIDE
    s0 = s_i * SEL_LEN
    return ((c0 < s0 + SEL_LEN) & (c0 + CMP_LEN > s0)).astype(F32)


def _block_expand(n_blk_rows, n_keys):
    b_i = lax.broadcasted_iota(jnp.int32, (n_blk_rows, n_keys), 0)
    k_i = lax.broadcasted_iota(jnp.int32, (n_blk_rows, n_keys), 1)
    return (jnp.right_shift(k_i, 6) == b_i).astype(BF16)


def _norm_proj_kernel(ops, x_ref, g_ref, *refs):
    n = len(ops)
    n_gain = sum(op == "hnorm" for op in ops)
    w_refs, gain_refs, o_refs = refs[:n], refs[n:n + n_gain], refs[n + n_gain:]
    h = _rms(x_ref[...], g_ref[...])
    gi = 0
    for op, w_ref, o_ref in zip(ops, w_refs, o_refs):
        z = _dot(h, w_ref[...])
        if op == "hnorm":
            gain = gain_refs[gi][...]
            gi += 1
            for hh in range(z.shape[1] // HEAD_DIM):
                seg = z[:, hh * HEAD_DIM:(hh + 1) * HEAD_DIM]
                o_ref[:, hh * HEAD_DIM:(hh + 1) * HEAD_DIM] = _rms(seg, gain)
        elif op == "sigmoid":
            o_ref[...] = jax.nn.sigmoid(z)
        else:
            o_ref[...] = z


def _norm_proj(x, g, weights, ops, gains, tm):
    n_rows, d = x.shape
    tm = min(tm, n_rows)
    assert n_rows % tm == 0
    in_specs = [pl.BlockSpec((tm, d), lambda i: (i, 0)), pl.BlockSpec((1, d), lambda i: (0, 0))]
    in_specs += [pl.BlockSpec(w.shape, lambda i: (0, 0)) for w in weights]
    in_specs += [pl.BlockSpec((1, HEAD_DIM), lambda i: (0, 0)) for _ in gains]
    out_shape = [jax.ShapeDtypeStruct((n_rows, w.shape[1]), F32) for w in weights]
    out_specs = [pl.BlockSpec((tm, w.shape[1]), lambda i: (i, 0)) for w in weights]
    return pl.pallas_call(
        functools.partial(_norm_proj_kernel, tuple(ops)),
        grid=(n_rows // tm,), in_specs=in_specs, out_specs=out_specs, out_shape=out_shape,
        compiler_params=_cparams("parallel"), name="norm_proj",
    )(x, g.reshape(1, d), *weights, *[gn.reshape(1, HEAD_DIM) for gn in gains])


def _compress_rows(rows_ref, pe_ref, w1_ref, b1_ref, w2_ref, gain, o_ref, u_ref, n_chunk):
    for g in range(KV_GROUPS):
        for l in range(CMP_STRIDE):
            src = l * KVW + g * HEAD_DIM
            u_ref[:, l * HEAD_DIM:(l + 1) * HEAD_DIM] = rows_ref[:, src:src + HEAD_DIM]
        _compress_mlp(u_ref, pe_ref, w1_ref, b1_ref, w2_ref, gain, o_ref, g, n_chunk)


def _compress_mlp(u_ref, pe_ref, w1_ref, b1_ref, w2_ref, gain, o_ref, g, n_chunk):
    half = CMP_STRIDE * HEAD_DIM
    u = u_ref[...]
    p0 = _dot(u + pe_ref[:, 0:half], w1_ref[0:half, :])
    p1 = _dot(u + pe_ref[:, half:2 * half], w1_ref[half:2 * half, :])
    hid = b1_ref[...] + p0 + pltpu.roll(p1, n_chunk - 1, 0)
    z = _dot(jax.nn.gelu(hid), w2_ref[...])
    if gain is not None:
        z = _rms(z, gain)
    o_ref[:, g * HEAD_DIM:(g + 1) * HEAD_DIM] = z


def _compress_cols(xt_ref, pe_ref, w1_ref, b1_ref, w2_ref, gain, o_ref, xs_ref, u_ref, n_chunk):
    for first, groups in ((0, (0, 1)), (HEAD_DIM, (2,))):
        xs_ref[...] = xt_ref[first:first + 2 * HEAD_DIM, :].T
        for g in groups:
            lane0 = g * HEAD_DIM - first
            for l in range(CMP_STRIDE):
                rows = xs_ref[pl.ds(l, n_chunk, stride=CMP_STRIDE), :]
                u_ref[:, l * HEAD_DIM:(l + 1) * HEAD_DIM] = rows[:, lane0:lane0 + HEAD_DIM]
            _compress_mlp(u_ref, pe_ref, w1_ref, b1_ref, w2_ref, gain, o_ref, g, n_chunk)


def _compress_prompt_kernel(k_ref, v_ref, pe_ref, w1_ref, b1_ref, w2_ref, gain_ref, ck_ref, cv_ref, u_ref,
                            *, n_chunk):
    _compress_rows(k_ref.at[0], pe_ref.at[0], w1_ref.at[0], b1_ref.at[0], w2_ref.at[0], gain_ref[...],
                   ck_ref.at[0], u_ref, n_chunk)
    _compress_rows(v_ref.at[0], pe_ref.at[1], w1_ref.at[1], b1_ref.at[1], w2_ref.at[1], None,
                   cv_ref.at[0], u_ref, n_chunk)


def _compress_weight_specs(nidx):
    zero = lambda *_: (0, 0, 0)
    del nidx
    return [pl.BlockSpec((2, 1, CMP_LEN * HEAD_DIM), zero),
            pl.BlockSpec((2, CMP_LEN * HEAD_DIM, CMP_HID), zero),
            pl.BlockSpec((2, 1, CMP_HID), zero),
            pl.BlockSpec((2, CMP_HID, HEAD_DIM), zero)]


def _compress_prompt(kc, vc, pe, w1, b1, w2, gain):
    b, s, _ = kc.shape
    n_chunk = s // CMP_STRIDE
    cw = CMP_STRIDE * KVW
    rows = pl.BlockSpec((1, n_chunk, cw), lambda i: (i, 0, 0))
    out = pl.BlockSpec((1, n_chunk, KVW), lambda i: (i, 0, 0))
    return pl.pallas_call(
        functools.partial(_compress_prompt_kernel, n_chunk=n_chunk),
        grid=(b,),
        in_specs=[rows, rows] + _compress_weight_specs(1) + [pl.BlockSpec((1, HEAD_DIM), lambda i: (0, 0))],
        out_specs=[out, out],
        out_shape=[jax.ShapeDtypeStruct((b, n_chunk, KVW), F32)] * 2,
        scratch_shapes=[pltpu.VMEM((n_chunk, CMP_STRIDE * HEAD_DIM), F32)],
        compiler_params=_cparams("parallel"), name="compress_prompt",
    )(kc.reshape(b, n_chunk, cw), vc.reshape(b, n_chunk, cw), pe.reshape(2, 1, -1), w1, b1.reshape(2, 1, -1), w2,
      gain.reshape(1, HEAD_DIM))


def _compress_sample_kernel(pt_ref, pk_ref, pv_ref, pe_ref, w1_ref, b1_ref, w2_ref, gain_ref,
                            ck_ref, cv_ref, kbuf, vbuf, xs_ref, u_ref, sem, *, n_pages):
    b = pl.program_id(0)
    n_chunk = n_pages * PAGE // CMP_STRIDE

    def gather(pool_ref, dst_ref, dma_sem):
        copies = []
        for j in range(n_pages):
            cp = pltpu.make_async_copy(pool_ref.at[pt_ref[b, j]], dst_ref.at[:, pl.ds(j * PAGE, PAGE)], dma_sem)
            cp.start()
            copies.append(cp)
        return copies

    ck_copies = gather(pk_ref, kbuf, sem.at[0])
    cv_copies = gather(pv_ref, vbuf, sem.at[1])
    for cp in ck_copies:
        cp.wait()
    _compress_cols(kbuf, pe_ref.at[0], w1_ref.at[0], b1_ref.at[0], w2_ref.at[0], gain_ref[...],
                   ck_ref.at[0], xs_ref, u_ref, n_chunk)
    for cp in cv_copies:
        cp.wait()
    _compress_cols(vbuf, pe_ref.at[1], w1_ref.at[1], b1_ref.at[1], w2_ref.at[1], None,
                   cv_ref.at[0], xs_ref, u_ref, n_chunk)


def _compress_sample(page_table, pool_k, pool_v, pe, w1, b1, w2, gain):
    bd, n_pages = page_table.shape
    n_pool = pool_k.shape[0]
    n_chunk = n_pages * PAGE // CMP_STRIDE
    past = n_pages * PAGE
    pos_minor = lambda z: jnp.transpose(z, (0, 2, 3, 1)).reshape(n_pool, KVW, PAGE)
    zero3 = lambda i, pt: (0, 0, 0)
    any_spec = pl.BlockSpec(memory_space=pl.ANY)
    out = pl.BlockSpec((1, n_chunk, KVW), lambda i, pt: (i, 0, 0))
    grid_spec = pltpu.PrefetchScalarGridSpec(
        num_scalar_prefetch=1, grid=(bd,),
        in_specs=[any_spec, any_spec,
                  pl.BlockSpec((2, 1, CMP_LEN * HEAD_DIM), zero3),
                  pl.BlockSpec((2, CMP_LEN * HEAD_DIM, CMP_HID), zero3),
                  pl.BlockSpec((2, 1, CMP_HID), zero3),
                  pl.BlockSpec((2, CMP_HID, HEAD_DIM), zero3),
                  pl.BlockSpec((1, HEAD_DIM), lambda i, pt: (0, 0))],
        out_specs=[out, out],
        scratch_shapes=[pltpu.VMEM((KVW, past), F32), pltpu.VMEM((KVW, past), F32),
                        pltpu.VMEM((past, LANES), F32),
                        pltpu.VMEM((n_chunk, CMP_STRIDE * HEAD_DIM), F32), pltpu.SemaphoreType.DMA((2,))])
    return pl.pallas_call(
        functools.partial(_compress_sample_kernel, n_pages=n_pages),
        grid_spec=grid_spec,
        out_shape=[jax.ShapeDtypeStruct((bd, n_chunk, KVW), F32)] * 2,
        compiler_params=_cparams("arbitrary"), name="compress_sample",
    )(page_table, pos_minor(pool_k), pos_minor(pool_v),
      pe.reshape(2, 1, -1), w1, b1.reshape(2, 1, -1), w2, gain.reshape(1, HEAD_DIM))


def _softmax_rows(s):
    e = jnp.exp(s - jnp.max(s, axis=-1, keepdims=True))
    return e / jnp.sum(e, axis=-1, keepdims=True)


def _nsa_prompt_block(q_ref, gt_ref, ck_ref, cv_ref, ks_ref, vs_ref, kw_ref, vw_ref, o_ref,
                      *, seq, n_keys, first_block):
    tq = Q_BLOCK
    hpg = HEADS_PER_GROUP
    n_cmp = ck_ref.shape[1]
    n_sel = seq // SEL_LEN
    wlen = min(WINDOW + tq, seq)
    q0 = (pl.program_id(1) + first_block) * tq
    t1 = q0 + lax.broadcasted_iota(jnp.int32, (tq, 1), 0)
    s2c = _sel_to_cmp(n_sel, n_cmp)
    expand = _block_expand(n_sel, n_keys)
    blk = lax.broadcasted_iota(jnp.int32, (n_sel, 1), 0)
    t_row = q0 + lax.broadcasted_iota(jnp.int32, (1, tq), 1)
    cur = jnp.right_shift(t_row, 6)
    forced = (blk == 0) | (blk == cur) | (blk == cur - 1)
    valid = blk * SEL_LEN <= t_row
    dist_c = t1 - (lax.broadcasted_iota(jnp.int32, (1, n_cmp), 1) * CMP_STRIDE + (CMP_LEN - 1))
    mask_c = dist_c >= 0
    dist_cf = dist_c.astype(F32)
    dist_s = t1 - lax.broadcasted_iota(jnp.int32, (1, n_keys), 1)
    causal_s = dist_s >= 0
    dist_sf = dist_s.astype(F32)
    w_start = pl.multiple_of(jnp.clip(q0 - WINDOW, 0, seq - wlen), tq)
    dist_w = t1 - (w_start + lax.broadcasted_iota(jnp.int32, (1, wlen), 1))
    bias_w = jnp.where((dist_w >= 0) & (dist_w <= WINDOW), 0.0, NEG)
    dist_wf = dist_w.astype(F32)
    gt = gt_ref[0]
    for g in range(KV_GROUPS):
        lanes = slice(g * HEAD_DIM, (g + 1) * HEAD_DIM)
        heads = [g * hpg + j for j in range(hpg)]
        rows = [slice(j * tq, (j + 1) * tq) for j in range(hpg)]
        q4 = jnp.concatenate([q_ref[0, :, h * HEAD_DIM:(h + 1) * HEAD_DIM] for h in heads],
                             axis=0) * (HEAD_DIM ** -0.5)
        s_c = _dot_nt(q4, ck_ref[0, :, lanes])
        p_c = [_masked_softmax(s_c[rows[j]] - _alibi_slope(heads[j]) * dist_cf, mask_c) for j in range(hpg)]
        cv = cv_ref[0, :, lanes]
        o_c = [_dot(p, cv) for p in p_c]
        imps = [_dot_nt(s2c, p) for p in p_c]
        imp = imps[0]
        for j in range(1, hpg):
            imp = imp + imps[j]
        score = jnp.where(valid, jnp.where(forced, BIG, imp), -BIG)
        key_sel = _dot_tn(_topk_rows_t(score, min(SEL_TOPK, n_sel)), expand)
        bias_s = jnp.where((key_sel > 0.5) & causal_s, 0.0, NEG)
        s_s = _dot_nt(q4, ks_ref[0, 0:n_keys, lanes])
        vs = vs_ref[0, 0:n_keys, lanes]
        o_s = [_dot(_softmax_rows(s_s[rows[j]] - _alibi_slope(heads[j]) * dist_sf + bias_s), vs)
               for j in range(hpg)]
        s_w = _dot_nt(q4, kw_ref[0, pl.ds(w_start, wlen), lanes])
        vw = vw_ref[0, pl.ds(w_start, wlen), lanes]
        o_w = [_dot(_softmax_rows(s_w[rows[j]] - _alibi_slope(heads[j]) * dist_wf + bias_w), vw)
               for j in range(hpg)]
        for j, h in enumerate(heads):
            o_ref[0, :, h * HEAD_DIM:(h + 1) * HEAD_DIM] = (
                gt[:, 3 * h:3 * h + 1] * o_c[j] + gt[:, 3 * h + 1:3 * h + 2] * o_s[j]
                + gt[:, 3 * h + 2:3 * h + 3] * o_w[j])


def _nsa_prompt(q, gates, ck, cv, ks, vs, kw, vw):
    b, s, _ = q.shape
    n_cmp = ck.shape[1]
    key_step = min(NSA_KEY_STEP, s)
    per_class = key_step // Q_BLOCK
    full = lambda n: pl.BlockSpec((1, n, KVW), lambda i, j: (i, 0, 0))
    outs = []
    for c in range(s // key_step):
        n_keys = (c + 1) * key_step
        first = c * per_class
        qspec = lambda w, first=first: pl.BlockSpec((1, Q_BLOCK, w), lambda i, j: (i, j + first, 0))
        outs.append(pl.pallas_call(
            functools.partial(_nsa_prompt_block, seq=s, n_keys=n_keys, first_block=first),
            grid=(b, per_class),
            in_specs=[qspec(QA), qspec(LANES), full(n_cmp), full(n_cmp), full(n_keys), full(n_keys), full(s), full(s)],
            out_specs=pl.BlockSpec((1, Q_BLOCK, QA), lambda i, j: (i, j, 0)),
            out_shape=jax.ShapeDtypeStruct((b, key_step, QA), F32),
            compiler_params=_cparams("parallel", "parallel"), name=f"nsa_prompt_{n_keys}",
        )(q, gates, ck, cv, ks, vs, kw, vw))
    return jnp.concatenate(outs, axis=1)


def _gather_pages_t(pt_ref, b, pool_ref, dst_ref, sem, n_pages):
    copies = []
    for j in range(n_pages):
        cp = pltpu.make_async_copy(pool_ref.at[pt_ref[b, j]], dst_ref.at[:, :, pl.ds(j * PAGE, PAGE)], sem)
        cp.start()
        copies.append(cp)
    return copies


def _nsa_sample_kernel(pt_ref, q_ref, gt_ref, ck_ref, cv_ref, ksn_ref, vsn_ref, kwn_ref, vwn_ref,
                       pks_ref, pvs_ref, wk_ref, wv_ref, o_ref, owk_ref, owv_ref, kbuf, vbuf, sem,
                       *, n_pages):
    b = pl.program_id(0)
    hpg = HEADS_PER_GROUP
    past = n_pages * PAGE
    n_cmp = ck_ref.shape[1]
    n_sel_past = past // SEL_LEN
    n_sel = n_sel_past + 1
    wb = wk_ref.shape[3]
    sel_lanes = 2 * LANES
    k_copies = _gather_pages_t(pt_ref, b, pks_ref, kbuf, sem.at[0], n_pages)
    v_copies = _gather_pages_t(pt_ref, b, pvs_ref, vbuf, sem.at[1], n_pages)

    row = lax.broadcasted_iota(jnp.int32, (8, 1), 0)
    n = HEAD_DIM
    eye = (lax.broadcasted_iota(jnp.int32, (n, n), 0) == lax.broadcasted_iota(jnp.int32, (n, n), 1)).astype(F32)
    gt = gt_ref[0]
    c_end = lax.broadcasted_iota(jnp.int32, (1, n_cmp), 1) * CMP_STRIDE + (CMP_LEN - 1)
    dist_c = past - c_end
    blk = lax.broadcasted_iota(jnp.int32, (1, sel_lanes), 1)
    cur = past // SEL_LEN
    forced = (blk == 0) | (blk == cur) | (blk == cur - 1)
    valid = blk * SEL_LEN <= past
    c2s = _cmp_to_sel(n_cmp, sel_lanes)

    def q_rows(g):
        q4 = jnp.concatenate(
            [q_ref[0, :, (g * hpg + j) * HEAD_DIM:(g * hpg + j + 1) * HEAD_DIM] for j in range(hpg)]
            + [jnp.zeros((8 - hpg, HEAD_DIM), F32)], axis=0) * (HEAD_DIM ** -0.5)
        slope = jnp.full((8, 1), _alibi_slope(g * hpg + hpg - 1), F32)
        for j in range(hpg - 2, -1, -1):
            slope = jnp.where(row < j + 1, _alibi_slope(g * hpg + j), slope)
        return q4, slope

    o_cs, imps = [], []
    for g in range(KV_GROUPS):
        lanes = slice(g * HEAD_DIM, (g + 1) * HEAD_DIM)
        q4, slope = q_rows(g)
        p_c = _masked_softmax(_dot_nt(q4, ck_ref[0, :, lanes]) - slope * dist_c.astype(F32), dist_c >= 0)
        o_cs.append(_dot(p_c, cv_ref[0, :, lanes]))
        imps.append(jnp.sum(jnp.where(row < hpg, _dot(p_c, c2s), 0.0), axis=0, keepdims=True))
    imp = jnp.concatenate(imps + [jnp.zeros((8 - KV_GROUPS, sel_lanes), F32)], axis=0)
    score = jnp.where(valid, jnp.where(forced, BIG, imp), -BIG)
    score = jnp.where(blk < n_sel, score, -jnp.inf)
    sel = _topk_mask(score, min(SEL_TOPK, n_sel))
    key_sel = _dot(sel[:, 0:n_sel_past], _block_expand(n_sel_past, past))
    sel_new = sel[:, n_sel_past:n_sel_past + 1]

    pos_s = lax.broadcasted_iota(jnp.int32, (1, past), 1)
    dist_s = (past - pos_s).astype(F32)
    lane_w = lax.broadcasted_iota(jnp.int32, (1, wb), 1)
    pos_w = past - wb + lane_w
    dist_w = past - pos_w
    mask_w = (dist_w >= 0) & (dist_w <= WINDOW) & (pos_w >= 0)
    for cp in k_copies:
        cp.wait()
    for cp in v_copies:
        cp.wait()
    for g in range(KV_GROUPS):
        lanes = slice(g * HEAD_DIM, (g + 1) * HEAD_DIM)
        q4, slope = q_rows(g)
        q4r = _bf16_round(q4)
        s_p = jnp.where(key_sel[g:g + 1] > 0.5, _dot(q4, kbuf[g]) - slope * dist_s, NEG)
        new_ok = sel_new[g:g + 1] > 0.5
        s_n = jnp.where(new_ok, jnp.sum(q4r * _bf16_round(ksn_ref[0, :, lanes]), axis=-1, keepdims=True), NEG)
        m = jnp.maximum(jnp.max(s_p, axis=-1, keepdims=True), s_n)
        e_p = jnp.where(key_sel[g:g + 1] > 0.5, jnp.exp(s_p - m), 0.0)
        e_n = jnp.where(new_ok, jnp.exp(s_n - m), 0.0)
        den = jnp.maximum(jnp.sum(e_p, axis=-1, keepdims=True) + e_n, 1e-30)
        o_s = _dot_nt(e_p / den, vbuf[g]) + _bf16_round(e_n / den) * _bf16_round(vsn_ref[0, :, lanes])
        s_p = jnp.where(mask_w, _dot(q4, wk_ref[0, g]) - slope * dist_w.astype(F32), NEG)
        s_n = jnp.sum(q4r * _bf16_round(kwn_ref[0, :, lanes]), axis=-1, keepdims=True)
        m = jnp.maximum(jnp.max(s_p, axis=-1, keepdims=True), s_n)
        e_p = jnp.where(mask_w, jnp.exp(s_p - m), 0.0)
        e_n = jnp.exp(s_n - m)
        den = jnp.maximum(jnp.sum(e_p, axis=-1, keepdims=True) + e_n, 1e-30)
        o_w = _dot_nt(e_p / den, wv_ref[0, g]) + _bf16_round(e_n / den) * _bf16_round(vwn_ref[0, :, lanes])
        for j in range(hpg):
            h = g * hpg + j
            o_ref[0, :, h * HEAD_DIM:(h + 1) * HEAD_DIM] = (
                gt[:, 3 * h:3 * h + 1] * o_cs[g][j:j + 1] + gt[:, 3 * h + 1:3 * h + 2] * o_s[j:j + 1]
                + gt[:, 3 * h + 2:3 * h + 3] * o_w[j:j + 1])
        k_col = jnp.sum(eye * kwn_ref[0, :, lanes], axis=1, keepdims=True)
        v_col = jnp.sum(eye * vwn_ref[0, :, lanes], axis=1, keepdims=True)
        owk_ref[0, g] = jnp.where(lane_w == wb - 1, k_col, pltpu.roll(wk_ref[0, g], wb - 1, 1))
        owv_ref[0, g] = jnp.where(lane_w == wb - 1, v_col, pltpu.roll(wv_ref[0, g], wb - 1, 1))


def _nsa_sample(page_table, q, gates, ck, cv, ks_new, vs_new, kw_new, vw_new, pool_ks, pool_vs, win_k, win_v):
    bd, n_pages = page_table.shape
    n_cmp = ck.shape[1]
    wb = win_k.shape[1]
    one = lambda w: pl.BlockSpec((1, 1, w), lambda i, pt: (i, 0, 0))
    rows = lambda m: pl.BlockSpec((1, m, KVW), lambda i, pt: (i, 0, 0))
    win = pl.BlockSpec((1, KV_GROUPS, HEAD_DIM, wb), lambda i, pt: (i, 0, 0, 0))
    any_spec = pl.BlockSpec(memory_space=pl.ANY)
    grid_spec = pltpu.PrefetchScalarGridSpec(
        num_scalar_prefetch=1, grid=(bd,),
        in_specs=[one(QA), one(LANES), rows(n_cmp), rows(n_cmp), one(KVW), one(KVW), one(KVW), one(KVW),
                  any_spec, any_spec, win, win],
        out_specs=[one(QA), win, win],
        scratch_shapes=[pltpu.VMEM((KV_GROUPS, HEAD_DIM, n_pages * PAGE), F32),
                        pltpu.VMEM((KV_GROUPS, HEAD_DIM, n_pages * PAGE), F32),
                        pltpu.SemaphoreType.DMA((2,))])
    r3 = lambda z: z.reshape(bd, 1, -1)
    pos_minor = lambda z: jnp.transpose(z, (0, 2, 3, 1))
    win_shape = jax.ShapeDtypeStruct((bd, KV_GROUPS, HEAD_DIM, wb), F32)
    mix, nwk, nwv = pl.pallas_call(
        functools.partial(_nsa_sample_kernel, n_pages=n_pages),
        grid_spec=grid_spec,
        out_shape=[jax.ShapeDtypeStruct((bd, 1, QA), F32), win_shape, win_shape],
        compiler_params=_cparams("arbitrary"), name="nsa_sample",
    )(page_table, r3(q), r3(gates), ck, cv, r3(ks_new), r3(vs_new), r3(kw_new), r3(vw_new),
      pos_minor(pool_ks), pos_minor(pool_vs), pos_minor(win_k), pos_minor(win_v))
    back = lambda z: jnp.transpose(z, (0, 3, 1, 2))
    return mix, back(nwk), back(nwv)


def _mem_attn_kernel(q_ref, k_ref, v_ref, gain_ref, o_ref):
    tq = q_ref.shape[1]
    pad = max(8 - tq, 0)
    for h in range(MEM_HEADS):
        lanes = slice(h * HEAD_DIM, (h + 1) * HEAD_DIM)
        q = _rms(q_ref[0, :, lanes], gain_ref[...]) * (HEAD_DIM ** -0.5)
        if pad:
            q = jnp.concatenate([q, jnp.zeros((pad, HEAD_DIM), F32)], axis=0)
        s = _dot_nt(q, k_ref[0, :, lanes])
        e = jnp.exp(s - jnp.max(s, axis=-1, keepdims=True))
        p = e / jnp.sum(e, axis=-1, keepdims=True)
        o_ref[0, :, lanes] = _dot(p, v_ref[0, :, lanes])[0:tq]


def _mem_attn(mq, km, vm, gain, tq):
    b, t, _ = mq.shape
    m = km.shape[1]
    tq = min(tq, t)
    qspec = pl.BlockSpec((1, tq, MEMQ), lambda i, j: (i, j, 0))
    kspec = pl.BlockSpec((1, m, MEMQ), lambda i, j: (i, 0, 0))
    return pl.pallas_call(
        _mem_attn_kernel, grid=(b, t // tq),
        in_specs=[qspec, kspec, kspec, pl.BlockSpec((1, HEAD_DIM), lambda i, j: (0, 0))],
        out_specs=qspec, out_shape=jax.ShapeDtypeStruct((b, t, MEMQ), F32),
        compiler_params=_cparams("parallel", "parallel"), name="mem_attn",
    )(mq, km, vm, gain.reshape(1, HEAD_DIM))


def _outproj_router_kernel(x_ref, mix_ref, mem_ref, wo_ref, g_ref, wc_ref, bc_ref, wf_ref, bf_ref,
                           xo_ref, hn_ref, cw_ref):
    x = x_ref[...] + _dot(mix_ref[...], wo_ref[0:QA, :]) + _dot(mem_ref[...], wo_ref[QA:QA + MEMQ, :])
    xo_ref[...] = x
    hn = _rms(x, g_ref[...])
    hn_ref[...] = hn
    lane = lax.broadcasted_iota(jnp.int32, (1, LANES), 1).astype(F32)
    lg = jnp.where(lane < N_GROUPS, _dot(hn, wc_ref[...]) + bc_ref[...], -jnp.inf)
    m = jnp.max(lg, axis=-1, keepdims=True)
    grp = jnp.min(jnp.where(lg == m, lane, 1e9), axis=-1, keepdims=True)
    p_grp = 1.0 / jnp.sum(jnp.exp(lg - m), axis=-1, keepdims=True)
    in_grp = (lane >= grp * E_PER_GROUP) & (lane < (grp + 1.0) * E_PER_GROUP)
    lf = jnp.where(in_grp, _dot(hn, wf_ref[...]) + bf_ref[...], -jnp.inf)
    v1 = jnp.max(lf, axis=-1, keepdims=True)
    i1 = jnp.min(jnp.where(lf == v1, lane, 1e9), axis=-1, keepdims=True)
    lf2 = jnp.where(lane == i1, -jnp.inf, lf)
    v2 = jnp.max(lf2, axis=-1, keepdims=True)
    i2 = jnp.min(jnp.where(lf2 == v2, lane, 1e9), axis=-1, keepdims=True)
    e2 = jnp.exp(v2 - v1)
    den = 1.0 + e2
    cw_ref[...] = jnp.where(lane == i1, p_grp / den, 0.0) + jnp.where(lane == i2, p_grp * (e2 / den), 0.0)


def _outproj_router(x, mix, mem, w_out, g_ffn, w_coarse, b_coarse, w_fine, b_fine, tm):
    n, d = x.shape
    tm = min(tm, n)
    pad_w = lambda w: jnp.pad(w, ((0, 0), (0, LANES - w.shape[1])))
    pad_b = lambda v: jnp.pad(v, (0, LANES - v.shape[0])).reshape(1, LANES)
    row = lambda w: pl.BlockSpec((tm, w), lambda i: (i, 0))
    const = lambda r, c: pl.BlockSpec((r, c), lambda i: (0, 0))
    return pl.pallas_call(
        _outproj_router_kernel, grid=(n // tm,),
        in_specs=[row(d), row(QA), row(MEMQ), const(QA + MEMQ, d), const(1, d), const(d, LANES),
                  const(1, LANES), const(d, LANES), const(1, LANES)],
        out_specs=[row(d), row(d), row(LANES)],
        out_shape=[jax.ShapeDtypeStruct((n, d), F32), jax.ShapeDtypeStruct((n, d), F32),
                   jax.ShapeDtypeStruct((n, LANES), F32)],
        compiler_params=_cparams("parallel"), name="outproj_router",
    )(x, mix, mem, w_out, g_ffn.reshape(1, d), pad_w(w_coarse), pad_b(b_coarse), pad_w(w_fine), pad_b(b_fine))


def _moe_kernel(x_ref, hn_ref, cw_ref, wg_ref, wu_ref, wd_ref, o_ref):
    e = pl.program_id(1)

    @pl.when(e == 0)
    def _():
        o_ref[...] = x_ref[...]

    hb = hn_ref[...].astype(BF16)
    gate = _dot(hb, wg_ref[0])
    up = _dot(hb, wu_ref[0])
    lane = lax.broadcasted_iota(jnp.int32, (1, LANES), 1)
    c = jnp.sum(jnp.where(lane == e, cw_ref[...], 0.0), axis=-1, keepdims=True)
    o_ref[...] += _dot(gate * jax.nn.sigmoid(gate) * up, wd_ref[0]) * c


def _moe(x, hn, cw, w_gate, w_up, w_down, layer, tm):
    n, d = x.shape
    tm = min(tm, n)
    _, n_exp, _, d_exp = w_gate.shape
    row = lambda w: pl.BlockSpec((tm, w), lambda i, e: (i, 0))
    return pl.pallas_call(
        _moe_kernel, grid=(n // tm, n_exp),
        in_specs=[row(d), row(d), row(LANES),
                  pl.BlockSpec((None, 1, d, d_exp), lambda i, e: (layer, e, 0, 0)),
                  pl.BlockSpec((None, 1, d, d_exp), lambda i, e: (layer, e, 0, 0)),
                  pl.BlockSpec((None, 1, d_exp, d), lambda i, e: (layer, e, 0, 0))],
        out_specs=row(d), out_shape=jax.ShapeDtypeStruct((n, d), F32),
        compiler_params=_cparams("parallel", "arbitrary"), name="moe",
    )(x, hn, cw, w_gate, w_up, w_down)


def _moe_plan_kernel(cw_ref, posa_ref, posb_ref, te_ref, nu_ref, cnt_ref, run_ref, start_ref, *, row_tile, n_rows):
    phase, i = pl.program_id(0), pl.program_id(1)
    tm = cw_ref.shape[0]
    cwt = cw_ref[...].T
    mask = cwt != 0.0
    maskf = mask.astype(F32)
    tile_cnt = jnp.sum(maskf, axis=1, keepdims=True)

    @pl.when((phase == 0) & (i == 0))
    def _():
        cnt_ref[...] = jnp.zeros_like(cnt_ref)

    @pl.when(phase == 0)
    def _():
        cnt_ref[...] += tile_cnt

    @pl.when((phase == 1) & (i == 0))
    def _():
        padded = jnp.floor((cnt_ref[...] + (row_tile - 1)) * (1.0 / row_tile)) * row_tile
        lower = (lax.broadcasted_iota(jnp.int32, (LANES, LANES), 0)
                 > lax.broadcasted_iota(jnp.int32, (LANES, LANES), 1)).astype(F32)
        start = _dot(lower, jnp.broadcast_to(padded, (LANES, LANES)), exact=True)[:, 0:1]
        start_ref[...] = start
        run_ref[...] = jnp.zeros_like(run_ref)
        tile_lo = lax.broadcasted_iota(jnp.int32, (1, 2 * LANES), 1).astype(F32) * row_tile
        n_done = jnp.sum(jnp.where(start + padded <= tile_lo, 1.0, 0.0), axis=0, keepdims=True)
        te_ref[...] = jnp.minimum(n_done, N_EXPERTS - 1.0).astype(jnp.int32)
        total = jnp.sum(padded, axis=0, keepdims=True)
        nu_ref[...] = jnp.broadcast_to(total * (1.0 / row_tile), (1, LANES)).astype(jnp.int32)

    @pl.when(phase == 1)
    def _():
        before = (lax.broadcasted_iota(jnp.int32, (tm, tm), 0)
                  < lax.broadcasted_iota(jnp.int32, (tm, tm), 1)).astype(BF16)
        pos = start_ref[...] + run_ref[...] + _dot(maskf, before)
        posa = jnp.min(jnp.where(mask, pos, 3e38), axis=0, keepdims=True)
        posb = jnp.max(jnp.where(mask, pos, -1.0), axis=0, keepdims=True)
        posb = jnp.where(jnp.sum(maskf, axis=0, keepdims=True) > 1.5, posb, n_rows - 1.0)
        posa_ref[0] = posa.astype(jnp.int32)
        posb_ref[0] = posb.astype(jnp.int32)
        run_ref[...] += tile_cnt


def _moe_row_copies(pos_refs, make_copy, tm):
    def issue(t, carry):
        for pos_ref in pos_refs:
            make_copy(pos_ref, t).start()
        return carry

    def drain(t, carry):
        for pos_ref in pos_refs:
            make_copy(pos_ref, 0).wait()
        return carry

    lax.fori_loop(0, tm, issue, 0, unroll=8)
    lax.fori_loop(0, tm, drain, 0, unroll=8)


def _moe_dispatch_kernel(posa_ref, posb_ref, hn_ref, xs_zero_ref, xs_ref, sem):
    del xs_zero_ref

    def make_copy(pos_ref, t):
        return pltpu.make_async_copy(hn_ref.at[pl.ds(t, 1)], xs_ref.at[pl.ds(pos_ref[0, 0, t], 1)], sem)

    _moe_row_copies((posa_ref, posb_ref), make_copy, hn_ref.shape[0])


def _moe_experts_kernel(te_ref, nu_ref, xs_ref, wg_ref, wu_ref, wd_ref, ys_ref):
    del te_ref
    used = pl.program_id(0) < nu_ref[0]

    @pl.when(used)
    def _():
        hb = xs_ref[...].astype(BF16)
        gate = _dot(hb, wg_ref[0])
        up = _dot(hb, wu_ref[0])
        ys_ref[...] = _dot(gate * jax.nn.sigmoid(gate) * up, wd_ref[0])

    @pl.when(jnp.logical_not(used))
    def _():
        ys_ref[...] = jnp.zeros_like(ys_ref)


def _moe_combine_kernel(posa_ref, posb_ref, x_ref, cw_ref, ys_ref, o_ref, ya_ref, yb_ref, sem):
    def make_copy(pos_ref, t):
        dst = ya_ref if pos_ref is posa_ref else yb_ref
        return pltpu.make_async_copy(ys_ref.at[pl.ds(pos_ref[0, 0, t], 1)], dst.at[pl.ds(t, 1)], sem)

    _moe_row_copies((posa_ref, posb_ref), make_copy, x_ref.shape[0])
    cw = cw_ref[...]
    lane = lax.broadcasted_iota(jnp.int32, (1, LANES), 1).astype(F32)
    routed = cw != 0.0
    ea = jnp.min(jnp.where(routed, lane, 1e9), axis=-1, keepdims=True)
    eb = jnp.max(jnp.where(routed, lane, -1.0), axis=-1, keepdims=True)
    wa = jnp.sum(jnp.where(lane == ea, cw, 0.0), axis=-1, keepdims=True)
    wb = jnp.sum(jnp.where((lane == eb) & (eb != ea), cw, 0.0), axis=-1, keepdims=True)
    o_ref[...] = x_ref[...] + (wa * ya_ref[...] + wb * yb_ref[...])


def _moe_routed(x, hn, cw, w_gate, w_up, w_down, layer):
    n, d = x.shape
    _, n_exp, _, d_exp = w_gate.shape
    tm, tr = MOE_TOKEN_TILE, MOE_ROW_TILE
    n_tiles = n // tm
    n_row_tiles = (2 * n + n_exp * (tr - 1)) // tr + 1
    n_rows = n_row_tiles * tr
    assert n % tm == 0 and n_row_tiles <= 2 * LANES
    pos_shape = jax.ShapeDtypeStruct((n_tiles, 1, tm), jnp.int32)
    posa, posb, tile_expert, n_used = pl.pallas_call(
        functools.partial(_moe_plan_kernel, row_tile=tr, n_rows=n_rows),
        grid=(2, n_tiles),
        in_specs=[pl.BlockSpec((tm, LANES), lambda p, i: (i, 0))],
        out_specs=[pl.BlockSpec((1, 1, tm), lambda p, i: (i * p, 0, 0)),
                   pl.BlockSpec((1, 1, tm), lambda p, i: (i * p, 0, 0)),
                   pl.BlockSpec((1, 2 * LANES), lambda p, i: (0, 0)),
                   pl.BlockSpec((1, LANES), lambda p, i: (0, 0))],
        out_shape=[pos_shape, pos_shape, jax.ShapeDtypeStruct((1, 2 * LANES), jnp.int32),
                   jax.ShapeDtypeStruct((1, LANES), jnp.int32)],
        scratch_shapes=[pltpu.VMEM((LANES, 1), F32)] * 3,
        compiler_params=_cparams("arbitrary", "arbitrary"), name="moe_plan",
    )(cw)
    pos_spec = pl.BlockSpec((1, 1, tm), lambda i: (i, 0, 0), memory_space=pltpu.SMEM)
    any_spec = pl.BlockSpec(memory_space=pl.ANY)
    row = lambda w: pl.BlockSpec((tm, w), lambda i: (i, 0))
    xs = pl.pallas_call(
        _moe_dispatch_kernel, grid=(n_tiles,),
        in_specs=[pos_spec, pos_spec, row(d), any_spec],
        out_specs=any_spec, out_shape=jax.ShapeDtypeStruct((n_rows, d), F32),
        scratch_shapes=[pltpu.SemaphoreType.DMA(())],
        input_output_aliases={3: 0},
        compiler_params=_cparams("arbitrary"), name="moe_dispatch",
    )(posa, posb, hn, jnp.zeros((n_rows, d), F32))
    wspec = lambda a, c: pl.BlockSpec((None, 1, a, c), lambda i, te, nu: (layer, te[i], 0, 0))
    ys = pl.pallas_call(
        _moe_experts_kernel,
        grid_spec=pltpu.PrefetchScalarGridSpec(
            num_scalar_prefetch=2, grid=(n_row_tiles,),
            in_specs=[pl.BlockSpec((tr, d), lambda i, te, nu: (i, 0)), wspec(d, d_exp), wspec(d, d_exp),
                      wspec(d_exp, d)],
            out_specs=pl.BlockSpec((tr, d), lambda i, te, nu: (i, 0))),
        out_shape=jax.ShapeDtypeStruct((n_rows, d), F32),
        compiler_params=_cparams("arbitrary"), name="moe_experts",
    )(tile_expert.reshape(-1), n_used.reshape(-1), xs, w_gate, w_up, w_down)
    return pl.pallas_call(
        _moe_combine_kernel, grid=(n_tiles,),
        in_specs=[pos_spec, pos_spec, row(d), row(LANES), any_spec],
        out_specs=row(d), out_shape=jax.ShapeDtypeStruct((n, d), F32),
        scratch_shapes=[pltpu.VMEM((tm, d), F32), pltpu.VMEM((tm, d), F32), pltpu.SemaphoreType.DMA(())],
        compiler_params=_cparams("arbitrary"), name="moe_combine",
    )(posa, posb, x, cw, ys)


def _rwkv_prep_body(p, prev, mu_ref, w0_ref, wup_ref, a0_ref, aup_ref, gup_ref, kk_ref, ka_ref, outs):
    r_ref, ld_ref, k_ref, v_ref, kkn_ref, a_ref, g_ref = outs
    x = p + mu_ref[...] * (prev - p)
    c0, c1, c2, c3, c4 = 768, 1536, 2304, 2368, 2432
    r, k, v = x[:, 0:c0], x[:, c0:c1], x[:, c1:c2]
    xw, xa, xg = x[:, c2:c3], x[:, c3:c4], x[:, c4:SHIFT_W]
    z = -(w0_ref[...] + _dot(jnp.tanh(xw), wup_ref[...]))
    softplus = jnp.maximum(z, 0.0) + jnp.log(1.0 + jnp.exp(-jnp.abs(z)))
    w = -softplus - 0.5
    a = jax.nn.sigmoid(a0_ref[...] + _dot(xa, aup_ref[...]))
    r_ref[...] = r
    ld_ref[...] = -jnp.exp(w)
    k_ref[...] = k * (1.0 + (a - 1.0) * ka_ref[...])
    v_ref[...] = v
    a_ref[...] = a
    g_ref[...] = _dot(jax.nn.sigmoid(xg), gup_ref[...])
    kk = k * kk_ref[...]
    for h in range(TOK_HEADS):
        lanes = slice(h * HEAD_DIM, (h + 1) * HEAD_DIM)
        seg = kk[:, lanes]
        nrm = jnp.sqrt(jnp.sum(seg * seg, axis=-1, keepdims=True))
        kkn_ref[:, lanes] = seg / jnp.maximum(nrm, 1e-12)


def _rwkv_prep_prompt_kernel(p_ref, mu_ref, w0_ref, wup_ref, a0_ref, aup_ref, gup_ref, kk_ref, ka_ref,
                             *rest):
    outs, carry = rest[:7], rest[7]
    tm = p_ref.shape[1]

    @pl.when(pl.program_id(1) == 0)
    def _():
        carry[...] = jnp.zeros_like(carry)

    p = p_ref[0]
    first = lax.broadcasted_iota(jnp.int32, (tm, 1), 0) == 0
    prev = jnp.where(first, carry[...], pltpu.roll(p, 1, 0))
    carry[...] = p[tm - 1:tm]
    _rwkv_prep_body(p, prev, mu_ref, w0_ref, wup_ref, a0_ref, aup_ref, gup_ref, kk_ref, ka_ref,
                    [o.at[0] for o in outs])


def _rwkv_prep_sample_kernel(p_ref, prev_ref, mu_ref, w0_ref, wup_ref, a0_ref, aup_ref, gup_ref, kk_ref,
                             ka_ref, *outs):
    _rwkv_prep_body(p_ref[...], prev_ref[...], mu_ref, w0_ref, wup_ref, a0_ref, aup_ref, gup_ref, kk_ref,
                    ka_ref, outs)


def _rwkv_weight_args(mu, w0, w_up, a0, a_up, g_up, k_k, k_a):
    row = lambda v: v.reshape(1, -1)
    return (row(mu), row(w0), w_up, row(a0), a_up, g_up, row(k_k), row(k_a))


def _rwkv_prep_prompt(pr, wargs, tm):
    b, s, _ = pr.shape
    tm = min(tm, s)
    const = lambda a: pl.BlockSpec(a.shape, lambda i, j: (0, 0))
    ospec = pl.BlockSpec((1, tm, QA), lambda i, j: (i, j, 0))
    return pl.pallas_call(
        _rwkv_prep_prompt_kernel, grid=(b, s // tm),
        in_specs=[pl.BlockSpec((1, tm, SHIFT_W), lambda i, j: (i, j, 0))] + [const(a) for a in wargs],
        out_specs=[ospec] * 7, out_shape=[jax.ShapeDtypeStruct((b, s, QA), F32)] * 7,
        scratch_shapes=[pltpu.VMEM((1, SHIFT_W), F32)],
        compiler_params=_cparams("parallel", "arbitrary"), name="rwkv_prep_prompt",
    )(pr, *wargs)


def _rwkv_prep_sample(pr, prev, wargs):
    n = pr.shape[0]
    full = lambda a: pl.BlockSpec(a.shape, lambda i: (0, 0))
    return pl.pallas_call(
        _rwkv_prep_sample_kernel, grid=(1,),
        in_specs=[full(pr), full(prev)] + [full(a) for a in wargs],
        out_specs=[pl.BlockSpec((n, QA), lambda i: (0, 0))] * 7,
        out_shape=[jax.ShapeDtypeStruct((n, QA), F32)] * 7,
        compiler_params=_cparams("arbitrary"), name="rwkv_prep_sample",
    )(pr, prev, *wargs)


def _rwkv_finish(y, r, k, v, g, rk, lnw, lnb):
    m = jnp.mean(y, axis=-1, keepdims=True)
    var = jnp.mean(jnp.square(y - m), axis=-1, keepdims=True)
    yn = (y - m) * lax.rsqrt(var + RWKV_GN_EPS) * lnw + lnb
    bonus = jnp.sum(r * k * rk, axis=-1, keepdims=True) * v
    return (yn + bonus) * g


def _rwkv_chunk_kernel(r_ref, ld_ref, k_ref, v_ref, kk_ref, a_ref, g_ref, rk_ref, lnw_ref, lnb_ref,
                       o_ref, st_ref, z_ref):
    c = r_ref.shape[1]
    ci = pl.program_id(1)

    @pl.when(ci == 0)
    def _():
        z_ref[...] = jnp.zeros_like(z_ref)

    ri = lax.broadcasted_iota(jnp.int32, (c, c), 0)
    cj = lax.broadcasted_iota(jnp.int32, (c, c), 1)
    incl = ri >= cj
    strict = ri > cj
    n = HEAD_DIM
    eye = (lax.broadcasted_iota(jnp.int32, (n, n), 0) == lax.broadcasted_iota(jnp.int32, (n, n), 1)).astype(F32)
    ex = RWKV_CHUNK_EXACT
    hl = [slice(h * n, (h + 1) * n) for h in range(TOK_HEADS)]
    prep = []
    for bi in range(r_ref.shape[0]):
        ld, r_all, k_all, kk_all = ld_ref[bi], r_ref[bi], k_ref[bi], kk_ref[bi]
        lc = _dot(incl.astype(F32), ld, exact=True)
        l_end = lc[c - 1:c]
        b_all = kk_all * a_ref[bi]
        e_neg = jnp.exp(-lc)
        e_rem = jnp.exp(l_end - lc)
        prep.append(dict(
            r=r_all, k=k_all, v=v_ref[bi],
            at=-kk_all * jnp.exp(lc - ld),
            rt=r_all * jnp.exp(lc),
            bt=b_all * e_neg, kt=k_all * e_neg,
            bh=b_all * e_rem, kh=k_all * e_rem,
            p_end=jnp.exp(l_end)))
    units = [(bi, h) for bi in range(r_ref.shape[0]) for h in range(TOK_HEADS)]
    col = lambda name, u: prep[u[0]][name][:, hl[u[1]]]
    v = [col("v", u) for u in units]
    big = [_dot_nt(jnp.concatenate([col("at", u), col("rt", u)], axis=0),
                   jnp.concatenate([col("bt", u), col("kt", u)], axis=0), ex) for u in units]
    a_rbk = [jnp.concatenate([jnp.where(incl, m[c:2 * c, 0:c], 0.0), jnp.where(incl, m[c:2 * c, c:2 * c], 0.0)],
                             axis=1) for m in big]
    akv = [_dot(jnp.where(strict, m[0:c, c:2 * c], 0.0), vv, ex) for m, vv in zip(big, v)]
    x = [jnp.concatenate([col("at", u), t], axis=1) for u, t in zip(units, akv)]
    npow = [jnp.where(strict, m[0:c, 0:c], 0.0) for m in big]
    x = [xx + _dot(m, xx, ex) for m, xx in zip(npow, x)]
    steps = 1
    while 2 * steps < c:
        npow = [_dot(m, m, ex) for m in npow]
        x = [xx + _dot(m, xx, ex) for m, xx in zip(npow, x)]
        steps *= 2
    zeros = jnp.zeros((c, n), F32)
    wv = [jnp.concatenate([xx, jnp.concatenate([zeros, vv], axis=1)], axis=0) for xx, vv in zip(x, v)]
    rq_y0 = [_dot(m, w, ex) for m, w in zip(a_rbk, wv)]
    m_n = [_dot_tn(jnp.concatenate([col("bh", u), col("kh", u)], axis=0), w, ex)
           for u, w in zip(units, wv)]
    yz = [_dot(jnp.concatenate([col("rt", u) + rq_y0[i][:, 0:n],
                                m_n[i][:, 0:n] + eye * col("p_end", u)], axis=0), z_ref[u[0], u[1]], ex)
          for i, u in enumerate(units)]
    for i, (bi, h) in enumerate(units):
        z_ref[bi, h] = yz[i][c:c + n] + m_n[i][:, n:2 * n]
        y = yz[i][0:c] + rq_y0[i][:, n:2 * n]
        o_ref[bi, :, hl[h]] = _rwkv_finish(y, col("r", (bi, h)), col("k", (bi, h)), v[i], g_ref[bi, :, hl[h]],
                                           rk_ref[:, hl[h]], lnw_ref[:, hl[h]], lnb_ref[:, hl[h]])

    @pl.when(ci == pl.num_programs(1) - 1)
    def _():
        for bi, h in units:
            st_ref[bi, h] = z_ref[bi, h].T


def _rwkv_chunk(r, ld, k, v, kk, a, g, r_k, ln_w, ln_b):
    b, s, _ = r.shape
    c = RWKV_CHUNK
    nb = RWKV_ROWS_PER_STEP if b % RWKV_ROWS_PER_STEP == 0 else 1
    tok = pl.BlockSpec((nb, c, QA), lambda i, j: (i, j, 0))
    const = pl.BlockSpec((1, QA), lambda i, j: (0, 0))
    return pl.pallas_call(
        _rwkv_chunk_kernel, grid=(b // nb, s // c),
        in_specs=[tok] * 7 + [const] * 3,
        out_specs=[tok, pl.BlockSpec((nb, TOK_HEADS, HEAD_DIM, HEAD_DIM), lambda i, j: (i, 0, 0, 0))],
        out_shape=[jax.ShapeDtypeStruct((b, s, QA), F32),
                   jax.ShapeDtypeStruct((b, TOK_HEADS, HEAD_DIM, HEAD_DIM), F32)],
        scratch_shapes=[pltpu.VMEM((nb, TOK_HEADS, HEAD_DIM, HEAD_DIM), F32)],
        compiler_params=_cparams("parallel", "arbitrary"), name="rwkv_chunk",
    )(r, ld, k, v, kk, a, g, r_k.reshape(1, QA), ln_w.reshape(1, QA), ln_b.reshape(1, QA))


def _rwkv_step_kernel(r_ref, ld_ref, k_ref, v_ref, kk_ref, a_ref, g_ref, rk_ref, lnw_ref, lnb_ref, s_ref,
                      o_ref, so_ref):
    n = HEAD_DIM
    eye = (lax.broadcasted_iota(jnp.int32, (n, n), 0) == lax.broadcasted_iota(jnp.int32, (n, n), 1)).astype(F32)
    for h in range(TOK_HEADS):
        lanes = slice(h * n, (h + 1) * n)
        r, k, v, kk, a = r_ref[0, :, lanes], k_ref[0, :, lanes], v_ref[0, :, lanes], kk_ref[0, :, lanes], \
            a_ref[0, :, lanes]
        d = jnp.exp(ld_ref[0, :, lanes])
        s = s_ref[0, h]
        sa = jnp.sum(_bf16_round(s) * _bf16_round(-kk), axis=1, keepdims=True)
        v_col = jnp.sum(eye * v, axis=1, keepdims=True)
        s_new = s * d + sa * (kk * a) + v_col * k
        so_ref[0, h] = s_new
        y_col = jnp.sum(_bf16_round(s_new) * _bf16_round(r), axis=1, keepdims=True)
        y = jnp.sum(eye * y_col, axis=0, keepdims=True)
        o_ref[0, :, lanes] = _rwkv_finish(y, r, k, v, g_ref[0, :, lanes], rk_ref[:, lanes], lnw_ref[:, lanes],
                                          lnb_ref[:, lanes])


def _rwkv_step(r, ld, k, v, kk, a, g, r_k, ln_w, ln_b, state):
    n = r.shape[0]
    tok = pl.BlockSpec((1, 1, QA), lambda i: (i, 0, 0))
    const = pl.BlockSpec((1, QA), lambda i: (0, 0))
    st = pl.BlockSpec((1, TOK_HEADS, HEAD_DIM, HEAD_DIM), lambda i: (i, 0, 0, 0))
    r3 = lambda z: z.reshape(n, 1, QA)
    return pl.pallas_call(
        _rwkv_step_kernel, grid=(n,),
        in_specs=[tok] * 7 + [const] * 3 + [st],
        out_specs=[tok, st],
        out_shape=[jax.ShapeDtypeStruct((n, 1, QA), F32), jax.ShapeDtypeStruct(state.shape, F32)],
        compiler_params=_cparams("parallel"), name="rwkv_step",
    )(r3(r), r3(ld), r3(k), r3(v), r3(kk), r3(a), r3(g), r_k.reshape(1, QA), ln_w.reshape(1, QA),
      ln_b.reshape(1, QA), state)


def _split_w_in_a(w):
    offs = [0, QA]
    for _ in range(6):
        offs.append(offs[-1] + KVW)
    offs.append(offs[-1] + 3 * TOK_HEADS)
    offs.append(offs[-1] + MEMQ)
    pieces = [w[:, offs[i]:offs[i + 1]] for i in range(9)]
    pieces[7] = jnp.pad(pieces[7], ((0, 0), (0, LANES - 3 * TOK_HEADS)))
    return pieces


_A_OPS = ("hnorm", "none", "none", "hnorm", "none", "hnorm", "none", "sigmoid", "none")


def _ffn(x, mix, mem, w_out, g_ffn, wc, bc, wf, bf, w_gate, w_up, w_down, layer, tm_proj, tm_moe):
    x_new, hn, cw = _outproj_router(x, mix, mem, w_out, g_ffn, wc, bc, wf, bf, tm_proj)
    if x.shape[0] % MOE_TOKEN_TILE == 0:
        return _moe_routed(x_new, hn, cw, w_gate, w_up, w_down, layer)
    return _moe(x_new, hn, cw, w_gate, w_up, w_down, layer, tm_moe)


def kernel(x_prompt, x_sample, cache_cmp_k, cache_cmp_v, cache_sel_k, cache_sel_v, cache_win_k, cache_win_v, cache_mem_k, cache_mem_v, state_rwkv, state_shift, page_table, mem_prompt, norm_mix, norm_ffn, norm_mem, w_mem_kv, mem_q_gain, mem_k_gain, w_in_a, nsa_q_gain, nsa_k_gain, cmp_pe, cmp_w1, cmp_b1, cmp_w2, w_in_b, rwkv_mu, rwkv_w0, rwkv_w_up, rwkv_a0, rwkv_a_up, rwkv_g_up, rwkv_k_k, rwkv_k_a, rwkv_r_k, rwkv_ln_w, rwkv_ln_b, w_out, moe_w_coarse, moe_b_coarse, moe_w_fine, moe_b_fine, moe_w_gate, moe_w_up, moe_w_down):
    b, s, d = x_prompt.shape
    bd = x_sample.shape[0]
    depth = norm_mix.shape[0]
    m_len = mem_prompt.shape[1]
    wl = min(WINDOW, s)
    xp = x_prompt.reshape(b * s, d)
    xs = x_sample.reshape(bd, d)
    mem2 = mem_prompt.reshape(b * m_len, d)
    outs = {name: [] for name in ("pc_k", "pc_v", "ps_k", "ps_v", "pw_k", "pw_v", "pm_k", "pm_v", "pr_s", "pr_x",
                                  "sc_k", "sc_v", "ss_k", "ss_v", "sw_k", "sw_v", "sr_s", "sr_x")}
    for i in range(depth):
        km_p, vm_p = _norm_proj(mem2, norm_mem[i], [w_mem_kv[i][:, :MEMQ], w_mem_kv[i][:, MEMQ:]],
                                ("hnorm", "none"), [mem_k_gain[i]], 256)
        km_p, vm_p = km_p.reshape(b, m_len, MEMQ), vm_p.reshape(b, m_len, MEMQ)
        outs["pm_k"].append(km_p.reshape(b, m_len, MEM_HEADS, HEAD_DIM))
        outs["pm_v"].append(vm_p.reshape(b, m_len, MEM_HEADS, HEAD_DIM))
        if i % 2 == 0:
            ia = i // 2
            pieces = _split_w_in_a(w_in_a[ia])
            gains = [nsa_q_gain[ia], nsa_k_gain[ia, 1], nsa_k_gain[ia, 2]]
            cmp_args = (cmp_pe[ia], cmp_w1[ia], cmp_b1[ia], cmp_w2[ia], nsa_k_gain[ia, 0])
            q, kc, vc, ks, vs, kw, vw, gt, mq_p = _norm_proj(xp, norm_mix[i], pieces, _A_OPS, gains, 256)
            r3 = lambda z: z.reshape(b, s, -1)
            kc, vc, ks, vs, kw, vw = (r3(z) for z in (kc, vc, ks, vs, kw, vw))
            ck, cv = _compress_prompt(kc, vc, *cmp_args)
            mix_p = _nsa_prompt(r3(q), r3(gt), ck, cv, ks, vs, kw, vw).reshape(b * s, QA)
            r5 = lambda z: z.reshape(b, -1, KV_GROUPS, HEAD_DIM)
            for name, z in (("pc_k", kc), ("pc_v", vc), ("ps_k", ks), ("ps_v", vs),
                            ("pw_k", kw[:, s - wl:]), ("pw_v", vw[:, s - wl:])):
                outs[name].append(r5(z))
            q, kc, vc, ks, vs, kw, vw, gt, mq_s = _norm_proj(xs, norm_mix[i], pieces, _A_OPS, gains, 256)
            ck, cv = _compress_sample(page_table, cache_cmp_k[ia], cache_cmp_v[ia], *cmp_args)
            mix_s, nwk, nwv = _nsa_sample(page_table, q, gt, ck, cv, ks, vs, kw, vw,
                                          cache_sel_k[ia], cache_sel_v[ia], cache_win_k[ia], cache_win_v[ia])
            mix_s = mix_s.reshape(bd, QA)
            r5 = lambda z: z.reshape(bd, -1, KV_GROUPS, HEAD_DIM)
            for name, z in (("sc_k", kc), ("sc_v", vc), ("ss_k", ks), ("ss_v", vs), ("sw_k", nwk), ("sw_v", nwv)):
                outs[name].append(r5(z))
        else:
            ib = i // 2
            pieces = [w_in_b[ib][:, :SHIFT_W], w_in_b[ib][:, SHIFT_W:]]
            wargs = _rwkv_weight_args(rwkv_mu[ib], rwkv_w0[ib], rwkv_w_up[ib], rwkv_a0[ib], rwkv_a_up[ib],
                                      rwkv_g_up[ib], rwkv_k_k[ib], rwkv_k_a[ib])
            fin = (rwkv_r_k[ib].reshape(-1), rwkv_ln_w[ib], rwkv_ln_b[ib])
            pr, mq_p = _norm_proj(xp, norm_mix[i], pieces, ("none", "none"), [], 256)
            pr = pr.reshape(b, s, SHIFT_W)
            prep = _rwkv_prep_prompt(pr, wargs, 256)
            mix_p, st_p = _rwkv_chunk(*prep, *fin)
            mix_p = mix_p.reshape(b * s, QA)
            outs["pr_s"].append(st_p)
            outs["pr_x"].append(pr[:, s - 1])
            pr, mq_s = _norm_proj(xs, norm_mix[i], pieces, ("none", "none"), [], 256)
            prep = _rwkv_prep_sample(pr, state_shift[ib], wargs)
            mix_s, st_s = _rwkv_step(*prep, *fin, state_rwkv[ib])
            mix_s = mix_s.reshape(bd, QA)
            outs["sr_s"].append(st_s)
            outs["sr_x"].append(pr)
        mem_p = _mem_attn(mq_p.reshape(b, s, MEMQ), km_p, vm_p, mem_q_gain[i], 256).reshape(b * s, MEMQ)
        mem_s = _mem_attn(mq_s.reshape(bd, 1, MEMQ), cache_mem_k[i].reshape(bd, -1, MEMQ),
                          cache_mem_v[i].reshape(bd, -1, MEMQ), mem_q_gain[i], 1).reshape(bd, MEMQ)
        ffn_w = (w_out[i], norm_ffn[i], moe_w_coarse[i], moe_b_coarse[i], moe_w_fine[i], moe_b_fine[i],
                 moe_w_gate, moe_w_up, moe_w_down, i)
        xp = _ffn(xp, mix_p, mem_p, *ffn_w, 256, 1024)
        xs = _ffn(xs, mix_s, mem_s, *ffn_w, 256, 1024)
    order = ("pc_k", "pc_v", "ps_k", "ps_v", "pw_k", "pw_v", "pm_k", "pm_v", "pr_s", "pr_x",
             "sc_k", "sc_v", "ss_k", "ss_v", "sw_k", "sw_v", "sr_s", "sr_x")
    return (xp.reshape(b, s, d), xs.reshape(bd, 1, d)) + tuple(jnp.stack(outs[name]) for name in order)
```

```python
import functools

import jax
import jax.numpy as jnp
from jax import lax
from jax.experimental import pallas as pl
from jax.experimental.pallas import tpu as pltpu

F32 = jnp.float32
BF16 = jnp.bfloat16
HI = lax.Precision.HIGHEST

HEAD_DIM = 64
TOK_HEADS = 12
MEM_HEADS = 4
KV_GROUPS = 3
HEADS_PER_GROUP = TOK_HEADS // KV_GROUPS
QA = TOK_HEADS * HEAD_DIM
KVW = KV_GROUPS * HEAD_DIM
MEMQ = MEM_HEADS * HEAD_DIM
CMP_LEN = 32
CMP_STRIDE = 16
CMP_HID = 128
SEL_LEN = 64
SEL_TOPK = 16
WINDOW = 512
Q_BLOCK = 128
PROJ_ROWS = 1024
NSA_KEY_STEP = 512
PAGE = 128
N_GROUPS = 4
E_PER_GROUP = 8
N_EXPERTS = 32
MOE_TOKEN_TILE = 512
MOE_ROW_TILE = 512
RWKV_COLS = (768, 768, 768, 64, 64, 128)
SHIFT_W = sum(RWKV_COLS)
RWKV_GN_EPS = 64e-5
RWKV_CHUNK = 64
RWKV_STEP_ROWS = 4
RWKV_ROWS_PER_STEP = 2
RWKV_CHUNK_EXACT = False
NEG = -1e30
BIG = 1e30
LANES = 128
VMEM_LIMIT = 56 * 1024 * 1024


def _cparams(*sem):
    return pltpu.CompilerParams(dimension_semantics=sem, vmem_limit_bytes=VMEM_LIMIT)


def _dot_general(a, b, dims, exact):
    if exact:
        return lax.dot_general(a, b, (dims, ((), ())), precision=HI, preferred_element_type=F32)
    return lax.dot_general(a.astype(BF16), b.astype(BF16), (dims, ((), ())), preferred_element_type=F32)


def _dot(a, b, exact=False):
    return _dot_general(a, b, ((1,), (0,)), exact)


def _dot_nt(a, b, exact=False):
    return _dot_general(a, b, ((1,), (1,)), exact)


def _dot_tn(a, b, exact=False):
    return _dot_general(a, b, ((0,), (0,)), exact)


def _bf16_round(x):
    return x.astype(BF16).astype(F32)


def _rms(x, g, eps=1e-6):
    return x * lax.rsqrt(jnp.mean(x * x, axis=-1, keepdims=True) + eps) * g


def _masked_softmax(s, mask):
    s = jnp.where(mask, s, NEG)
    m = jnp.max(s, axis=-1, keepdims=True)
    e = jnp.where(mask, jnp.exp(s - m), 0.0)
    return e / jnp.maximum(jnp.sum(e, axis=-1, keepdims=True), 1e-30)


def _alibi_slope(h):
    return 2.0 ** (-8.0 * (h + 1) / TOK_HEADS)


def _topk_mask(score, k):
    lane = lax.broadcasted_iota(jnp.int32, score.shape, 1).astype(F32)
    sel = jnp.zeros(score.shape, F32)
    for _ in range(k):
        m = jnp.max(score, axis=-1, keepdims=True)
        idx = jnp.min(jnp.where(score == m, lane, 1e9), axis=-1, keepdims=True)
        hit = lane == idx
        sel = jnp.where(hit, 1.0, sel)
        score = jnp.where(hit, -jnp.inf, score)
    return sel


def _topk_rows_t(score_t, k):
    n_blk = score_t.shape[0]
    idx = lax.broadcasted_iota(jnp.int32, score_t.shape, 0)
    ahead = jnp.zeros(score_t.shape, F32)
    for j in range(n_blk):
        row = score_t[j:j + 1]
        ahead = ahead + jnp.where((row > score_t) | ((row == score_t) & (idx > j)), 1.0, 0.0)
    return jnp.where(ahead < k, 1.0, 0.0)


def _sel_to_cmp(n_sel_rows, n_cmp_lanes):
    s0 = lax.broadcasted_iota(jnp.int32, (n_sel_rows, n_cmp_lanes), 0) * SEL_LEN
    c0 = lax.broadcasted_iota(jnp.int32, (n_sel_rows, n_cmp_lanes), 1) * CMP_STRIDE
    return ((c0 < s0 + SEL_LEN) & (c0 + CMP_LEN > s0)).astype(F32)


def _cmp_to_sel(n_cmp_rows, n_sel_lanes):
    n_i = lax.broadcasted_iota(jnp.int32, (n_cmp_rows, n_sel_lanes), 0)
    s_i = lax.broadcasted_iota(jnp.int32, (n_cmp_rows, n_sel_lanes), 1)
    c0 = n_i * CMP_STRIDE
    s0 = s_i * SEL_LEN
    return ((c0 < s0 + SEL_LEN) & (c0 + CMP_LEN > s0)).astype(F32)


def _block_expand(n_blk_rows, n_keys):
    b_i = lax.broadcasted_iota(jnp.int32, (n_blk_rows, n_keys), 0)
    k_i = lax.broadcasted_iota(jnp.int32, (n_blk_rows, n_keys), 1)
    return (jnp.right_shift(k_i, 6) == b_i).astype(BF16)


def _norm_proj_kernel(ops, x_ref, g_ref, *refs):
    n = len(ops)
    n_gain = sum(op == "hnorm" for op in ops)
    w_refs, gain_refs, o_refs = refs[:n], refs[n:n + n_gain], refs[n + n_gain:]
    h = _rms(x_ref[...], g_ref[...]).astype(BF16)
    gi = 0
    for op, w_ref, o_ref in zip(ops, w_refs, o_refs):
        z = _dot(h, w_ref[...])
        if op == "hnorm":
            gain = gain_refs[gi][...]
            gi += 1
            for hh in range(z.shape[1] // HEAD_DIM):
                seg = z[:, hh * HEAD_DIM:(hh + 1) * HEAD_DIM]
                o_ref[:, hh * HEAD_DIM:(hh + 1) * HEAD_DIM] = _rms(seg, gain)
        elif op == "sigmoid":
            o_ref[...] = jax.nn.sigmoid(z)
        else:
            o_ref[...] = z


def _norm_proj(x, g, weights, ops, gains, tm):
    n_rows, d = x.shape
    tm = min(tm, n_rows)
    assert n_rows % tm == 0
    in_specs = [pl.BlockSpec((tm, d), lambda i: (i, 0)), pl.BlockSpec((1, d), lambda i: (0, 0))]
    in_specs += [pl.BlockSpec(w.shape, lambda i: (0, 0)) for w in weights]
    in_specs += [pl.BlockSpec((1, HEAD_DIM), lambda i: (0, 0)) for _ in gains]
    out_shape = [jax.ShapeDtypeStruct((n_rows, w.shape[1]), F32) for w in weights]
    out_specs = [pl.BlockSpec((tm, w.shape[1]), lambda i: (i, 0)) for w in weights]
    return pl.pallas_call(
        functools.partial(_norm_proj_kernel, tuple(ops)),
        grid=(n_rows // tm,), in_specs=in_specs, out_specs=out_specs, out_shape=out_shape,
        compiler_params=_cparams("parallel"), name="norm_proj",
    )(x, g.reshape(1, d), *[w.astype(BF16) for w in weights], *[gn.reshape(1, HEAD_DIM) for gn in gains])


def _compress_rows(rows_ref, pe_ref, w1_ref, b1_ref, w2_ref, gain, o_ref, u_ref, n_chunk):
    for g in range(KV_GROUPS):
        for l in range(CMP_STRIDE):
            src = l * KVW + g * HEAD_DIM
            u_ref[:, l * HEAD_DIM:(l + 1) * HEAD_DIM] = rows_ref[:, src:src + HEAD_DIM]
        _compress_mlp(u_ref, pe_ref, w1_ref, b1_ref, w2_ref, gain, o_ref, g, n_chunk)


def _compress_mlp(u_ref, pe_ref, w1_ref, b1_ref, w2_ref, gain, o_ref, g, n_chunk):
    half = CMP_STRIDE * HEAD_DIM
    u = u_ref[...]
    p0 = _dot(u + pe_ref[:, 0:half], w1_ref[0:half, :])
    p1 = _dot(u + pe_ref[:, half:2 * half], w1_ref[half:2 * half, :])
    hid = b1_ref[...] + p0 + pltpu.roll(p1, n_chunk - 1, 0)
    z = _dot(jax.nn.gelu(hid), w2_ref[...])
    if gain is not None:
        z = _rms(z, gain)
    o_ref[:, g * HEAD_DIM:(g + 1) * HEAD_DIM] = z


def _compress_cols(xt_ref, pe_ref, w1_ref, b1_ref, w2_ref, gain, o_ref, xs_ref, u_ref, n_chunk):
    half = CMP_STRIDE * 2 * HEAD_DIM
    lane = lax.broadcasted_iota(jnp.int32, (1, 2 * HEAD_DIM), 1)
    for first in (0, HEAD_DIM):
        xs_ref[...] = xt_ref[first:first + 2 * HEAD_DIM, :].T
        for l in range(CMP_STRIDE):
            u_ref[:, l * 2 * HEAD_DIM:(l + 1) * 2 * HEAD_DIM] = xs_ref[pl.ds(l, n_chunk, stride=CMP_STRIDE), :]
        u = u_ref[...]
        p0 = _dot(u + pe_ref[:, 0:half], w1_ref[0:half, :])
        p1 = _dot(u + pe_ref[:, half:2 * half], w1_ref[half:2 * half, :])
        hid = b1_ref[...] + p0 + pltpu.roll(p1, n_chunk - 1, 0)
        z = _dot(jax.nn.gelu(hid), w2_ref[...])
        if gain is not None:
            sq = z * z
            ss_lo = jnp.sum(jnp.where(lane < HEAD_DIM, sq, 0.0), axis=-1, keepdims=True)
            ss_hi = jnp.sum(jnp.where(lane < HEAD_DIM, 0.0, sq), axis=-1, keepdims=True)
            ms = jnp.where(lane < HEAD_DIM, ss_lo, ss_hi) * (1.0 / HEAD_DIM)
            z = z * lax.rsqrt(ms + 1e-6) * gain
        if first == 0:
            o_ref[:, 0:2 * HEAD_DIM] = z
        else:
            o_ref[:, 2 * HEAD_DIM:3 * HEAD_DIM] = z[:, HEAD_DIM:2 * HEAD_DIM]


def _compress_prompt_kernel(k_ref, v_ref, pe_ref, w1_ref, b1_ref, w2_ref, gain_ref, ck_ref, cv_ref, u_ref,
                            *, n_chunk):
    _compress_rows(k_ref.at[0], pe_ref.at[0], w1_ref.at[0], b1_ref.at[0], w2_ref.at[0], gain_ref[...],
                   ck_ref.at[0], u_ref, n_chunk)
    _compress_rows(v_ref.at[0], pe_ref.at[1], w1_ref.at[1], b1_ref.at[1], w2_ref.at[1], None,
                   cv_ref.at[0], u_ref, n_chunk)


def _compress_weight_specs(nidx):
    zero = lambda *_: (0, 0, 0)
    del nidx
    return [pl.BlockSpec((2, 1, CMP_LEN * HEAD_DIM), zero),
            pl.BlockSpec((2, CMP_LEN * HEAD_DIM, CMP_HID), zero),
            pl.BlockSpec((2, 1, CMP_HID), zero),
            pl.BlockSpec((2, CMP_HID, HEAD_DIM), zero)]


def _compress_prompt(kc, vc, pe, w1, b1, w2, gain):
    b, s, _ = kc.shape
    n_chunk = s // CMP_STRIDE
    cw = CMP_STRIDE * KVW
    rows = pl.BlockSpec((1, n_chunk, cw), lambda i: (i, 0, 0))
    out = pl.BlockSpec((1, n_chunk, KVW), lambda i: (i, 0, 0))
    return pl.pallas_call(
        functools.partial(_compress_prompt_kernel, n_chunk=n_chunk),
        grid=(b,),
        in_specs=[rows, rows] + _compress_weight_specs(1) + [pl.BlockSpec((1, HEAD_DIM), lambda i: (0, 0))],
        out_specs=[out, out],
        out_shape=[jax.ShapeDtypeStruct((b, n_chunk, KVW), F32)] * 2,
        scratch_shapes=[pltpu.VMEM((n_chunk, CMP_STRIDE * HEAD_DIM), F32)],
        compiler_params=_cparams("parallel"), name="compress_prompt",
    )(kc.reshape(b, n_chunk, cw), vc.reshape(b, n_chunk, cw), pe.reshape(2, 1, -1), w1, b1.reshape(2, 1, -1), w2,
      gain.reshape(1, HEAD_DIM))


def _compress_sample_kernel(pt_ref, pk_ref, pv_ref, pe_ref, w1_ref, b1_ref, w2_ref, gain_ref,
                            ck_ref, cv_ref, kbuf, vbuf, xs_ref, u_ref, sem, *, n_pages):
    b = pl.program_id(0)
    n_chunk = n_pages * PAGE // CMP_STRIDE

    def gather(pool_ref, dst_ref, dma_sem):
        copies = []
        for j in range(n_pages):
            cp = pltpu.make_async_copy(pool_ref.at[pt_ref[b, j]], dst_ref.at[:, pl.ds(j * PAGE, PAGE)], dma_sem)
            cp.start()
            copies.append(cp)
        return copies

    ck_copies = gather(pk_ref, kbuf, sem.at[0])
    cv_copies = gather(pv_ref, vbuf, sem.at[1])
    for cp in ck_copies:
        cp.wait()
    _compress_cols(kbuf, pe_ref.at[0], w1_ref.at[0], b1_ref.at[0], w2_ref.at[0], gain_ref[...],
                   ck_ref.at[0], xs_ref, u_ref, n_chunk)
    for cp in cv_copies:
        cp.wait()
    _compress_cols(vbuf, pe_ref.at[1], w1_ref.at[1], b1_ref.at[1], w2_ref.at[1], None,
                   cv_ref.at[0], xs_ref, u_ref, n_chunk)


def _compress_sample(page_table, pool_k, pool_v, pe, w1, b1, w2, gain):
    bd, n_pages = page_table.shape
    n_pool = pool_k.shape[0]
    n_chunk = n_pages * PAGE // CMP_STRIDE
    past = n_pages * PAGE
    pos_minor = lambda z: jnp.transpose(z, (0, 2, 3, 1)).reshape(n_pool, KVW, PAGE)
    eye2 = jnp.eye(2, dtype=F32)
    n_l, dh, hid = CMP_STRIDE, HEAD_DIM, CMP_HID
    pe2 = jnp.broadcast_to(pe.reshape(2, 2, n_l, 1, dh), (2, 2, n_l, 2, dh)).reshape(2, 1, 4 * n_l * dh)
    w1x = jnp.einsum("krldf,gh->krlgdhf", w1.reshape(2, 2, n_l, dh, hid), eye2).reshape(2, 4 * n_l * dh, 2 * hid)
    b1x = jnp.tile(b1, (1, 2)).reshape(2, 1, 2 * hid)
    w2x = jnp.einsum("kfd,gh->kgfhd", w2, eye2).reshape(2, 2 * hid, 2 * dh)
    zero3 = lambda i, pt: (0, 0, 0)
    any_spec = pl.BlockSpec(memory_space=pl.ANY)
    out = pl.BlockSpec((1, n_chunk, KVW), lambda i, pt: (i, 0, 0))
    grid_spec = pltpu.PrefetchScalarGridSpec(
        num_scalar_prefetch=1, grid=(bd,),
        in_specs=[any_spec, any_spec,
                  pl.BlockSpec(pe2.shape, zero3), pl.BlockSpec(w1x.shape, zero3),
                  pl.BlockSpec(b1x.shape, zero3), pl.BlockSpec(w2x.shape, zero3),
                  pl.BlockSpec((1, 2 * dh), lambda i, pt: (0, 0))],
        out_specs=[out, out],
        scratch_shapes=[pltpu.VMEM((KVW, past), F32), pltpu.VMEM((KVW, past), F32),
                        pltpu.VMEM((past, LANES), F32),
                        pltpu.VMEM((n_chunk, 2 * n_l * dh), F32), pltpu.SemaphoreType.DMA((2,))])
    return pl.pallas_call(
        functools.partial(_compress_sample_kernel, n_pages=n_pages),
        grid_spec=grid_spec,
        out_shape=[jax.ShapeDtypeStruct((bd, n_chunk, KVW), F32)] * 2,
        compiler_params=_cparams("arbitrary"), name="compress_sample",
    )(page_table, pos_minor(pool_k), pos_minor(pool_v),
      pe2, w1x.astype(BF16), b1x, w2x.astype(BF16), jnp.tile(gain, 2).reshape(1, 2 * dh))


def _softmax_rows(s):
    e = jnp.exp(s - jnp.max(s, axis=-1, keepdims=True))
    return e / jnp.sum(e, axis=-1, keepdims=True)


def _nsa_prompt_block(q_ref, gt_ref, ck_ref, cv_ref, ks_ref, vs_ref, kw_ref, vw_ref, o_ref,
                      *, seq, n_keys, first_block):
    tq = Q_BLOCK
    hpg = HEADS_PER_GROUP
    n_cmp = ck_ref.shape[1]
    n_sel = seq // SEL_LEN
    wlen = min(WINDOW + tq, seq)
    q0 = (pl.program_id(1) + first_block) * tq
    t1 = q0 + lax.broadcasted_iota(jnp.int32, (tq, 1), 0)
    s2c = _sel_to_cmp(n_sel, n_cmp)
    expand = _block_expand(n_sel, n_keys)
    blk = lax.broadcasted_iota(jnp.int32, (n_sel, 1), 0)
    t_row = q0 + lax.broadcasted_iota(jnp.int32, (1, tq), 1)
    cur = jnp.right_shift(t_row, 6)
    forced = (blk == 0) | (blk == cur) | (blk == cur - 1)
    valid = blk * SEL_LEN <= t_row
    dist_c = t1 - (lax.broadcasted_iota(jnp.int32, (1, n_cmp), 1) * CMP_STRIDE + (CMP_LEN - 1))
    mask_c = dist_c >= 0
    dist_cf = dist_c.astype(F32)
    dist_s = t1 - lax.broadcasted_iota(jnp.int32, (1, n_keys), 1)
    causal_s = dist_s >= 0
    dist_sf = dist_s.astype(F32)
    w_start = pl.multiple_of(jnp.clip(q0 - WINDOW, 0, seq - wlen), tq)
    dist_w = t1 - (w_start + lax.broadcasted_iota(jnp.int32, (1, wlen), 1))
    bias_w = jnp.where((dist_w >= 0) & (dist_w <= WINDOW), 0.0, NEG)
    dist_wf = dist_w.astype(F32)
    gt = gt_ref[0]
    for g in range(KV_GROUPS):
        lanes = slice(g * HEAD_DIM, (g + 1) * HEAD_DIM)
        heads = [g * hpg + j for j in range(hpg)]
        rows = [slice(j * tq, (j + 1) * tq) for j in range(hpg)]
        q4 = jnp.concatenate([q_ref[0, :, h * HEAD_DIM:(h + 1) * HEAD_DIM] for h in heads],
                             axis=0) * (HEAD_DIM ** -0.5)
        s_c = _dot_nt(q4, ck_ref[0, :, lanes])
        p_c = [_masked_softmax(s_c[rows[j]] - _alibi_slope(heads[j]) * dist_cf, mask_c) for j in range(hpg)]
        cv = cv_ref[0, :, lanes]
        o_c = [_dot(p, cv) for p in p_c]
        imps = [_dot_nt(s2c, p) for p in p_c]
        imp = imps[0]
        for j in range(1, hpg):
            imp = imp + imps[j]
        score = jnp.where(valid, jnp.where(forced, BIG, imp), -BIG)
        key_sel = _dot_tn(_topk_rows_t(score, min(SEL_TOPK, n_sel)), expand)
        bias_s = jnp.where((key_sel > 0.5) & causal_s, 0.0, NEG)
        s_s = _dot_nt(q4, ks_ref[0, 0:n_keys, lanes])
        vs = vs_ref[0, 0:n_keys, lanes]
        o_s = [_dot(_softmax_rows(s_s[rows[j]] - _alibi_slope(heads[j]) * dist_sf + bias_s), vs)
               for j in range(hpg)]
        s_w = _dot_nt(q4, kw_ref[0, pl.ds(w_start, wlen), lanes])
        vw = vw_ref[0, pl.ds(w_start, wlen), lanes]
        o_w = [_dot(_softmax_rows(s_w[rows[j]] - _alibi_slope(heads[j]) * dist_wf + bias_w), vw)
               for j in range(hpg)]
        for j, h in enumerate(heads):
            o_ref[0, :, h * HEAD_DIM:(h + 1) * HEAD_DIM] = (
                gt[:, 3 * h:3 * h + 1] * o_c[j] + gt[:, 3 * h + 1:3 * h + 2] * o_s[j]
                + gt[:, 3 * h + 2:3 * h + 3] * o_w[j])


def _nsa_prompt(q, gates, ck, cv, ks, vs, kw, vw):
    b, s, _ = q.shape
    n_cmp = ck.shape[1]
    key_step = min(NSA_KEY_STEP, s)
    per_class = key_step // Q_BLOCK
    full = lambda n: pl.BlockSpec((1, n, KVW), lambda i, j: (i, 0, 0))
    outs = []
    for c in range(s // key_step):
        n_keys = (c + 1) * key_step
        first = c * per_class
        qspec = lambda w, first=first: pl.BlockSpec((1, Q_BLOCK, w), lambda i, j: (i, j + first, 0))
        outs.append(pl.pallas_call(
            functools.partial(_nsa_prompt_block, seq=s, n_keys=n_keys, first_block=first),
            grid=(b, per_class),
            in_specs=[qspec(QA), qspec(LANES), full(n_cmp), full(n_cmp), full(n_keys), full(n_keys), full(s), full(s)],
            out_specs=pl.BlockSpec((1, Q_BLOCK, QA), lambda i, j: (i, j, 0)),
            out_shape=jax.ShapeDtypeStruct((b, key_step, QA), F32),
            compiler_params=_cparams("parallel", "parallel"), name=f"nsa_prompt_{n_keys}",
        )(q, gates, ck, cv, ks, vs, kw, vw))
    return jnp.concatenate(outs, axis=1)


def _gather_pages_t(pt_ref, b, pool_ref, dst_ref, sem, n_pages):
    copies = []
    for j in range(n_pages):
        cp = pltpu.make_async_copy(pool_ref.at[pt_ref[b, j]], dst_ref.at[:, :, pl.ds(j * PAGE, PAGE)], sem)
        cp.start()
        copies.append(cp)
    return copies


def _nsa_sample_kernel(pt_ref, q_ref, gt_ref, ck_ref, cv_ref, ksn_ref, vsn_ref, kwn_ref, vwn_ref,
                       pks_ref, pvs_ref, wk_ref, wv_ref, o_ref, owk_ref, owv_ref, kbuf, vbuf, sem,
                       *, n_pages):
    b = pl.program_id(0)
    hpg = HEADS_PER_GROUP
    past = n_pages * PAGE
    n_cmp = ck_ref.shape[1]
    n_sel_past = past // SEL_LEN
    n_sel = n_sel_past + 1
    wb = wk_ref.shape[3]
    sel_lanes = 2 * LANES
    k_copies = _gather_pages_t(pt_ref, b, pks_ref, kbuf, sem.at[0], n_pages)
    v_copies = _gather_pages_t(pt_ref, b, pvs_ref, vbuf, sem.at[1], n_pages)

    row = lax.broadcasted_iota(jnp.int32, (8, 1), 0)
    n = HEAD_DIM
    eye = (lax.broadcasted_iota(jnp.int32, (n, n), 0) == lax.broadcasted_iota(jnp.int32, (n, n), 1)).astype(F32)
    gt = gt_ref[0]
    c_end = lax.broadcasted_iota(jnp.int32, (1, n_cmp), 1) * CMP_STRIDE + (CMP_LEN - 1)
    dist_c = past - c_end
    blk = lax.broadcasted_iota(jnp.int32, (1, sel_lanes), 1)
    cur = past // SEL_LEN
    forced = (blk == 0) | (blk == cur) | (blk == cur - 1)
    valid = blk * SEL_LEN <= past
    c2s = _cmp_to_sel(n_cmp, sel_lanes)

    def q_rows(g):
        q4 = jnp.concatenate(
            [q_ref[0, :, (g * hpg + j) * HEAD_DIM:(g * hpg + j + 1) * HEAD_DIM] for j in range(hpg)]
            + [jnp.zeros((8 - hpg, HEAD_DIM), F32)], axis=0) * (HEAD_DIM ** -0.5)
        slope = jnp.full((8, 1), _alibi_slope(g * hpg + hpg - 1), F32)
        for j in range(hpg - 2, -1, -1):
            slope = jnp.where(row < j + 1, _alibi_slope(g * hpg + j), slope)
        return q4, slope

    o_cs, imps = [], []
    for g in range(KV_GROUPS):
        lanes = slice(g * HEAD_DIM, (g + 1) * HEAD_DIM)
        q4, slope = q_rows(g)
        p_c = _masked_softmax(_dot_nt(q4, ck_ref[0, :, lanes]) - slope * dist_c.astype(F32), dist_c >= 0)
        o_cs.append(_dot(p_c, cv_ref[0, :, lanes]))
        imps.append(jnp.sum(jnp.where(row < hpg, _dot(p_c, c2s), 0.0), axis=0, keepdims=True))
    imp = jnp.concatenate(imps + [jnp.zeros((8 - KV_GROUPS, sel_lanes), F32)], axis=0)
    score = jnp.where(valid, jnp.where(forced, BIG, imp), -BIG)
    score = jnp.where(blk < n_sel, score, -jnp.inf)
    sel = _topk_mask(score, min(SEL_TOPK, n_sel))
    key_sel = _dot(sel[:, 0:n_sel_past], _block_expand(n_sel_past, past))
    sel_new = sel[:, n_sel_past:n_sel_past + 1]

    pos_s = lax.broadcasted_iota(jnp.int32, (1, past), 1)
    dist_s = (past - pos_s).astype(F32)
    lane_w = lax.broadcasted_iota(jnp.int32, (1, wb), 1)
    pos_w = past - wb + lane_w
    dist_w = past - pos_w
    mask_w = (dist_w >= 0) & (dist_w <= WINDOW) & (pos_w >= 0)
    for cp in k_copies:
        cp.wait()
    for cp in v_copies:
        cp.wait()
    for g in range(KV_GROUPS):
        lanes = slice(g * HEAD_DIM, (g + 1) * HEAD_DIM)
        q4, slope = q_rows(g)
        q4r = _bf16_round(q4)
        s_p = jnp.where(key_sel[g:g + 1] > 0.5, _dot(q4, kbuf[g]) - slope * dist_s, NEG)
        new_ok = sel_new[g:g + 1] > 0.5
        s_n = jnp.where(new_ok, jnp.sum(q4r * _bf16_round(ksn_ref[0, :, lanes]), axis=-1, keepdims=True), NEG)
        m = jnp.maximum(jnp.max(s_p, axis=-1, keepdims=True), s_n)
        e_p = jnp.where(key_sel[g:g + 1] > 0.5, jnp.exp(s_p - m), 0.0)
        e_n = jnp.where(new_ok, jnp.exp(s_n - m), 0.0)
        den = jnp.maximum(jnp.sum(e_p, axis=-1, keepdims=True) + e_n, 1e-30)
        o_s = _dot_nt(e_p / den, vbuf[g]) + _bf16_round(e_n / den) * _bf16_round(vsn_ref[0, :, lanes])
        s_p = jnp.where(mask_w, _dot(q4, wk_ref[0, g]) - slope * dist_w.astype(F32), NEG)
        s_n = jnp.sum(q4r * _bf16_round(kwn_ref[0, :, lanes]), axis=-1, keepdims=True)
        m = jnp.maximum(jnp.max(s_p, axis=-1, keepdims=True), s_n)
        e_p = jnp.where(mask_w, jnp.exp(s_p - m), 0.0)
        e_n = jnp.exp(s_n - m)
        den = jnp.maximum(jnp.sum(e_p, axis=-1, keepdims=True) + e_n, 1e-30)
        o_w = _dot_nt(e_p / den, wv_ref[0, g]) + _bf16_round(e_n / den) * _bf16_round(vwn_ref[0, :, lanes])
        for j in range(hpg):
            h = g * hpg + j
            o_ref[0, :, h * HEAD_DIM:(h + 1) * HEAD_DIM] = (
                gt[:, 3 * h:3 * h + 1] * o_cs[g][j:j + 1] + gt[:, 3 * h + 1:3 * h + 2] * o_s[j:j + 1]
                + gt[:, 3 * h + 2:3 * h + 3] * o_w[j:j + 1])
        k_col = jnp.sum(eye * kwn_ref[0, :, lanes], axis=1, keepdims=True)
        v_col = jnp.sum(eye * vwn_ref[0, :, lanes], axis=1, keepdims=True)
        owk_ref[0, g] = jnp.where(lane_w == wb - 1, k_col, pltpu.roll(wk_ref[0, g], wb - 1, 1))
        owv_ref[0, g] = jnp.where(lane_w == wb - 1, v_col, pltpu.roll(wv_ref[0, g], wb - 1, 1))


def _nsa_sample(page_table, q, gates, ck, cv, ks_new, vs_new, kw_new, vw_new, pool_ks, pool_vs, win_k, win_v):
    bd, n_pages = page_table.shape
    n_cmp = ck.shape[1]
    wb = win_k.shape[1]
    one = lambda w: pl.BlockSpec((1, 1, w), lambda i, pt: (i, 0, 0))
    rows = lambda m: pl.BlockSpec((1, m, KVW), lambda i, pt: (i, 0, 0))
    win = pl.BlockSpec((1, KV_GROUPS, HEAD_DIM, wb), lambda i, pt: (i, 0, 0, 0))
    any_spec = pl.BlockSpec(memory_space=pl.ANY)
    grid_spec = pltpu.PrefetchScalarGridSpec(
        num_scalar_prefetch=1, grid=(bd,),
        in_specs=[one(QA), one(LANES), rows(n_cmp), rows(n_cmp), one(KVW), one(KVW), one(KVW), one(KVW),
                  any_spec, any_spec, win, win],
        out_specs=[one(QA), win, win],
        scratch_shapes=[pltpu.VMEM((KV_GROUPS, HEAD_DIM, n_pages * PAGE), F32),
                        pltpu.VMEM((KV_GROUPS, HEAD_DIM, n_pages * PAGE), F32),
                        pltpu.SemaphoreType.DMA((2,))])
    r3 = lambda z: z.reshape(bd, 1, -1)
    pos_minor = lambda z: jnp.transpose(z, (0, 2, 3, 1))
    win_shape = jax.ShapeDtypeStruct((bd, KV_GROUPS, HEAD_DIM, wb), F32)
    mix, nwk, nwv = pl.pallas_call(
        functools.partial(_nsa_sample_kernel, n_pages=n_pages),
        grid_spec=grid_spec,
        out_shape=[jax.ShapeDtypeStruct((bd, 1, QA), F32), win_shape, win_shape],
        compiler_params=_cparams("arbitrary"), name="nsa_sample",
    )(page_table, r3(q), r3(gates), ck, cv, r3(ks_new), r3(vs_new), r3(kw_new), r3(vw_new),
      pos_minor(pool_ks), pos_minor(pool_vs), pos_minor(win_k), pos_minor(win_v))
    back = lambda z: jnp.transpose(z, (0, 3, 1, 2))
    return mix, back(nwk), back(nwv)


def _mem_attn_kernel(q_ref, k_ref, v_ref, gain_ref, o_ref):
    tq = q_ref.shape[1]
    pad = max(8 - tq, 0)
    for h in range(MEM_HEADS):
        lanes = slice(h * HEAD_DIM, (h + 1) * HEAD_DIM)
        q = _rms(q_ref[0, :, lanes], gain_ref[...]) * (HEAD_DIM ** -0.5)
        if pad:
            q = jnp.concatenate([q, jnp.zeros((pad, HEAD_DIM), F32)], axis=0)
        s = _dot_nt(q, k_ref[0, :, lanes])
        e = jnp.exp(s - jnp.max(s, axis=-1, keepdims=True))
        p = e / jnp.sum(e, axis=-1, keepdims=True)
        o_ref[0, :, lanes] = _dot(p, v_ref[0, :, lanes])[0:tq]


def _mem_attn(mq, km, vm, gain, tq):
    b, t, _ = mq.shape
    m = km.shape[1]
    tq = min(tq, t)
    qspec = pl.BlockSpec((1, tq, MEMQ), lambda i, j: (i, j, 0))
    kspec = pl.BlockSpec((1, m, MEMQ), lambda i, j: (i, 0, 0))
    return pl.pallas_call(
        _mem_attn_kernel, grid=(b, t // tq),
        in_specs=[qspec, kspec, kspec, pl.BlockSpec((1, HEAD_DIM), lambda i, j: (0, 0))],
        out_specs=qspec, out_shape=jax.ShapeDtypeStruct((b, t, MEMQ), F32),
        compiler_params=_cparams("parallel", "parallel"), name="mem_attn",
    )(mq, km, vm, gain.reshape(1, HEAD_DIM))


def _outproj_router_kernel(x_ref, mix_ref, mem_ref, wo_ref, g_ref, wc_ref, bc_ref, wf_ref, bf_ref,
                           xo_ref, hn_ref, cw_ref):
    x = x_ref[...] + _dot(mix_ref[...], wo_ref[0:QA, :]) + _dot(mem_ref[...], wo_ref[QA:QA + MEMQ, :])
    xo_ref[...] = x
    hn = _rms(x, g_ref[...])
    hn_ref[...] = hn
    hn = hn.astype(BF16)
    lane = lax.broadcasted_iota(jnp.int32, (1, LANES), 1).astype(F32)
    lg = jnp.where(lane < N_GROUPS, _dot(hn, wc_ref[...]) + bc_ref[...], -jnp.inf)
    m = jnp.max(lg, axis=-1, keepdims=True)
    grp = jnp.min(jnp.where(lg == m, lane, 1e9), axis=-1, keepdims=True)
    p_grp = 1.0 / jnp.sum(jnp.exp(lg - m), axis=-1, keepdims=True)
    in_grp = (lane >= grp * E_PER_GROUP) & (lane < (grp + 1.0) * E_PER_GROUP)
    lf = jnp.where(in_grp, _dot(hn, wf_ref[...]) + bf_ref[...], -jnp.inf)
    v1 = jnp.max(lf, axis=-1, keepdims=True)
    i1 = jnp.min(jnp.where(lf == v1, lane, 1e9), axis=-1, keepdims=True)
    lf2 = jnp.where(lane == i1, -jnp.inf, lf)
    v2 = jnp.max(lf2, axis=-1, keepdims=True)
    i2 = jnp.min(jnp.where(lf2 == v2, lane, 1e9), axis=-1, keepdims=True)
    e2 = jnp.exp(v2 - v1)
    den = 1.0 + e2
    cw_ref[...] = jnp.where(lane == i1, p_grp / den, 0.0) + jnp.where(lane == i2, p_grp * (e2 / den), 0.0)


def _outproj_router(x, mix, mem, w_out, g_ffn, w_coarse, b_coarse, w_fine, b_fine, tm):
    n, d = x.shape
    tm = min(tm, n)
    pad_w = lambda w: jnp.pad(w, ((0, 0), (0, LANES - w.shape[1])))
    pad_b = lambda v: jnp.pad(v, (0, LANES - v.shape[0])).reshape(1, LANES)
    row = lambda w: pl.BlockSpec((tm, w), lambda i: (i, 0))
    const = lambda r, c: pl.BlockSpec((r, c), lambda i: (0, 0))
    return pl.pallas_call(
        _outproj_router_kernel, grid=(n // tm,),
        in_specs=[row(d), row(QA), row(MEMQ), const(QA + MEMQ, d), const(1, d), const(d, LANES),
                  const(1, LANES), const(d, LANES), const(1, LANES)],
        out_specs=[row(d), row(d), row(LANES)],
        out_shape=[jax.ShapeDtypeStruct((n, d), F32), jax.ShapeDtypeStruct((n, d), F32),
                   jax.ShapeDtypeStruct((n, LANES), F32)],
        compiler_params=_cparams("parallel"), name="outproj_router",
    )(x, mix, mem, w_out.astype(BF16), g_ffn.reshape(1, d), pad_w(w_coarse).astype(BF16), pad_b(b_coarse),
      pad_w(w_fine).astype(BF16), pad_b(b_fine))


def _moe_kernel(x_ref, hn_ref, cw_ref, wg_ref, wu_ref, wd_ref, o_ref):
    e = pl.program_id(1)

    @pl.when(e == 0)
    def _():
        o_ref[...] = x_ref[...]

    hb = hn_ref[...].astype(BF16)
    gate = _dot(hb, wg_ref[0])
    up = _dot(hb, wu_ref[0])
    lane = lax.broadcasted_iota(jnp.int32, (1, LANES), 1)
    c = jnp.sum(jnp.where(lane == e, cw_ref[...], 0.0), axis=-1, keepdims=True)
    o_ref[...] += _dot(gate * jax.nn.sigmoid(gate) * up, wd_ref[0]) * c


def _moe(x, hn, cw, w_gate, w_up, w_down, layer, tm):
    n, d = x.shape
    tm = min(tm, n)
    _, n_exp, _, d_exp = w_gate.shape
    row = lambda w: pl.BlockSpec((tm, w), lambda i, e: (i, 0))
    return pl.pallas_call(
        _moe_kernel, grid=(n // tm, n_exp),
        in_specs=[row(d), row(d), row(LANES),
                  pl.BlockSpec((None, 1, d, d_exp), lambda i, e: (layer, e, 0, 0)),
                  pl.BlockSpec((None, 1, d, d_exp), lambda i, e: (layer, e, 0, 0)),
                  pl.BlockSpec((None, 1, d_exp, d), lambda i, e: (layer, e, 0, 0))],
        out_specs=row(d), out_shape=jax.ShapeDtypeStruct((n, d), F32),
        compiler_params=_cparams("parallel", "arbitrary"), name="moe",
    )(x, hn, cw, w_gate, w_up, w_down)


def _moe_plan_kernel(cw_ref, posa_ref, posb_ref, te_ref, nu_ref, cnt_ref, run_ref, start_ref, *, row_tile, n_rows):
    phase, i = pl.program_id(0), pl.program_id(1)
    tm = cw_ref.shape[0]
    cwt = cw_ref[...].T
    mask = cwt != 0.0
    maskf = mask.astype(F32)
    tile_cnt = jnp.sum(maskf, axis=1, keepdims=True)

    @pl.when((phase == 0) & (i == 0))
    def _():
        cnt_ref[...] = jnp.zeros_like(cnt_ref)

    @pl.when(phase == 0)
    def _():
        cnt_ref[...] += tile_cnt

    @pl.when((phase == 1) & (i == 0))
    def _():
        padded = jnp.floor((cnt_ref[...] + (row_tile - 1)) * (1.0 / row_tile)) * row_tile
        lower = (lax.broadcasted_iota(jnp.int32, (LANES, LANES), 0)
                 > lax.broadcasted_iota(jnp.int32, (LANES, LANES), 1)).astype(F32)
        start = _dot(lower, jnp.broadcast_to(padded, (LANES, LANES)), exact=True)[:, 0:1]
        start_ref[...] = start
        run_ref[...] = jnp.zeros_like(run_ref)
        tile_lo = lax.broadcasted_iota(jnp.int32, (1, 2 * LANES), 1).astype(F32) * row_tile
        n_done = jnp.sum(jnp.where(start + padded <= tile_lo, 1.0, 0.0), axis=0, keepdims=True)
        te_ref[...] = jnp.minimum(n_done, N_EXPERTS - 1.0).astype(jnp.int32)
        total = jnp.sum(padded, axis=0, keepdims=True)
        nu_ref[...] = jnp.broadcast_to(total * (1.0 / row_tile), (1, LANES)).astype(jnp.int32)

    @pl.when(phase == 1)
    def _():
        before = (lax.broadcasted_iota(jnp.int32, (tm, tm), 0)
                  < lax.broadcasted_iota(jnp.int32, (tm, tm), 1)).astype(BF16)
        pos = start_ref[...] + run_ref[...] + _dot(maskf, before)
        posa = jnp.min(jnp.where(mask, pos, 3e38), axis=0, keepdims=True)
        posb = jnp.max(jnp.where(mask, pos, -1.0), axis=0, keepdims=True)
        posb = jnp.where(jnp.sum(maskf, axis=0, keepdims=True) > 1.5, posb, n_rows - 1.0)
        posa_ref[0] = posa.astype(jnp.int32)
        posb_ref[0] = posb.astype(jnp.int32)
        run_ref[...] += tile_cnt


def _moe_row_copies(pos_refs, make_copy, tm):
    def issue(t, carry):
        for pos_ref in pos_refs:
            make_copy(pos_ref, t).start()
        return carry

    def drain(t, carry):
        for pos_ref in pos_refs:
            make_copy(pos_ref, 0).wait()
        return carry

    lax.fori_loop(0, tm, issue, 0, unroll=8)
    lax.fori_loop(0, tm, drain, 0, unroll=8)


def _moe_dispatch_kernel(posa_ref, posb_ref, hn_ref, xs_zero_ref, xs_ref, sem):
    del xs_zero_ref

    def make_copy(pos_ref, t):
        return pltpu.make_async_copy(hn_ref.at[pl.ds(t, 1)], xs_ref.at[pl.ds(pos_ref[0, 0, t], 1)], sem)

    _moe_row_copies((posa_ref, posb_ref), make_copy, hn_ref.shape[0])


def _moe_experts_kernel(te_ref, nu_ref, xs_ref, wg_ref, wu_ref, wd_ref, ys_ref):
    del te_ref
    used = pl.program_id(0) < nu_ref[0]

    @pl.when(used)
    def _():
        hb = xs_ref[...].astype(BF16)
        gate = _dot(hb, wg_ref[0])
        up = _dot(hb, wu_ref[0])
        ys_ref[...] = _dot(gate * jax.nn.sigmoid(gate) * up, wd_ref[0])

    @pl.when(jnp.logical_not(used))
    def _():
        ys_ref[...] = jnp.zeros_like(ys_ref)


def _moe_combine_kernel(posa_ref, posb_ref, x_ref, cw_ref, ys_ref, o_ref, ya_ref, yb_ref, sem):
    def make_copy(pos_ref, t):
        dst = ya_ref if pos_ref is posa_ref else yb_ref
        return pltpu.make_async_copy(ys_ref.at[pl.ds(pos_ref[0, 0, t], 1)], dst.at[pl.ds(t, 1)], sem)

    _moe_row_copies((posa_ref, posb_ref), make_copy, x_ref.shape[0])
    cw = cw_ref[...]
    lane = lax.broadcasted_iota(jnp.int32, (1, LANES), 1).astype(F32)
    routed = cw != 0.0
    ea = jnp.min(jnp.where(routed, lane, 1e9), axis=-1, keepdims=True)
    eb = jnp.max(jnp.where(routed, lane, -1.0), axis=-1, keepdims=True)
    wa = jnp.sum(jnp.where(lane == ea, cw, 0.0), axis=-1, keepdims=True)
    wb = jnp.sum(jnp.where((lane == eb) & (eb != ea), cw, 0.0), axis=-1, keepdims=True)
    o_ref[...] = x_ref[...] + (wa * ya_ref[...] + wb * yb_ref[...])


def _moe_routed(x, hn, cw, w_gate, w_up, w_down, layer):
    n, d = x.shape
    _, n_exp, _, d_exp = w_gate.shape
    tm, tr = MOE_TOKEN_TILE, MOE_ROW_TILE
    n_tiles = n // tm
    n_row_tiles = (2 * n + n_exp * (tr - 1)) // tr + 1
    n_rows = n_row_tiles * tr
    assert n % tm == 0 and n_row_tiles <= 2 * LANES
    pos_shape = jax.ShapeDtypeStruct((n_tiles, 1, tm), jnp.int32)
    posa, posb, tile_expert, n_used = pl.pallas_call(
        functools.partial(_moe_plan_kernel, row_tile=tr, n_rows=n_rows),
        grid=(2, n_tiles),
        in_specs=[pl.BlockSpec((tm, LANES), lambda p, i: (i, 0))],
        out_specs=[pl.BlockSpec((1, 1, tm), lambda p, i: (i * p, 0, 0)),
                   pl.BlockSpec((1, 1, tm), lambda p, i: (i * p, 0, 0)),
                   pl.BlockSpec((1, 2 * LANES), lambda p, i: (0, 0)),
                   pl.BlockSpec((1, LANES), lambda p, i: (0, 0))],
        out_shape=[pos_shape, pos_shape, jax.ShapeDtypeStruct((1, 2 * LANES), jnp.int32),
                   jax.ShapeDtypeStruct((1, LANES), jnp.int32)],
        scratch_shapes=[pltpu.VMEM((LANES, 1), F32)] * 3,
        compiler_params=_cparams("arbitrary", "arbitrary"), name="moe_plan",
    )(cw)
    pos_spec = pl.BlockSpec((1, 1, tm), lambda i: (i, 0, 0), memory_space=pltpu.SMEM)
    any_spec = pl.BlockSpec(memory_space=pl.ANY)
    row = lambda w: pl.BlockSpec((tm, w), lambda i: (i, 0))
    xs = pl.pallas_call(
        _moe_dispatch_kernel, grid=(n_tiles,),
        in_specs=[pos_spec, pos_spec, row(d), any_spec],
        out_specs=any_spec, out_shape=jax.ShapeDtypeStruct((n_rows, d), F32),
        scratch_shapes=[pltpu.SemaphoreType.DMA(())],
        input_output_aliases={3: 0},
        compiler_params=_cparams("arbitrary"), name="moe_dispatch",
    )(posa, posb, hn, jnp.zeros((n_rows, d), F32))
    wspec = lambda a, c: pl.BlockSpec((None, 1, a, c), lambda i, te, nu: (layer, te[i], 0, 0))
    ys = pl.pallas_call(
        _moe_experts_kernel,
        grid_spec=pltpu.PrefetchScalarGridSpec(
            num_scalar_prefetch=2, grid=(n_row_tiles,),
            in_specs=[pl.BlockSpec((tr, d), lambda i, te, nu: (i, 0)), wspec(d, d_exp), wspec(d, d_exp),
                      wspec(d_exp, d)],
            out_specs=pl.BlockSpec((tr, d), lambda i, te, nu: (i, 0))),
        out_shape=jax.ShapeDtypeStruct((n_rows, d), F32),
        compiler_params=_cparams("arbitrary"), name="moe_experts",
    )(tile_expert.reshape(-1), n_used.reshape(-1), xs, w_gate, w_up, w_down)
    return pl.pallas_call(
        _moe_combine_kernel, grid=(n_tiles,),
        in_specs=[pos_spec, pos_spec, row(d), row(LANES), any_spec],
        out_specs=row(d), out_shape=jax.ShapeDtypeStruct((n, d), F32),
        scratch_shapes=[pltpu.VMEM((tm, d), F32), pltpu.VMEM((tm, d), F32), pltpu.SemaphoreType.DMA(())],
        compiler_params=_cparams("arbitrary"), name="moe_combine",
    )(posa, posb, x, cw, ys)


def _rwkv_prep_body(p, prev, mu_ref, w0_ref, wup_ref, a0_ref, aup_ref, gup_ref, kk_ref, ka_ref, outs):
    r_ref, ld_ref, k_ref, v_ref, kkn_ref, a_ref, g_ref = outs
    x = p + mu_ref[...] * (prev - p)
    c0, c1, c2, c3, c4 = 768, 1536, 2304, 2368, 2432
    r, k, v = x[:, 0:c0], x[:, c0:c1], x[:, c1:c2]
    xw, xa, xg = x[:, c2:c3], x[:, c3:c4], x[:, c4:SHIFT_W]
    z = -(w0_ref[...] + _dot(jnp.tanh(xw), wup_ref[...]))
    softplus = jnp.maximum(z, 0.0) + jnp.log(1.0 + jnp.exp(-jnp.abs(z)))
    w = -softplus - 0.5
    a = jax.nn.sigmoid(a0_ref[...] + _dot(xa, aup_ref[...]))
    r_ref[...] = r
    ld_ref[...] = -jnp.exp(w)
    k_ref[...] = k * (1.0 + (a - 1.0) * ka_ref[...])
    v_ref[...] = v
    a_ref[...] = a
    g_ref[...] = _dot(jax.nn.sigmoid(xg), gup_ref[...])
    kk = k * kk_ref[...]
    for h in range(TOK_HEADS):
        lanes = slice(h * HEAD_DIM, (h + 1) * HEAD_DIM)
        seg = kk[:, lanes]
        nrm = jnp.sqrt(jnp.sum(seg * seg, axis=-1, keepdims=True))
        kkn_ref[:, lanes] = seg / jnp.maximum(nrm, 1e-12)


def _rwkv_prep_prompt_kernel(p_ref, mu_ref, w0_ref, wup_ref, a0_ref, aup_ref, gup_ref, kk_ref, ka_ref,
                             *rest):
    outs, carry = rest[:7], rest[7]
    tm = p_ref.shape[1]

    @pl.when(pl.program_id(1) == 0)
    def _():
        carry[...] = jnp.zeros_like(carry)

    p = p_ref[0]
    first = lax.broadcasted_iota(jnp.int32, (tm, 1), 0) == 0
    prev = jnp.where(first, carry[...], pltpu.roll(p, 1, 0))
    carry[...] = p[tm - 1:tm]
    _rwkv_prep_body(p, prev, mu_ref, w0_ref, wup_ref, a0_ref, aup_ref, gup_ref, kk_ref, ka_ref,
                    [o.at[0] for o in outs])


def _rwkv_prep_sample_kernel(p_ref, prev_ref, mu_ref, w0_ref, wup_ref, a0_ref, aup_ref, gup_ref, kk_ref,
                             ka_ref, *outs):
    _rwkv_prep_body(p_ref[...], prev_ref[...], mu_ref, w0_ref, wup_ref, a0_ref, aup_ref, gup_ref, kk_ref,
                    ka_ref, outs)


def _rwkv_weight_args(mu, w0, w_up, a0, a_up, g_up, k_k, k_a):
    row = lambda v: v.reshape(1, -1)
    return (row(mu), row(w0), w_up, row(a0), a_up, g_up, row(k_k), row(k_a))


def _rwkv_prep_prompt(pr, wargs, tm):
    b, s, _ = pr.shape
    tm = min(tm, s)
    const = lambda a: pl.BlockSpec(a.shape, lambda i, j: (0, 0))
    ospec = pl.BlockSpec((1, tm, QA), lambda i, j: (i, j, 0))
    return pl.pallas_call(
        _rwkv_prep_prompt_kernel, grid=(b, s // tm),
        in_specs=[pl.BlockSpec((1, tm, SHIFT_W), lambda i, j: (i, j, 0))] + [const(a) for a in wargs],
        out_specs=[ospec] * 7, out_shape=[jax.ShapeDtypeStruct((b, s, QA), F32)] * 7,
        scratch_shapes=[pltpu.VMEM((1, SHIFT_W), F32)],
        compiler_params=_cparams("parallel", "arbitrary"), name="rwkv_prep_prompt",
    )(pr, *wargs)


def _rwkv_prep_sample(pr, prev, wargs):
    n = pr.shape[0]
    full = lambda a: pl.BlockSpec(a.shape, lambda i: (0, 0))
    return pl.pallas_call(
        _rwkv_prep_sample_kernel, grid=(1,),
        in_specs=[full(pr), full(prev)] + [full(a) for a in wargs],
        out_specs=[pl.BlockSpec((n, QA), lambda i: (0, 0))] * 7,
        out_shape=[jax.ShapeDtypeStruct((n, QA), F32)] * 7,
        compiler_params=_cparams("arbitrary"), name="rwkv_prep_sample",
    )(pr, prev, *wargs)


def _rwkv_finish(y, r, k, v, g, rk, lnw, lnb):
    m = jnp.mean(y, axis=-1, keepdims=True)
    var = jnp.mean(jnp.square(y - m), axis=-1, keepdims=True)
    yn = (y - m) * lax.rsqrt(var + RWKV_GN_EPS) * lnw + lnb
    bonus = jnp.sum(r * k * rk, axis=-1, keepdims=True) * v
    return (yn + bonus) * g


def _rwkv_chunk_kernel(r_ref, ld_ref, k_ref, v_ref, kk_ref, a_ref, g_ref, rk_ref, lnw_ref, lnb_ref,
                       o_ref, st_ref, z_ref):
    c = r_ref.shape[1]
    ci = pl.program_id(1)

    @pl.when(ci == 0)
    def _():
        z_ref[...] = jnp.zeros_like(z_ref)

    ri = lax.broadcasted_iota(jnp.int32, (c, c), 0)
    cj = lax.broadcasted_iota(jnp.int32, (c, c), 1)
    incl = ri >= cj
    strict = ri > cj
    n = HEAD_DIM
    eye = (lax.broadcasted_iota(jnp.int32, (n, n), 0) == lax.broadcasted_iota(jnp.int32, (n, n), 1)).astype(F32)
    ex = RWKV_CHUNK_EXACT
    hl = [slice(h * n, (h + 1) * n) for h in range(TOK_HEADS)]
    prep = []
    for bi in range(r_ref.shape[0]):
        ld, r_all, k_all, kk_all = ld_ref[bi], r_ref[bi], k_ref[bi], kk_ref[bi]
        lc = _dot(incl.astype(F32), ld, exact=True)
        l_end = lc[c - 1:c]
        b_all = kk_all * a_ref[bi]
        e_neg = jnp.exp(-lc)
        e_rem = jnp.exp(l_end - lc)
        prep.append(dict(
            r=r_all, k=k_all, v=v_ref[bi],
            at=-kk_all * jnp.exp(lc - ld),
            rt=r_all * jnp.exp(lc),
            bt=b_all * e_neg, kt=k_all * e_neg,
            bh=b_all * e_rem, kh=k_all * e_rem,
            p_end=jnp.exp(l_end)))
    units = [(bi, h) for bi in range(r_ref.shape[0]) for h in range(TOK_HEADS)]
    col = lambda name, u: prep[u[0]][name][:, hl[u[1]]]
    v = [col("v", u) for u in units]
    big = [_dot_nt(jnp.concatenate([col("at", u), col("rt", u)], axis=0),
                   jnp.concatenate([col("bt", u), col("kt", u)], axis=0), ex) for u in units]
    a_rbk = [jnp.concatenate([jnp.where(incl, m[c:2 * c, 0:c], 0.0), jnp.where(incl, m[c:2 * c, c:2 * c], 0.0)],
                             axis=1) for m in big]
    akv = [_dot(jnp.where(strict, m[0:c, c:2 * c], 0.0), vv, ex) for m, vv in zip(big, v)]
    x = [jnp.concatenate([col("at", u), t], axis=1) for u, t in zip(units, akv)]
    npow = [jnp.where(strict, m[0:c, 0:c], 0.0) for m in big]
    x = [xx + _dot(m, xx, ex) for m, xx in zip(npow, x)]
    steps = 1
    while 2 * steps < c:
        npow = [_dot(m, m, ex) for m in npow]
        x = [xx + _dot(m, xx, ex) for m, xx in zip(npow, x)]
        steps *= 2
    zeros = jnp.zeros((c, n), F32)
    wv = [jnp.concatenate([xx, jnp.concatenate([zeros, vv], axis=1)], axis=0) for xx, vv in zip(x, v)]
    rq_y0 = [_dot(m, w, ex) for m, w in zip(a_rbk, wv)]
    m_n = [_dot_tn(jnp.concatenate([col("bh", u), col("kh", u)], axis=0), w, ex)
           for u, w in zip(units, wv)]
    yz = [_dot(jnp.concatenate([col("rt", u) + rq_y0[i][:, 0:n],
                                m_n[i][:, 0:n] + eye * col("p_end", u)], axis=0), z_ref[u[0], u[1]], ex)
          for i, u in enumerate(units)]
    for i, (bi, h) in enumerate(units):
        z_ref[bi, h] = yz[i][c:c + n] + m_n[i][:, n:2 * n]
        y = yz[i][0:c] + rq_y0[i][:, n:2 * n]
        o_ref[bi, :, hl[h]] = _rwkv_finish(y, col("r", (bi, h)), col("k", (bi, h)), v[i], g_ref[bi, :, hl[h]],
                                           rk_ref[:, hl[h]], lnw_ref[:, hl[h]], lnb_ref[:, hl[h]])

    @pl.when(ci == pl.num_programs(1) - 1)
    def _():
        for bi, h in units:
            st_ref[bi, h] = z_ref[bi, h].T


def _rwkv_chunk(r, ld, k, v, kk, a, g, r_k, ln_w, ln_b):
    b, s, _ = r.shape
    c = RWKV_CHUNK
    nb = RWKV_ROWS_PER_STEP if b % RWKV_ROWS_PER_STEP == 0 else 1
    tok = pl.BlockSpec((nb, c, QA), lambda i, j: (i, j, 0))
    const = pl.BlockSpec((1, QA), lambda i, j: (0, 0))
    return pl.pallas_call(
        _rwkv_chunk_kernel, grid=(b // nb, s // c),
        in_specs=[tok] * 7 + [const] * 3,
        out_specs=[tok, pl.BlockSpec((nb, TOK_HEADS, HEAD_DIM, HEAD_DIM), lambda i, j: (i, 0, 0, 0))],
        out_shape=[jax.ShapeDtypeStruct((b, s, QA), F32),
                   jax.ShapeDtypeStruct((b, TOK_HEADS, HEAD_DIM, HEAD_DIM), F32)],
        scratch_shapes=[pltpu.VMEM((nb, TOK_HEADS, HEAD_DIM, HEAD_DIM), F32)],
        compiler_params=_cparams("parallel", "arbitrary"), name="rwkv_chunk",
    )(r, ld, k, v, kk, a, g, r_k.reshape(1, QA), ln_w.reshape(1, QA), ln_b.reshape(1, QA))


def _rwkv_step_kernel(r_ref, ld_ref, k_ref, v_ref, kk_ref, a_ref, g_ref, rk_ref, lnw_ref, lnb_ref, s_ref,
                      o_ref, so_ref):
    n = HEAD_DIM
    eye = (lax.broadcasted_iota(jnp.int32, (n, n), 0) == lax.broadcasted_iota(jnp.int32, (n, n), 1)).astype(F32)
    units = [(bi, slice(h * n, (h + 1) * n), h) for bi in range(r_ref.shape[0]) for h in range(TOK_HEADS)]
    row = lambda ref: [ref[bi, :, lanes] for bi, lanes, _ in units]
    r, k, v, kk, a, ld = row(r_ref), row(k_ref), row(v_ref), row(kk_ref), row(a_ref), row(ld_ref)
    s = [s_ref[bi, h] for bi, _, h in units]
    sa = [jnp.sum(_bf16_round(si) * _bf16_round(-kki), axis=1, keepdims=True) for si, kki in zip(s, kk)]
    v_col = [jnp.sum(eye * vi, axis=1, keepdims=True) for vi in v]
    s_new = [si * jnp.exp(ldi) + sai * (kki * ai) + vci * ki
             for si, ldi, sai, kki, ai, vci, ki in zip(s, ld, sa, kk, a, v_col, k)]
    y_col = [jnp.sum(_bf16_round(si) * _bf16_round(ri), axis=1, keepdims=True) for si, ri in zip(s_new, r)]
    y = [jnp.sum(eye * yc, axis=0, keepdims=True) for yc in y_col]
    for i, (bi, lanes, h) in enumerate(units):
        so_ref[bi, h] = s_new[i]
        o_ref[bi, :, lanes] = _rwkv_finish(y[i], r[i], k[i], v[i], g_ref[bi, :, lanes], rk_ref[:, lanes],
                                           lnw_ref[:, lanes], lnb_ref[:, lanes])


def _rwkv_step(r, ld, k, v, kk, a, g, r_k, ln_w, ln_b, state):
    n = r.shape[0]
    nb = RWKV_STEP_ROWS if n % RWKV_STEP_ROWS == 0 else 1
    tok = pl.BlockSpec((nb, 1, QA), lambda i: (i, 0, 0))
    const = pl.BlockSpec((1, QA), lambda i: (0, 0))
    st = pl.BlockSpec((nb, TOK_HEADS, HEAD_DIM, HEAD_DIM), lambda i: (i, 0, 0, 0))
    r3 = lambda z: z.reshape(n, 1, QA)
    return pl.pallas_call(
        _rwkv_step_kernel, grid=(n // nb,),
        in_specs=[tok] * 7 + [const] * 3 + [st],
        out_specs=[tok, st],
        out_shape=[jax.ShapeDtypeStruct((n, 1, QA), F32), jax.ShapeDtypeStruct(state.shape, F32)],
        compiler_params=_cparams("parallel"), name="rwkv_step",
    )(r3(r), r3(ld), r3(k), r3(v), r3(kk), r3(a), r3(g), r_k.reshape(1, QA), ln_w.reshape(1, QA),
      ln_b.reshape(1, QA), state)


def _split_w_in_a(w):
    offs = [0, QA]
    for _ in range(6):
        offs.append(offs[-1] + KVW)
    offs.append(offs[-1] + 3 * TOK_HEADS)
    offs.append(offs[-1] + MEMQ)
    pieces = [w[:, offs[i]:offs[i + 1]] for i in range(9)]
    pieces[7] = jnp.pad(pieces[7], ((0, 0), (0, LANES - 3 * TOK_HEADS)))
    return pieces


_A_OPS = ("hnorm", "none", "none", "hnorm", "none", "hnorm", "none", "sigmoid", "none")


def _ffn(x, mix, mem, w_out, g_ffn, wc, bc, wf, bf, w_gate, w_up, w_down, layer, tm_proj, tm_moe):
    x_new, hn, cw = _outproj_router(x, mix, mem, w_out, g_ffn, wc, bc, wf, bf, tm_proj)
    if x.shape[0] % MOE_TOKEN_TILE == 0:
        return _moe_routed(x_new, hn, cw, w_gate, w_up, w_down, layer)
    return _moe(x_new, hn, cw, w_gate, w_up, w_down, layer, tm_moe)


def kernel(x_prompt, x_sample, cache_cmp_k, cache_cmp_v, cache_sel_k, cache_sel_v, cache_win_k, cache_win_v, cache_mem_k, cache_mem_v, state_rwkv, state_shift, page_table, mem_prompt, norm_mix, norm_ffn, norm_mem, w_mem_kv, mem_q_gain, mem_k_gain, w_in_a, nsa_q_gain, nsa_k_gain, cmp_pe, cmp_w1, cmp_b1, cmp_w2, w_in_b, rwkv_mu, rwkv_w0, rwkv_w_up, rwkv_a0, rwkv_a_up, rwkv_g_up, rwkv_k_k, rwkv_k_a, rwkv_r_k, rwkv_ln_w, rwkv_ln_b, w_out, moe_w_coarse, moe_b_coarse, moe_w_fine, moe_b_fine, moe_w_gate, moe_w_up, moe_w_down):
    b, s, d = x_prompt.shape
    bd = x_sample.shape[0]
    depth = norm_mix.shape[0]
    m_len = mem_prompt.shape[1]
    wl = min(WINDOW, s)
    xp = x_prompt.reshape(b * s, d)
    xs = x_sample.reshape(bd, d)
    mem2 = mem_prompt.reshape(b * m_len, d)
    outs = {name: [] for name in ("pc_k", "pc_v", "ps_k", "ps_v", "pw_k", "pw_v", "pm_k", "pm_v", "pr_s", "pr_x",
                                  "sc_k", "sc_v", "ss_k", "ss_v", "sw_k", "sw_v", "sr_s", "sr_x")}
    for i in range(depth):
        km_p, vm_p = _norm_proj(mem2, norm_mem[i], [w_mem_kv[i][:, :MEMQ], w_mem_kv[i][:, MEMQ:]],
                                ("hnorm", "none"), [mem_k_gain[i]], 256)
        km_p, vm_p = km_p.reshape(b, m_len, MEMQ), vm_p.reshape(b, m_len, MEMQ)
        outs["pm_k"].append(km_p.reshape(b, m_len, MEM_HEADS, HEAD_DIM))
        outs["pm_v"].append(vm_p.reshape(b, m_len, MEM_HEADS, HEAD_DIM))
        if i % 2 == 0:
            ia = i // 2
            pieces = _split_w_in_a(w_in_a[ia])
            gains = [nsa_q_gain[ia], nsa_k_gain[ia, 1], nsa_k_gain[ia, 2]]
            cmp_args = (cmp_pe[ia], cmp_w1[ia], cmp_b1[ia], cmp_w2[ia], nsa_k_gain[ia, 0])
            q, kc, vc, ks, vs, kw, vw, gt, mq_p = _norm_proj(xp, norm_mix[i], pieces, _A_OPS, gains, PROJ_ROWS)
            r3 = lambda z: z.reshape(b, s, -1)
            kc, vc, ks, vs, kw, vw = (r3(z) for z in (kc, vc, ks, vs, kw, vw))
            ck, cv = _compress_prompt(kc, vc, *cmp_args)
            mix_p = _nsa_prompt(r3(q), r3(gt), ck, cv, ks, vs, kw, vw).reshape(b * s, QA)
            r5 = lambda z: z.reshape(b, -1, KV_GROUPS, HEAD_DIM)
            for name, z in (("pc_k", kc), ("pc_v", vc), ("ps_k", ks), ("ps_v", vs),
                            ("pw_k", kw[:, s - wl:]), ("pw_v", vw[:, s - wl:])):
                outs[name].append(r5(z))
            q, kc, vc, ks, vs, kw, vw, gt, mq_s = _norm_proj(xs, norm_mix[i], pieces, _A_OPS, gains, 256)
            ck, cv = _compress_sample(page_table, cache_cmp_k[ia], cache_cmp_v[ia], *cmp_args)
            mix_s, nwk, nwv = _nsa_sample(page_table, q, gt, ck, cv, ks, vs, kw, vw,
                                          cache_sel_k[ia], cache_sel_v[ia], cache_win_k[ia], cache_win_v[ia])
            mix_s = mix_s.reshape(bd, QA)
            r5 = lambda z: z.reshape(bd, -1, KV_GROUPS, HEAD_DIM)
            for name, z in (("sc_k", kc), ("sc_v", vc), ("ss_k", ks), ("ss_v", vs), ("sw_k", nwk), ("sw_v", nwv)):
                outs[name].append(r5(z))
        else:
            ib = i // 2
            pieces = [w_in_b[ib][:, :SHIFT_W], w_in_b[ib][:, SHIFT_W:]]
            wargs = _rwkv_weight_args(rwkv_mu[ib], rwkv_w0[ib], rwkv_w_up[ib], rwkv_a0[ib], rwkv_a_up[ib],
                                      rwkv_g_up[ib], rwkv_k_k[ib], rwkv_k_a[ib])
            fin = (rwkv_r_k[ib].reshape(-1), rwkv_ln_w[ib], rwkv_ln_b[ib])
            pr, mq_p = _norm_proj(xp, norm_mix[i], pieces, ("none", "none"), [], PROJ_ROWS)
            pr = pr.reshape(b, s, SHIFT_W)
            prep = _rwkv_prep_prompt(pr, wargs, 256)
            mix_p, st_p = _rwkv_chunk(*prep, *fin)
            mix_p = mix_p.reshape(b * s, QA)
            outs["pr_s"].append(st_p)
            outs["pr_x"].append(pr[:, s - 1])
            pr, mq_s = _norm_proj(xs, norm_mix[i], pieces, ("none", "none"), [], 256)
            prep = _rwkv_prep_sample(pr, state_shift[ib], wargs)
            mix_s, st_s = _rwkv_step(*prep, *fin, state_rwkv[ib])
            mix_s = mix_s.reshape(bd, QA)
            outs["sr_s"].append(st_s)
            outs["sr_x"].append(pr)
        mem_p = _mem_attn(mq_p.reshape(b, s, MEMQ), km_p, vm_p, mem_q_gain[i], 256).reshape(b * s, MEMQ)
        mem_s = _mem_attn(mq_s.reshape(bd, 1, MEMQ), cache_mem_k[i].reshape(bd, -1, MEMQ),
                          cache_mem_v[i].reshape(bd, -1, MEMQ), mem_q_gain[i], 1).reshape(bd, MEMQ)
        ffn_w = (w_out[i], norm_ffn[i], moe_w_coarse[i], moe_b_coarse[i], moe_w_fine[i], moe_b_fine[i],
                 moe_w_gate, moe_w_up, moe_w_down, i)
        xp = _ffn(xp, mix_p, mem_p, *ffn_w, PROJ_ROWS, 1024)
        xs = _ffn(xs, mix_s, mem_s, *ffn_w, PROJ_ROWS, 1024)
    order = ("pc_k", "pc_v", "ps_k", "ps_v", "pw_k", "pw_v", "pm_k", "pm_v", "pr_s", "pr_x",
             "sc_k", "sc_v", "ss_k", "ss_v", "sw_k", "sw_v", "sr_s", "sr_x")
    return (xp.reshape(b, s, d), xs.reshape(bd, 1, d)) + tuple(jnp.stack(outs[name]) for name in order)
```

```python
import functools

import jax
import jax.numpy as jnp
from jax import lax
from jax.experimental import pallas as pl
from jax.experimental.pallas import tpu as pltpu

F32 = jnp.float32
BF16 = jnp.bfloat16
HI = lax.Precision.HIGHEST

HEAD_DIM = 64
TOK_HEADS = 12
MEM_HEADS = 4
KV_GROUPS = 3
HEADS_PER_GROUP = TOK_HEADS // KV_GROUPS
QA = TOK_HEADS * HEAD_DIM
KVW = KV_GROUPS * HEAD_DIM
MEMQ = MEM_HEADS * HEAD_DIM
CMP_LEN = 32
CMP_STRIDE = 16
CMP_HID = 128
SEL_LEN = 64
SEL_TOPK = 16
WINDOW = 512
Q_BLOCK = 256
MEM_Q_ROWS = 1024
PROJ_ROWS = 1024
NSA_KEY_STEP = 512
PAGE = 128
N_GROUPS = 4
E_PER_GROUP = 8
N_EXPERTS = 32
MOE_TOKEN_TILE = 512
MOE_ROW_TILE = 512
RWKV_COLS = (768, 768, 768, 64, 64, 128)
SHIFT_W = sum(RWKV_COLS)
RWKV_GN_EPS = 64e-5
RWKV_CHUNK = 64
RWKV_STEP_ROWS = 4
RWKV_ROWS_PER_STEP = 2
RWKV_CHUNK_EXACT = False
NEG = -1e30
BIG = 1e30
LANES = 128
VMEM_LIMIT = 56 * 1024 * 1024


def _cparams(*sem):
    return pltpu.CompilerParams(dimension_semantics=sem, vmem_limit_bytes=VMEM_LIMIT)


def _dot_general(a, b, dims, exact):
    if exact:
        return lax.dot_general(a, b, (dims, ((), ())), precision=HI, preferred_element_type=F32)
    return lax.dot_general(a.astype(BF16), b.astype(BF16), (dims, ((), ())), preferred_element_type=F32)


def _dot(a, b, exact=False):
    return _dot_general(a, b, ((1,), (0,)), exact)


def _dot_nt(a, b, exact=False):
    return _dot_general(a, b, ((1,), (1,)), exact)


def _dot_tn(a, b, exact=False):
    return _dot_general(a, b, ((0,), (0,)), exact)


def _bf16_round(x):
    return x.astype(BF16).astype(F32)


def _rms(x, g, eps=1e-6):
    return x * lax.rsqrt(jnp.mean(x * x, axis=-1, keepdims=True) + eps) * g


def _masked_softmax(s, mask):
    s = jnp.where(mask, s, NEG)
    m = jnp.max(s, axis=-1, keepdims=True)
    e = jnp.where(mask, jnp.exp(s - m), 0.0)
    return e / jnp.maximum(jnp.sum(e, axis=-1, keepdims=True), 1e-30)


def _alibi_slope(h):
    return 2.0 ** (-8.0 * (h + 1) / TOK_HEADS)


def _topk_mask(score, k, n_real):
    lane = lax.broadcasted_iota(jnp.int32, score.shape, 1)
    ahead = jnp.zeros(score.shape, F32)
    for j in range(n_real):
        col = score[:, j:j + 1]
        ahead = ahead + jnp.where((col > score) | ((col == score) & (lane > j)), 1.0, 0.0)
    return jnp.where(ahead < k, 1.0, 0.0)


def _topk_rows_t(score_t, k):
    n_blk = score_t.shape[0]
    idx = lax.broadcasted_iota(jnp.int32, score_t.shape, 0)
    ahead = jnp.zeros(score_t.shape, F32)
    for j in range(n_blk):
        row = score_t[j:j + 1]
        ahead = ahead + jnp.where((row > score_t) | ((row == score_t) & (idx > j)), 1.0, 0.0)
    return jnp.where(ahead < k, 1.0, 0.0)


def _sel_to_cmp(n_sel_rows, n_cmp_lanes):
    s0 = lax.broadcasted_iota(jnp.int32, (n_sel_rows, n_cmp_lanes), 0) * SEL_LEN
    c0 = lax.broadcasted_iota(jnp.int32, (n_sel_rows, n_cmp_lanes), 1) * CMP_STRIDE
    return ((c0 < s0 + SEL_LEN) & (c0 + CMP_LEN > s0)).astype(F32)


def _cmp_to_sel(n_cmp_rows, n_sel_lanes):
    n_i = lax.broadcasted_iota(jnp.int32, (n_cmp_rows, n_sel_lanes), 0)
    s_i = lax.broadcasted_iota(jnp.int32, (n_cmp_rows, n_sel_lanes), 1)
    c0 = n_i * CMP_STRIDE
    s0 = s_i * SEL_LEN
    return ((c0 < s0 + SEL_LEN) & (c0 + CMP_LEN > s0)).astype(F32)


def _block_expand(n_blk_rows, n_keys):
    b_i = lax.broadcasted_iota(jnp.int32, (n_blk_rows, n_keys), 0)
    k_i = lax.broadcasted_iota(jnp.int32, (n_blk_rows, n_keys), 1)
    return (jnp.right_shift(k_i, 6) == b_i).astype(BF16)


def _norm_proj_kernel(ops, x_ref, g_ref, *refs):
    n = len(ops)
    n_gain = sum(op == "hnorm" for op in ops)
    w_refs, gain_refs, o_refs = refs[:n], refs[n:n + n_gain], refs[n + n_gain:]
    h = _rms(x_ref[...], g_ref[...]).astype(BF16)
    gi = 0
    for op, w_ref, o_ref in zip(ops, w_refs, o_refs):
        z = _dot(h, w_ref[...])
        if op == "hnorm":
            gain = gain_refs[gi][...]
            gi += 1
            for hh in range(z.shape[1] // HEAD_DIM):
                seg = z[:, hh * HEAD_DIM:(hh + 1) * HEAD_DIM]
                o_ref[:, hh * HEAD_DIM:(hh + 1) * HEAD_DIM] = _rms(seg, gain)
        elif op == "sigmoid":
            o_ref[...] = jax.nn.sigmoid(z)
        else:
            o_ref[...] = z


def _norm_proj(x, g, weights, ops, gains, tm):
    n_rows, d = x.shape
    tm = min(tm, n_rows)
    assert n_rows % tm == 0
    in_specs = [pl.BlockSpec((tm, d), lambda i: (i, 0)), pl.BlockSpec((1, d), lambda i: (0, 0))]
    in_specs += [pl.BlockSpec(w.shape, lambda i: (0, 0)) for w in weights]
    in_specs += [pl.BlockSpec((1, HEAD_DIM), lambda i: (0, 0)) for _ in gains]
    out_shape = [jax.ShapeDtypeStruct((n_rows, w.shape[1]), F32) for w in weights]
    out_specs = [pl.BlockSpec((tm, w.shape[1]), lambda i: (i, 0)) for w in weights]
    return pl.pallas_call(
        functools.partial(_norm_proj_kernel, tuple(ops)),
        grid=(n_rows // tm,), in_specs=in_specs, out_specs=out_specs, out_shape=out_shape,
        compiler_params=_cparams("parallel"), name="norm_proj",
    )(x, g.reshape(1, d), *[w.astype(BF16) for w in weights], *[gn.reshape(1, HEAD_DIM) for gn in gains])


def _compress_rows(rows_ref, pe_ref, w1_ref, b1_ref, w2_ref, gain, o_ref, u_ref, n_chunk):
    for g in range(KV_GROUPS):
        for l in range(CMP_STRIDE):
            src = l * KVW + g * HEAD_DIM
            u_ref[:, l * HEAD_DIM:(l + 1) * HEAD_DIM] = rows_ref[:, src:src + HEAD_DIM]
        _compress_mlp(u_ref, pe_ref, w1_ref, b1_ref, w2_ref, gain, o_ref, g, n_chunk)


def _compress_mlp(u_ref, pe_ref, w1_ref, b1_ref, w2_ref, gain, o_ref, g, n_chunk):
    half = CMP_STRIDE * HEAD_DIM
    u = u_ref[...]
    p0 = _dot(u + pe_ref[:, 0:half], w1_ref[0:half, :])
    p1 = _dot(u + pe_ref[:, half:2 * half], w1_ref[half:2 * half, :])
    hid = b1_ref[...] + p0 + pltpu.roll(p1, n_chunk - 1, 0)
    z = _dot(jax.nn.gelu(hid), w2_ref[...])
    if gain is not None:
        z = _rms(z, gain)
    o_ref[:, g * HEAD_DIM:(g + 1) * HEAD_DIM] = z


def _compress_cols(xt_ref, pe_ref, w1_ref, b1_ref, w2_ref, gain, o_ref, xs_ref, u_ref, n_chunk):
    half = CMP_STRIDE * 2 * HEAD_DIM
    lane = lax.broadcasted_iota(jnp.int32, (1, 2 * HEAD_DIM), 1)
    for first in (0, HEAD_DIM):
        xs_ref[...] = xt_ref[first:first + 2 * HEAD_DIM, :].T
        for l in range(CMP_STRIDE):
            u_ref[:, l * 2 * HEAD_DIM:(l + 1) * 2 * HEAD_DIM] = xs_ref[pl.ds(l, n_chunk, stride=CMP_STRIDE), :]
        u = u_ref[...]
        p0 = _dot(u + pe_ref[:, 0:half], w1_ref[0:half, :])
        p1 = _dot(u + pe_ref[:, half:2 * half], w1_ref[half:2 * half, :])
        hid = b1_ref[...] + p0 + pltpu.roll(p1, n_chunk - 1, 0)
        z = _dot(jax.nn.gelu(hid), w2_ref[...])
        if gain is not None:
            sq = z * z
            ss_lo = jnp.sum(jnp.where(lane < HEAD_DIM, sq, 0.0), axis=-1, keepdims=True)
            ss_hi = jnp.sum(jnp.where(lane < HEAD_DIM, 0.0, sq), axis=-1, keepdims=True)
            ms = jnp.where(lane < HEAD_DIM, ss_lo, ss_hi) * (1.0 / HEAD_DIM)
            z = z * lax.rsqrt(ms + 1e-6) * gain
        if first == 0:
            o_ref[:, 0:2 * HEAD_DIM] = z
        else:
            o_ref[:, 2 * HEAD_DIM:3 * HEAD_DIM] = z[:, HEAD_DIM:2 * HEAD_DIM]


def _compress_prompt_kernel(k_ref, v_ref, pe_ref, w1_ref, b1_ref, w2_ref, gain_ref, ck_ref, cv_ref, u_ref,
                            *, n_chunk):
    _compress_rows(k_ref.at[0], pe_ref.at[0], w1_ref.at[0], b1_ref.at[0], w2_ref.at[0], gain_ref[...],
                   ck_ref.at[0], u_ref, n_chunk)
    _compress_rows(v_ref.at[0], pe_ref.at[1], w1_ref.at[1], b1_ref.at[1], w2_ref.at[1], None,
                   cv_ref.at[0], u_ref, n_chunk)


def _compress_weight_specs(nidx):
    zero = lambda *_: (0, 0, 0)
    del nidx
    return [pl.BlockSpec((2, 1, CMP_LEN * HEAD_DIM), zero),
            pl.BlockSpec((2, CMP_LEN * HEAD_DIM, CMP_HID), zero),
            pl.BlockSpec((2, 1, CMP_HID), zero),
            pl.BlockSpec((2, CMP_HID, HEAD_DIM), zero)]


def _compress_prompt(kc, vc, pe, w1, b1, w2, gain):
    b, s, _ = kc.shape
    n_chunk = s // CMP_STRIDE
    cw = CMP_STRIDE * KVW
    rows = pl.BlockSpec((1, n_chunk, cw), lambda i: (i, 0, 0))
    out = pl.BlockSpec((1, n_chunk, KVW), lambda i: (i, 0, 0))
    return pl.pallas_call(
        functools.partial(_compress_prompt_kernel, n_chunk=n_chunk),
        grid=(b,),
        in_specs=[rows, rows] + _compress_weight_specs(1) + [pl.BlockSpec((1, HEAD_DIM), lambda i: (0, 0))],
        out_specs=[out, out],
        out_shape=[jax.ShapeDtypeStruct((b, n_chunk, KVW), F32)] * 2,
        scratch_shapes=[pltpu.VMEM((n_chunk, CMP_STRIDE * HEAD_DIM), F32)],
        compiler_params=_cparams("parallel"), name="compress_prompt",
    )(kc.reshape(b, n_chunk, cw), vc.reshape(b, n_chunk, cw), pe.reshape(2, 1, -1), w1, b1.reshape(2, 1, -1), w2,
      gain.reshape(1, HEAD_DIM))


def _compress_sample_kernel(pt_ref, pk_ref, pv_ref, pe_ref, w1_ref, b1_ref, w2_ref, gain_ref,
                            ck_ref, cv_ref, kbuf, vbuf, xs_ref, u_ref, sem, *, n_pages):
    b = pl.program_id(0)
    n_chunk = n_pages * PAGE // CMP_STRIDE

    def k_copies(row_b):
        return [pltpu.make_async_copy(pk_ref.at[pt_ref[row_b, j]], kbuf.at[:, pl.ds(j * PAGE, PAGE)], sem.at[0])
                for j in range(n_pages)]

    v_copies = [pltpu.make_async_copy(pv_ref.at[pt_ref[b, j]], vbuf.at[:, pl.ds(j * PAGE, PAGE)], sem.at[1])
                for j in range(n_pages)]

    @pl.when(b == 0)
    def _():
        for cp in k_copies(b):
            cp.start()

    for cp in v_copies:
        cp.start()
    for cp in k_copies(b):
        cp.wait()
    _compress_cols(kbuf, pe_ref.at[0], w1_ref.at[0], b1_ref.at[0], w2_ref.at[0], gain_ref[...],
                   ck_ref.at[0], xs_ref, u_ref, n_chunk)

    @pl.when(b + 1 < pl.num_programs(0))
    def _():
        for cp in k_copies(b + 1):
            cp.start()

    for cp in v_copies:
        cp.wait()
    _compress_cols(vbuf, pe_ref.at[1], w1_ref.at[1], b1_ref.at[1], w2_ref.at[1], None,
                   cv_ref.at[0], xs_ref, u_ref, n_chunk)


def _compress_sample(page_table, pool_k, pool_v, pe, w1, b1, w2, gain):
    bd, n_pages = page_table.shape
    n_pool = pool_k.shape[0]
    n_chunk = n_pages * PAGE // CMP_STRIDE
    past = n_pages * PAGE
    pos_minor = lambda z: jnp.transpose(z, (0, 2, 3, 1)).reshape(n_pool, KVW, PAGE)
    eye2 = jnp.eye(2, dtype=F32)
    n_l, dh, hid = CMP_STRIDE, HEAD_DIM, CMP_HID
    pe2 = jnp.broadcast_to(pe.reshape(2, 2, n_l, 1, dh), (2, 2, n_l, 2, dh)).reshape(2, 1, 4 * n_l * dh)
    w1x = jnp.einsum("krldf,gh->krlgdhf", w1.reshape(2, 2, n_l, dh, hid), eye2).reshape(2, 4 * n_l * dh, 2 * hid)
    b1x = jnp.tile(b1, (1, 2)).reshape(2, 1, 2 * hid)
    w2x = jnp.einsum("kfd,gh->kgfhd", w2, eye2).reshape(2, 2 * hid, 2 * dh)
    zero3 = lambda i, pt: (0, 0, 0)
    any_spec = pl.BlockSpec(memory_space=pl.ANY)
    out = pl.BlockSpec((1, n_chunk, KVW), lambda i, pt: (i, 0, 0))
    grid_spec = pltpu.PrefetchScalarGridSpec(
        num_scalar_prefetch=1, grid=(bd,),
        in_specs=[any_spec, any_spec,
                  pl.BlockSpec(pe2.shape, zero3), pl.BlockSpec(w1x.shape, zero3),
                  pl.BlockSpec(b1x.shape, zero3), pl.BlockSpec(w2x.shape, zero3),
                  pl.BlockSpec((1, 2 * dh), lambda i, pt: (0, 0))],
        out_specs=[out, out],
        scratch_shapes=[pltpu.VMEM((KVW, past), F32), pltpu.VMEM((KVW, past), F32),
                        pltpu.VMEM((past, LANES), F32),
                        pltpu.VMEM((n_chunk, 2 * n_l * dh), F32), pltpu.SemaphoreType.DMA((2,))])
    return pl.pallas_call(
        functools.partial(_compress_sample_kernel, n_pages=n_pages),
        grid_spec=grid_spec,
        out_shape=[jax.ShapeDtypeStruct((bd, n_chunk, KVW), F32)] * 2,
        compiler_params=_cparams("arbitrary"), name="compress_sample",
    )(page_table, pos_minor(pool_k), pos_minor(pool_v),
      pe2, w1x.astype(BF16), b1x, w2x.astype(BF16), jnp.tile(gain, 2).reshape(1, 2 * dh))


def _softmax_rows(s):
    e = jnp.exp(s - jnp.max(s, axis=-1, keepdims=True))
    return e / jnp.sum(e, axis=-1, keepdims=True)


def _nsa_prompt_block(q_ref, gt_ref, ck_ref, cv_ref, ks_ref, vs_ref, kw_ref, vw_ref, o_ref,
                      *, seq, n_keys, first_block):
    tq = Q_BLOCK
    hpg = HEADS_PER_GROUP
    n_cmp = ck_ref.shape[1]
    n_sel = seq // SEL_LEN
    wlen = min(WINDOW + tq, seq)
    q0 = (pl.program_id(1) + first_block) * tq
    t1 = q0 + lax.broadcasted_iota(jnp.int32, (tq, 1), 0)
    s2c = _sel_to_cmp(n_sel, n_cmp)
    expand = _block_expand(n_sel, n_keys)
    blk = lax.broadcasted_iota(jnp.int32, (n_sel, 1), 0)
    t_row = q0 + lax.broadcasted_iota(jnp.int32, (1, tq), 1)
    cur = jnp.right_shift(t_row, 6)
    forced = (blk == 0) | (blk == cur) | (blk == cur - 1)
    valid = blk * SEL_LEN <= t_row
    dist_c = t1 - (lax.broadcasted_iota(jnp.int32, (1, n_cmp), 1) * CMP_STRIDE + (CMP_LEN - 1))
    mask_c = dist_c >= 0
    dist_cf = dist_c.astype(F32)
    dist_s = t1 - lax.broadcasted_iota(jnp.int32, (1, n_keys), 1)
    causal_s = dist_s >= 0
    dist_sf = dist_s.astype(F32)
    w_start = pl.multiple_of(jnp.clip(q0 - WINDOW, 0, seq - wlen), tq)
    dist_w = t1 - (w_start + lax.broadcasted_iota(jnp.int32, (1, wlen), 1))
    bias_w = jnp.where((dist_w >= 0) & (dist_w <= WINDOW), 0.0, NEG)
    dist_wf = dist_w.astype(F32)
    gt = gt_ref[0]
    for g in range(KV_GROUPS):
        lanes = slice(g * HEAD_DIM, (g + 1) * HEAD_DIM)
        heads = [g * hpg + j for j in range(hpg)]
        rows = [slice(j * tq, (j + 1) * tq) for j in range(hpg)]
        q4 = jnp.concatenate([q_ref[0, :, h * HEAD_DIM:(h + 1) * HEAD_DIM] for h in heads],
                             axis=0) * (HEAD_DIM ** -0.5)
        s_c = _dot_nt(q4, ck_ref[0, :, lanes])
        p_c = [_masked_softmax(s_c[rows[j]] - _alibi_slope(heads[j]) * dist_cf, mask_c) for j in range(hpg)]
        cv = cv_ref[0, :, lanes]
        o_c = [_dot(p, cv) for p in p_c]
        imps = [_dot_nt(s2c, p) for p in p_c]
        imp = imps[0]
        for j in range(1, hpg):
            imp = imp + imps[j]
        score = jnp.where(valid, jnp.where(forced, BIG, imp), -BIG)
        key_sel = _dot_tn(_topk_rows_t(score, min(SEL_TOPK, n_sel)), expand)
        bias_s = jnp.where((key_sel > 0.5) & causal_s, 0.0, NEG)
        s_s = _dot_nt(q4, ks_ref[0, 0:n_keys, lanes])
        vs = vs_ref[0, 0:n_keys, lanes]
        o_s = [_dot(_softmax_rows(s_s[rows[j]] - _alibi_slope(heads[j]) * dist_sf + bias_s), vs)
               for j in range(hpg)]
        s_w = _dot_nt(q4, kw_ref[0, pl.ds(w_start, wlen), lanes])
        vw = vw_ref[0, pl.ds(w_start, wlen), lanes]
        o_w = [_dot(_softmax_rows(s_w[rows[j]] - _alibi_slope(heads[j]) * dist_wf + bias_w), vw)
               for j in range(hpg)]
        for j, h in enumerate(heads):
            o_ref[0, :, h * HEAD_DIM:(h + 1) * HEAD_DIM] = (
                gt[:, 3 * h:3 * h + 1] * o_c[j] + gt[:, 3 * h + 1:3 * h + 2] * o_s[j]
                + gt[:, 3 * h + 2:3 * h + 3] * o_w[j])


def _nsa_prompt(q, gates, ck, cv, ks, vs, kw, vw):
    b, s, _ = q.shape
    n_cmp = ck.shape[1]
    key_step = min(NSA_KEY_STEP, s)
    per_class = key_step // Q_BLOCK
    full = lambda n: pl.BlockSpec((1, n, KVW), lambda i, j: (i, 0, 0))
    outs = []
    for c in range(s // key_step):
        n_keys = (c + 1) * key_step
        first = c * per_class
        qspec = lambda w, first=first: pl.BlockSpec((1, Q_BLOCK, w), lambda i, j: (i, j + first, 0))
        outs.append(pl.pallas_call(
            functools.partial(_nsa_prompt_block, seq=s, n_keys=n_keys, first_block=first),
            grid=(b, per_class),
            in_specs=[qspec(QA), qspec(LANES), full(n_cmp), full(n_cmp), full(n_keys), full(n_keys), full(s), full(s)],
            out_specs=pl.BlockSpec((1, Q_BLOCK, QA), lambda i, j: (i, j, 0)),
            out_shape=jax.ShapeDtypeStruct((b, key_step, QA), F32),
            compiler_params=_cparams("parallel", "parallel"), name=f"nsa_prompt_{n_keys}",
        )(q, gates, ck, cv, ks, vs, kw, vw))
    return jnp.concatenate(outs, axis=1)


def _gather_pages_t(pt_ref, b, pool_ref, dst_ref, sem, n_pages):
    return [pltpu.make_async_copy(pool_ref.at[pt_ref[b, j]], dst_ref.at[:, :, pl.ds(j * PAGE, PAGE)], sem)
            for j in range(n_pages)]


def _nsa_sample_kernel(pt_ref, q_ref, gt_ref, ck_ref, cv_ref, ksn_ref, vsn_ref, kwn_ref, vwn_ref,
                       pks_ref, pvs_ref, wk_ref, wv_ref, o_ref, owk_ref, owv_ref, kbuf, vbuf, sem,
                       *, n_pages):
    b = pl.program_id(0)
    hpg = HEADS_PER_GROUP
    past = n_pages * PAGE
    n_cmp = ck_ref.shape[1]
    n_sel_past = past // SEL_LEN
    n_sel = n_sel_past + 1
    wb = wk_ref.shape[3]
    sel_lanes = 2 * LANES
    slot = lax.rem(b, 2)

    def page_copies(row_b, to_slot):
        return (_gather_pages_t(pt_ref, row_b, pks_ref, kbuf.at[to_slot], sem.at[0, to_slot], n_pages)
                + _gather_pages_t(pt_ref, row_b, pvs_ref, vbuf.at[to_slot], sem.at[1, to_slot], n_pages))

    @pl.when(b == 0)
    def _():
        for cp in page_copies(b, slot):
            cp.start()

    @pl.when(b + 1 < pl.num_programs(0))
    def _():
        for cp in page_copies(b + 1, 1 - slot):
            cp.start()

    row = lax.broadcasted_iota(jnp.int32, (8, 1), 0)
    n = HEAD_DIM
    eye = (lax.broadcasted_iota(jnp.int32, (n, n), 0) == lax.broadcasted_iota(jnp.int32, (n, n), 1)).astype(F32)
    gt = gt_ref[0]
    c_end = lax.broadcasted_iota(jnp.int32, (1, n_cmp), 1) * CMP_STRIDE + (CMP_LEN - 1)
    dist_c = past - c_end
    blk = lax.broadcasted_iota(jnp.int32, (1, sel_lanes), 1)
    cur = past // SEL_LEN
    forced = (blk == 0) | (blk == cur) | (blk == cur - 1)
    valid = blk * SEL_LEN <= past
    c2s = _cmp_to_sel(n_cmp, sel_lanes)

    def q_rows(g):
        q4 = jnp.concatenate(
            [q_ref[0, :, (g * hpg + j) * HEAD_DIM:(g * hpg + j + 1) * HEAD_DIM] for j in range(hpg)]
            + [jnp.zeros((8 - hpg, HEAD_DIM), F32)], axis=0) * (HEAD_DIM ** -0.5)
        slope = jnp.full((8, 1), _alibi_slope(g * hpg + hpg - 1), F32)
        for j in range(hpg - 2, -1, -1):
            slope = jnp.where(row < j + 1, _alibi_slope(g * hpg + j), slope)
        return q4, slope

    o_cs, imps = [], []
    for g in range(KV_GROUPS):
        lanes = slice(g * HEAD_DIM, (g + 1) * HEAD_DIM)
        q4, slope = q_rows(g)
        p_c = _masked_softmax(_dot_nt(q4, ck_ref[0, :, lanes]) - slope * dist_c.astype(F32), dist_c >= 0)
        o_cs.append(_dot(p_c, cv_ref[0, :, lanes]))
        imps.append(jnp.sum(jnp.where(row < hpg, _dot(p_c, c2s), 0.0), axis=0, keepdims=True))
    imp = jnp.concatenate(imps + [jnp.zeros((8 - KV_GROUPS, sel_lanes), F32)], axis=0)
    score = jnp.where(valid, jnp.where(forced, BIG, imp), -BIG)
    score = jnp.where(blk < n_sel, score, -jnp.inf)
    sel = _topk_mask(score, min(SEL_TOPK, n_sel), n_sel)
    key_sel = _dot(sel[:, 0:n_sel_past], _block_expand(n_sel_past, past))
    sel_new = sel[:, n_sel_past:n_sel_past + 1]

    pos_s = lax.broadcasted_iota(jnp.int32, (1, past), 1)
    dist_s = (past - pos_s).astype(F32)
    lane_w = lax.broadcasted_iota(jnp.int32, (1, wb), 1)
    pos_w = past - wb + lane_w
    dist_w = past - pos_w
    mask_w = (dist_w >= 0) & (dist_w <= WINDOW) & (pos_w >= 0)
    for cp in page_copies(b, slot):
        cp.wait()
    for g in range(KV_GROUPS):
        lanes = slice(g * HEAD_DIM, (g + 1) * HEAD_DIM)
        q4, slope = q_rows(g)
        q4r = _bf16_round(q4)
        s_p = jnp.where(key_sel[g:g + 1] > 0.5, _dot(q4, kbuf[slot, g]) - slope * dist_s, NEG)
        new_ok = sel_new[g:g + 1] > 0.5
        s_n = jnp.where(new_ok, jnp.sum(q4r * _bf16_round(ksn_ref[0, :, lanes]), axis=-1, keepdims=True), NEG)
        m = jnp.maximum(jnp.max(s_p, axis=-1, keepdims=True), s_n)
        e_p = jnp.where(key_sel[g:g + 1] > 0.5, jnp.exp(s_p - m), 0.0)
        e_n = jnp.where(new_ok, jnp.exp(s_n - m), 0.0)
        den = jnp.maximum(jnp.sum(e_p, axis=-1, keepdims=True) + e_n, 1e-30)
        o_s = _dot_nt(e_p / den, vbuf[slot, g]) + _bf16_round(e_n / den) * _bf16_round(vsn_ref[0, :, lanes])
        s_p = jnp.where(mask_w, _dot(q4, wk_ref[0, g]) - slope * dist_w.astype(F32), NEG)
        s_n = jnp.sum(q4r * _bf16_round(kwn_ref[0, :, lanes]), axis=-1, keepdims=True)
        m = jnp.maximum(jnp.max(s_p, axis=-1, keepdims=True), s_n)
        e_p = jnp.where(mask_w, jnp.exp(s_p - m), 0.0)
        e_n = jnp.exp(s_n - m)
        den = jnp.maximum(jnp.sum(e_p, axis=-1, keepdims=True) + e_n, 1e-30)
        o_w = _dot_nt(e_p / den, wv_ref[0, g]) + _bf16_round(e_n / den) * _bf16_round(vwn_ref[0, :, lanes])
        for j in range(hpg):
            h = g * hpg + j
            o_ref[0, :, h * HEAD_DIM:(h + 1) * HEAD_DIM] = (
                gt[:, 3 * h:3 * h + 1] * o_cs[g][j:j + 1] + gt[:, 3 * h + 1:3 * h + 2] * o_s[j:j + 1]
                + gt[:, 3 * h + 2:3 * h + 3] * o_w[j:j + 1])
        k_col = jnp.sum(eye * kwn_ref[0, :, lanes], axis=1, keepdims=True)
        v_col = jnp.sum(eye * vwn_ref[0, :, lanes], axis=1, keepdims=True)
        owk_ref[0, g] = jnp.where(lane_w == wb - 1, k_col, pltpu.roll(wk_ref[0, g], wb - 1, 1))
        owv_ref[0, g] = jnp.where(lane_w == wb - 1, v_col, pltpu.roll(wv_ref[0, g], wb - 1, 1))


def _nsa_sample(page_table, q, gates, ck, cv, ks_new, vs_new, kw_new, vw_new, pool_ks, pool_vs, win_k, win_v):
    bd, n_pages = page_table.shape
    n_cmp = ck.shape[1]
    wb = win_k.shape[1]
    one = lambda w: pl.BlockSpec((1, 1, w), lambda i, pt: (i, 0, 0))
    rows = lambda m: pl.BlockSpec((1, m, KVW), lambda i, pt: (i, 0, 0))
    win = pl.BlockSpec((1, KV_GROUPS, HEAD_DIM, wb), lambda i, pt: (i, 0, 0, 0))
    any_spec = pl.BlockSpec(memory_space=pl.ANY)
    grid_spec = pltpu.PrefetchScalarGridSpec(
        num_scalar_prefetch=1, grid=(bd,),
        in_specs=[one(QA), one(LANES), rows(n_cmp), rows(n_cmp), one(KVW), one(KVW), one(KVW), one(KVW),
                  any_spec, any_spec, win, win],
        out_specs=[one(QA), win, win],
        scratch_shapes=[pltpu.VMEM((2, KV_GROUPS, HEAD_DIM, n_pages * PAGE), F32),
                        pltpu.VMEM((2, KV_GROUPS, HEAD_DIM, n_pages * PAGE), F32),
                        pltpu.SemaphoreType.DMA((2, 2))])
    r3 = lambda z: z.reshape(bd, 1, -1)
    pos_minor = lambda z: jnp.transpose(z, (0, 2, 3, 1))
    win_shape = jax.ShapeDtypeStruct((bd, KV_GROUPS, HEAD_DIM, wb), F32)
    mix, nwk, nwv = pl.pallas_call(
        functools.partial(_nsa_sample_kernel, n_pages=n_pages),
        grid_spec=grid_spec,
        out_shape=[jax.ShapeDtypeStruct((bd, 1, QA), F32), win_shape, win_shape],
        compiler_params=_cparams("arbitrary"), name="nsa_sample",
    )(page_table, r3(q), r3(gates), ck, cv, r3(ks_new), r3(vs_new), r3(kw_new), r3(vw_new),
      pos_minor(pool_ks), pos_minor(pool_vs), pos_minor(win_k), pos_minor(win_v))
    back = lambda z: jnp.transpose(z, (0, 3, 1, 2))
    return mix, back(nwk), back(nwv)


def _mem_attn_kernel(q_ref, k_ref, v_ref, gain_ref, o_ref):
    tq = q_ref.shape[1]
    pad = max(8 - tq, 0)
    for h in range(MEM_HEADS):
        lanes = slice(h * HEAD_DIM, (h + 1) * HEAD_DIM)
        q = _rms(q_ref[0, :, lanes], gain_ref[...]) * (HEAD_DIM ** -0.5)
        if pad:
            q = jnp.concatenate([q, jnp.zeros((pad, HEAD_DIM), F32)], axis=0)
        s = _dot_nt(q, k_ref[0, :, lanes])
        e = jnp.exp(s - jnp.max(s, axis=-1, keepdims=True))
        p = e / jnp.sum(e, axis=-1, keepdims=True)
        o_ref[0, :, lanes] = _dot(p, v_ref[0, :, lanes])[0:tq]


def _mem_attn(mq, km, vm, gain, tq):
    b, t, _ = mq.shape
    m = km.shape[1]
    tq = min(tq, t)
    qspec = pl.BlockSpec((1, tq, MEMQ), lambda i, j: (i, j, 0))
    kspec = pl.BlockSpec((1, m, MEMQ), lambda i, j: (i, 0, 0))
    return pl.pallas_call(
        _mem_attn_kernel, grid=(b, t // tq),
        in_specs=[qspec, kspec, kspec, pl.BlockSpec((1, HEAD_DIM), lambda i, j: (0, 0))],
        out_specs=qspec, out_shape=jax.ShapeDtypeStruct((b, t, MEMQ), F32),
        compiler_params=_cparams("parallel", "parallel"), name="mem_attn",
    )(mq, km, vm, gain.reshape(1, HEAD_DIM))


def _outproj_router_kernel(x_ref, mix_ref, mem_ref, wo_ref, g_ref, wc_ref, bc_ref, wf_ref, bf_ref,
                           xo_ref, hn_ref, cw_ref):
    x = x_ref[...] + (_dot(mix_ref[...], wo_ref[0:QA, :]) + _dot(mem_ref[...], wo_ref[QA:QA + MEMQ, :]))
    xo_ref[...] = x
    hn = _rms(x, g_ref[...])
    hn_ref[...] = hn
    hn = hn.astype(BF16)
    lane = lax.broadcasted_iota(jnp.int32, (1, LANES), 1).astype(F32)
    lg = jnp.where(lane < N_GROUPS, _dot(hn, wc_ref[...]) + bc_ref[...], -jnp.inf)
    m = jnp.max(lg, axis=-1, keepdims=True)
    grp = jnp.min(jnp.where(lg == m, lane, 1e9), axis=-1, keepdims=True)
    p_grp = 1.0 / jnp.sum(jnp.exp(lg - m), axis=-1, keepdims=True)
    in_grp = (lane >= grp * E_PER_GROUP) & (lane < (grp + 1.0) * E_PER_GROUP)
    lf = jnp.where(in_grp, _dot(hn, wf_ref[...]) + bf_ref[...], -jnp.inf)
    v1 = jnp.max(lf, axis=-1, keepdims=True)
    i1 = jnp.min(jnp.where(lf == v1, lane, 1e9), axis=-1, keepdims=True)
    lf2 = jnp.where(lane == i1, -jnp.inf, lf)
    v2 = jnp.max(lf2, axis=-1, keepdims=True)
    i2 = jnp.min(jnp.where(lf2 == v2, lane, 1e9), axis=-1, keepdims=True)
    e2 = jnp.exp(v2 - v1)
    den = 1.0 + e2
    cw_ref[...] = jnp.where(lane == i1, p_grp / den, 0.0) + jnp.where(lane == i2, p_grp * (e2 / den), 0.0)


def _outproj_router(x, mix, mem, w_out, g_ffn, w_coarse, b_coarse, w_fine, b_fine, tm):
    n, d = x.shape
    tm = min(tm, n)
    pad_w = lambda w: jnp.pad(w, ((0, 0), (0, LANES - w.shape[1])))
    pad_b = lambda v: jnp.pad(v, (0, LANES - v.shape[0])).reshape(1, LANES)
    row = lambda w: pl.BlockSpec((tm, w), lambda i: (i, 0))
    const = lambda r, c: pl.BlockSpec((r, c), lambda i: (0, 0))
    return pl.pallas_call(
        _outproj_router_kernel, grid=(n // tm,),
        in_specs=[row(d), row(QA), row(MEMQ), const(QA + MEMQ, d), const(1, d), const(d, LANES),
                  const(1, LANES), const(d, LANES), const(1, LANES)],
        out_specs=[row(d), row(d), row(LANES)],
        out_shape=[jax.ShapeDtypeStruct((n, d), F32), jax.ShapeDtypeStruct((n, d), F32),
                   jax.ShapeDtypeStruct((n, LANES), F32)],
        compiler_params=_cparams("parallel"), name="outproj_router",
    )(x, mix, mem, w_out.astype(BF16), g_ffn.reshape(1, d), pad_w(w_coarse).astype(BF16), pad_b(b_coarse),
      pad_w(w_fine).astype(BF16), pad_b(b_fine))


def _moe_kernel(x_ref, hn_ref, cw_ref, wg_ref, wu_ref, wd_ref, o_ref):
    e = pl.program_id(1)

    @pl.when(e == 0)
    def _():
        o_ref[...] = jnp.zeros_like(o_ref)

    hb = hn_ref[...].astype(BF16)
    gate = _dot(hb, wg_ref[0])
    up = _dot(hb, wu_ref[0])
    lane = lax.broadcasted_iota(jnp.int32, (1, LANES), 1)
    c = jnp.sum(jnp.where(lane == e, cw_ref[...], 0.0), axis=-1, keepdims=True)
    o_ref[...] += _dot(gate * jax.nn.sigmoid(gate) * up, wd_ref[0]) * c

    @pl.when(e == pl.num_programs(1) - 1)
    def _():
        o_ref[...] = x_ref[...] + o_ref[...]


def _moe(x, hn, cw, w_gate, w_up, w_down, layer, tm):
    n, d = x.shape
    tm = min(tm, n)
    _, n_exp, _, d_exp = w_gate.shape
    row = lambda w: pl.BlockSpec((tm, w), lambda i, e: (i, 0))
    return pl.pallas_call(
        _moe_kernel, grid=(n // tm, n_exp),
        in_specs=[row(d), row(d), row(LANES),
                  pl.BlockSpec((None, 1, d, d_exp), lambda i, e: (layer, e, 0, 0)),
                  pl.BlockSpec((None, 1, d, d_exp), lambda i, e: (layer, e, 0, 0)),
                  pl.BlockSpec((None, 1, d_exp, d), lambda i, e: (layer, e, 0, 0))],
        out_specs=row(d), out_shape=jax.ShapeDtypeStruct((n, d), F32),
        compiler_params=_cparams("parallel", "arbitrary"), name="moe",
    )(x, hn, cw, w_gate, w_up, w_down)


def _moe_plan_kernel(cw_ref, posa_ref, posb_ref, te_ref, nu_ref, cnt_ref, run_ref, start_ref, *, row_tile, n_rows):
    phase, i = pl.program_id(0), pl.program_id(1)
    tm = cw_ref.shape[0]
    cwt = cw_ref[...].T
    mask = cwt != 0.0
    maskf = mask.astype(F32)
    tile_cnt = jnp.sum(maskf, axis=1, keepdims=True)

    @pl.when((phase == 0) & (i == 0))
    def _():
        cnt_ref[...] = jnp.zeros_like(cnt_ref)

    @pl.when(phase == 0)
    def _():
        cnt_ref[...] += tile_cnt

    @pl.when((phase == 1) & (i == 0))
    def _():
        padded = jnp.floor((cnt_ref[...] + (row_tile - 1)) * (1.0 / row_tile)) * row_tile
        lower = (lax.broadcasted_iota(jnp.int32, (LANES, LANES), 0)
                 > lax.broadcasted_iota(jnp.int32, (LANES, LANES), 1)).astype(F32)
        start = _dot(lower, jnp.broadcast_to(padded, (LANES, LANES)), exact=True)[:, 0:1]
        start_ref[...] = start
        run_ref[...] = jnp.zeros_like(run_ref)
        tile_lo = lax.broadcasted_iota(jnp.int32, (1, 2 * LANES), 1).astype(F32) * row_tile
        n_done = jnp.sum(jnp.where(start + padded <= tile_lo, 1.0, 0.0), axis=0, keepdims=True)
        te_ref[...] = jnp.minimum(n_done, N_EXPERTS - 1.0).astype(jnp.int32)
        total = jnp.sum(padded, axis=0, keepdims=True)
        nu_ref[...] = jnp.broadcast_to(total * (1.0 / row_tile), (1, LANES)).astype(jnp.int32)

    @pl.when(phase == 1)
    def _():
        before = (lax.broadcasted_iota(jnp.int32, (tm, tm), 0)
                  < lax.broadcasted_iota(jnp.int32, (tm, tm), 1)).astype(BF16)
        pos = start_ref[...] + run_ref[...] + _dot(maskf, before)
        posa = jnp.min(jnp.where(mask, pos, 3e38), axis=0, keepdims=True)
        posb = jnp.max(jnp.where(mask, pos, -1.0), axis=0, keepdims=True)
        posb = jnp.where(jnp.sum(maskf, axis=0, keepdims=True) > 1.5, posb, n_rows - 1.0)
        posa_ref[0] = posa.astype(jnp.int32)
        posb_ref[0] = posb.astype(jnp.int32)
        run_ref[...] += tile_cnt


def _moe_row_copies(pos_refs, make_copy, tm):
    def issue(t, carry):
        for pos_ref in pos_refs:
            make_copy(pos_ref, t).start()
        return carry

    def drain(t, carry):
        for pos_ref in pos_refs:
            make_copy(pos_ref, 0).wait()
        return carry

    lax.fori_loop(0, tm, issue, 0, unroll=8)
    lax.fori_loop(0, tm, drain, 0, unroll=8)


def _moe_dispatch_kernel(posa_ref, posb_ref, hn_ref, xs_zero_ref, xs_ref, sem):
    del xs_zero_ref

    def make_copy(pos_ref, t):
        return pltpu.make_async_copy(hn_ref.at[pl.ds(t, 1)], xs_ref.at[pl.ds(pos_ref[0, 0, t], 1)], sem)

    _moe_row_copies((posa_ref, posb_ref), make_copy, hn_ref.shape[0])


def _moe_experts_kernel(te_ref, nu_ref, xs_ref, wg_ref, wu_ref, wd_ref, ys_ref):
    del te_ref
    used = pl.program_id(0) < nu_ref[0]

    @pl.when(used)
    def _():
        hb = xs_ref[...].astype(BF16)
        gate = _dot(hb, wg_ref[0])
        up = _dot(hb, wu_ref[0])
        ys_ref[...] = _dot(gate * jax.nn.sigmoid(gate) * up, wd_ref[0])

    @pl.when(jnp.logical_not(used))
    def _():
        ys_ref[...] = jnp.zeros_like(ys_ref)


def _moe_combine_kernel(posa_ref, posb_ref, x_ref, cw_ref, ys_ref, o_ref, ya_ref, yb_ref, sem):
    def make_copy(pos_ref, t):
        dst = ya_ref if pos_ref is posa_ref else yb_ref
        return pltpu.make_async_copy(ys_ref.at[pl.ds(pos_ref[0, 0, t], 1)], dst.at[pl.ds(t, 1)], sem)

    _moe_row_copies((posa_ref, posb_ref), make_copy, x_ref.shape[0])
    cw = cw_ref[...]
    lane = lax.broadcasted_iota(jnp.int32, (1, LANES), 1).astype(F32)
    routed = cw != 0.0
    ea = jnp.min(jnp.where(routed, lane, 1e9), axis=-1, keepdims=True)
    eb = jnp.max(jnp.where(routed, lane, -1.0), axis=-1, keepdims=True)
    wa = jnp.sum(jnp.where(lane == ea, cw, 0.0), axis=-1, keepdims=True)
    wb = jnp.sum(jnp.where((lane == eb) & (eb != ea), cw, 0.0), axis=-1, keepdims=True)
    o_ref[...] = x_ref[...] + (wa * ya_ref[...] + wb * yb_ref[...])


def _moe_routed(x, hn, cw, w_gate, w_up, w_down, layer):
    n, d = x.shape
    _, n_exp, _, d_exp = w_gate.shape
    tm, tr = MOE_TOKEN_TILE, MOE_ROW_TILE
    n_tiles = n // tm
    n_row_tiles = (2 * n + n_exp * (tr - 1)) // tr + 1
    n_rows = n_row_tiles * tr
    assert n % tm == 0 and n_row_tiles <= 2 * LANES
    pos_shape = jax.ShapeDtypeStruct((n_tiles, 1, tm), jnp.int32)
    posa, posb, tile_expert, n_used = pl.pallas_call(
        functools.partial(_moe_plan_kernel, row_tile=tr, n_rows=n_rows),
        grid=(2, n_tiles),
        in_specs=[pl.BlockSpec((tm, LANES), lambda p, i: (i, 0))],
        out_specs=[pl.BlockSpec((1, 1, tm), lambda p, i: (i * p, 0, 0)),
                   pl.BlockSpec((1, 1, tm), lambda p, i: (i * p, 0, 0)),
                   pl.BlockSpec((1, 2 * LANES), lambda p, i: (0, 0)),
                   pl.BlockSpec((1, LANES), lambda p, i: (0, 0))],
        out_shape=[pos_shape, pos_shape, jax.ShapeDtypeStruct((1, 2 * LANES), jnp.int32),
                   jax.ShapeDtypeStruct((1, LANES), jnp.int32)],
        scratch_shapes=[pltpu.VMEM((LANES, 1), F32)] * 3,
        compiler_params=_cparams("arbitrary", "arbitrary"), name="moe_plan",
    )(cw)
    pos_spec = pl.BlockSpec((1, 1, tm), lambda i: (i, 0, 0), memory_space=pltpu.SMEM)
    any_spec = pl.BlockSpec(memory_space=pl.ANY)
    row = lambda w: pl.BlockSpec((tm, w), lambda i: (i, 0))
    xs = pl.pallas_call(
        _moe_dispatch_kernel, grid=(n_tiles,),
        in_specs=[pos_spec, pos_spec, row(d), any_spec],
        out_specs=any_spec, out_shape=jax.ShapeDtypeStruct((n_rows, d), F32),
        scratch_shapes=[pltpu.SemaphoreType.DMA(())],
        input_output_aliases={3: 0},
        compiler_params=_cparams("arbitrary"), name="moe_dispatch",
    )(posa, posb, hn, jnp.zeros((n_rows, d), F32))
    wspec = lambda a, c: pl.BlockSpec((None, 1, a, c), lambda i, te, nu: (layer, te[i], 0, 0))
    ys = pl.pallas_call(
        _moe_experts_kernel,
        grid_spec=pltpu.PrefetchScalarGridSpec(
            num_scalar_prefetch=2, grid=(n_row_tiles,),
            in_specs=[pl.BlockSpec((tr, d), lambda i, te, nu: (i, 0)), wspec(d, d_exp), wspec(d, d_exp),
                      wspec(d_exp, d)],
            out_specs=pl.BlockSpec((tr, d), lambda i, te, nu: (i, 0))),
        out_shape=jax.ShapeDtypeStruct((n_rows, d), F32),
        compiler_params=_cparams("arbitrary"), name="moe_experts",
    )(tile_expert.reshape(-1), n_used.reshape(-1), xs, w_gate, w_up, w_down)
    return pl.pallas_call(
        _moe_combine_kernel, grid=(n_tiles,),
        in_specs=[pos_spec, pos_spec, row(d), row(LANES), any_spec],
        out_specs=row(d), out_shape=jax.ShapeDtypeStruct((n, d), F32),
        scratch_shapes=[pltpu.VMEM((tm, d), F32), pltpu.VMEM((tm, d), F32), pltpu.SemaphoreType.DMA(())],
        compiler_params=_cparams("arbitrary"), name="moe_combine",
    )(posa, posb, x, cw, ys)


def _rwkv_prep_body(p, prev, mu_ref, w0_ref, wup_ref, a0_ref, aup_ref, gup_ref, kk_ref, ka_ref, outs):
    r_ref, ld_ref, k_ref, v_ref, kkn_ref, a_ref, g_ref = outs
    x = p + mu_ref[...] * (prev - p)
    c0, c1, c2, c3, c4 = 768, 1536, 2304, 2368, 2432
    r, k, v = x[:, 0:c0], x[:, c0:c1], x[:, c1:c2]
    xw, xa, xg = x[:, c2:c3], x[:, c3:c4], x[:, c4:SHIFT_W]
    z = -(w0_ref[...] + _dot(jnp.tanh(xw), wup_ref[...]))
    softplus = jnp.maximum(z, 0.0) + jnp.log(1.0 + jnp.exp(-jnp.abs(z)))
    w = -softplus - 0.5
    a = jax.nn.sigmoid(a0_ref[...] + _dot(xa, aup_ref[...]))
    r_ref[...] = r
    ld_ref[...] = -jnp.exp(w)
    k_ref[...] = k * (1.0 + (a - 1.0) * ka_ref[...])
    v_ref[...] = v
    a_ref[...] = a
    g_ref[...] = _dot(jax.nn.sigmoid(xg), gup_ref[...])
    kk = k * kk_ref[...]
    for h in range(TOK_HEADS):
        lanes = slice(h * HEAD_DIM, (h + 1) * HEAD_DIM)
        seg = kk[:, lanes]
        nrm = jnp.sqrt(jnp.sum(seg * seg, axis=-1, keepdims=True))
        kkn_ref[:, lanes] = seg / jnp.maximum(nrm, 1e-12)


def _rwkv_prep_prompt_kernel(p_ref, mu_ref, w0_ref, wup_ref, a0_ref, aup_ref, gup_ref, kk_ref, ka_ref,
                             *rest):
    outs, carry = rest[:7], rest[7]
    tm = p_ref.shape[1]

    @pl.when(pl.program_id(1) == 0)
    def _():
        carry[...] = jnp.zeros_like(carry)

    p = p_ref[0]
    first = lax.broadcasted_iota(jnp.int32, (tm, 1), 0) == 0
    prev = jnp.where(first, carry[...], pltpu.roll(p, 1, 0))
    carry[...] = p[tm - 1:tm]
    _rwkv_prep_body(p, prev, mu_ref, w0_ref, wup_ref, a0_ref, aup_ref, gup_ref, kk_ref, ka_ref,
                    [o.at[0] for o in outs])


def _rwkv_prep_sample_kernel(p_ref, prev_ref, mu_ref, w0_ref, wup_ref, a0_ref, aup_ref, gup_ref, kk_ref,
                             ka_ref, *outs):
    _rwkv_prep_body(p_ref[...], prev_ref[...], mu_ref, w0_ref, wup_ref, a0_ref, aup_ref, gup_ref, kk_ref,
                    ka_ref, outs)


def _rwkv_weight_args(mu, w0, w_up, a0, a_up, g_up, k_k, k_a):
    row = lambda v: v.reshape(1, -1)
    return (row(mu), row(w0), w_up, row(a0), a_up, g_up, row(k_k), row(k_a))


def _rwkv_prep_prompt(pr, wargs, tm):
    b, s, _ = pr.shape
    tm = min(tm, s)
    const = lambda a: pl.BlockSpec(a.shape, lambda i, j: (0, 0))
    ospec = pl.BlockSpec((1, tm, QA), lambda i, j: (i, j, 0))
    return pl.pallas_call(
        _rwkv_prep_prompt_kernel, grid=(b, s // tm),
        in_specs=[pl.BlockSpec((1, tm, SHIFT_W), lambda i, j: (i, j, 0))] + [const(a) for a in wargs],
        out_specs=[ospec] * 7, out_shape=[jax.ShapeDtypeStruct((b, s, QA), F32)] * 7,
        scratch_shapes=[pltpu.VMEM((1, SHIFT_W), F32)],
        compiler_params=_cparams("parallel", "arbitrary"), name="rwkv_prep_prompt",
    )(pr, *wargs)


def _rwkv_prep_sample(pr, prev, wargs):
    n = pr.shape[0]
    full = lambda a: pl.BlockSpec(a.shape, lambda i: (0, 0))
    return pl.pallas_call(
        _rwkv_prep_sample_kernel, grid=(1,),
        in_specs=[full(pr), full(prev)] + [full(a) for a in wargs],
        out_specs=[pl.BlockSpec((n, QA), lambda i: (0, 0))] * 7,
        out_shape=[jax.ShapeDtypeStruct((n, QA), F32)] * 7,
        compiler_params=_cparams("arbitrary"), name="rwkv_prep_sample",
    )(pr, prev, *wargs)


def _rwkv_finish(y, r, k, v, g, rk, lnw, lnb):
    m = jnp.mean(y, axis=-1, keepdims=True)
    var = jnp.mean(jnp.square(y - m), axis=-1, keepdims=True)
    yn = (y - m) * lax.rsqrt(var + RWKV_GN_EPS) * lnw + lnb
    bonus = jnp.sum(r * k * rk, axis=-1, keepdims=True) * v
    return (yn + bonus) * g


def _rwkv_chunk_kernel(r_ref, ld_ref, k_ref, v_ref, kk_ref, a_ref, g_ref, rk_ref, lnw_ref, lnb_ref,
                       o_ref, st_ref, z_ref):
    c = r_ref.shape[1]
    ci = pl.program_id(1)

    @pl.when(ci == 0)
    def _():
        z_ref[...] = jnp.zeros_like(z_ref)

    ri = lax.broadcasted_iota(jnp.int32, (c, c), 0)
    cj = lax.broadcasted_iota(jnp.int32, (c, c), 1)
    incl = ri >= cj
    strict = ri > cj
    n = HEAD_DIM
    eye = (lax.broadcasted_iota(jnp.int32, (n, n), 0) == lax.broadcasted_iota(jnp.int32, (n, n), 1)).astype(F32)
    ex = RWKV_CHUNK_EXACT
    hl = [slice(h * n, (h + 1) * n) for h in range(TOK_HEADS)]
    prep = []
    for bi in range(r_ref.shape[0]):
        ld, r_all, k_all, kk_all = ld_ref[bi], r_ref[bi], k_ref[bi], kk_ref[bi]
        lc = _dot(incl.astype(F32), ld, exact=True)
        l_end = lc[c - 1:c]
        b_all = kk_all * a_ref[bi]
        e_neg = jnp.exp(-lc)
        e_rem = jnp.exp(l_end - lc)
        prep.append(dict(
            r=r_all, k=k_all, v=v_ref[bi],
            at=-kk_all * jnp.exp(lc - ld),
            rt=r_all * jnp.exp(lc),
            bt=b_all * e_neg, kt=k_all * e_neg,
            bh=b_all * e_rem, kh=k_all * e_rem,
            p_end=jnp.exp(l_end)))
    units = [(bi, h) for bi in range(r_ref.shape[0]) for h in range(TOK_HEADS)]
    col = lambda name, u: prep[u[0]][name][:, hl[u[1]]]
    v = [col("v", u) for u in units]
    big = [_dot_nt(jnp.concatenate([col("at", u), col("rt", u)], axis=0),
                   jnp.concatenate([col("bt", u), col("kt", u)], axis=0), ex) for u in units]
    a_rbk = [jnp.concatenate([jnp.where(incl, m[c:2 * c, 0:c], 0.0), jnp.where(incl, m[c:2 * c, c:2 * c], 0.0)],
                             axis=1) for m in big]
    akv = [_dot(jnp.where(strict, m[0:c, c:2 * c], 0.0), vv, ex) for m, vv in zip(big, v)]
    x = [jnp.concatenate([col("at", u), t], axis=1) for u, t in zip(units, akv)]
    npow = [jnp.where(strict, m[0:c, 0:c], 0.0) for m in big]
    x = [xx + _dot(m, xx, ex) for m, xx in zip(npow, x)]
    steps = 1
    while 2 * steps < c:
        npow = [_dot(m, m, ex) for m in npow]
        x = [xx + _dot(m, xx, ex) for m, xx in zip(npow, x)]
        steps *= 2
    zeros = jnp.zeros((c, n), F32)
    wv = [jnp.concatenate([xx, jnp.concatenate([zeros, vv], axis=1)], axis=0) for xx, vv in zip(x, v)]
    rq_y0 = [_dot(m, w, ex) for m, w in zip(a_rbk, wv)]
    m_n = [_dot_tn(jnp.concatenate([col("bh", u), col("kh", u)], axis=0), w, ex)
           for u, w in zip(units, wv)]
    yz = [_dot(jnp.concatenate([col("rt", u) + rq_y0[i][:, 0:n],
                                m_n[i][:, 0:n] + eye * col("p_end", u)], axis=0), z_ref[u[0], u[1]], ex)
          for i, u in enumerate(units)]
    for i, (bi, h) in enumerate(units):
        z_ref[bi, h] = yz[i][c:c + n] + m_n[i][:, n:2 * n]
        y = yz[i][0:c] + rq_y0[i][:, n:2 * n]
        o_ref[bi, :, hl[h]] = _rwkv_finish(y, col("r", (bi, h)), col("k", (bi, h)), v[i], g_ref[bi, :, hl[h]],
                                           rk_ref[:, hl[h]], lnw_ref[:, hl[h]], lnb_ref[:, hl[h]])

    @pl.when(ci == pl.num_programs(1) - 1)
    def _():
        for bi, h in units:
            st_ref[bi, h] = z_ref[bi, h].T


def _rwkv_chunk(r, ld, k, v, kk, a, g, r_k, ln_w, ln_b):
    b, s, _ = r.shape
    c = RWKV_CHUNK
    nb = RWKV_ROWS_PER_STEP if b % RWKV_ROWS_PER_STEP == 0 else 1
    tok = pl.BlockSpec((nb, c, QA), lambda i, j: (i, j, 0))
    const = pl.BlockSpec((1, QA), lambda i, j: (0, 0))
    return pl.pallas_call(
        _rwkv_chunk_kernel, grid=(b // nb, s // c),
        in_specs=[tok] * 7 + [const] * 3,
        out_specs=[tok, pl.BlockSpec((nb, TOK_HEADS, HEAD_DIM, HEAD_DIM), lambda i, j: (i, 0, 0, 0))],
        out_shape=[jax.ShapeDtypeStruct((b, s, QA), F32),
                   jax.ShapeDtypeStruct((b, TOK_HEADS, HEAD_DIM, HEAD_DIM), F32)],
        scratch_shapes=[pltpu.VMEM((nb, TOK_HEADS, HEAD_DIM, HEAD_DIM), F32)],
        compiler_params=_cparams("parallel", "arbitrary"), name="rwkv_chunk",
    )(r, ld, k, v, kk, a, g, r_k.reshape(1, QA), ln_w.reshape(1, QA), ln_b.reshape(1, QA))


def _rwkv_step_kernel(r_ref, ld_ref, k_ref, v_ref, kk_ref, a_ref, g_ref, rk_ref, lnw_ref, lnb_ref, s_ref,
                      o_ref, so_ref):
    n = HEAD_DIM
    eye = (lax.broadcasted_iota(jnp.int32, (n, n), 0) == lax.broadcasted_iota(jnp.int32, (n, n), 1)).astype(F32)
    units = [(bi, slice(h * n, (h + 1) * n), h) for bi in range(r_ref.shape[0]) for h in range(TOK_HEADS)]
    row = lambda ref: [ref[bi, :, lanes] for bi, lanes, _ in units]
    r, k, v, kk, a, ld = row(r_ref), row(k_ref), row(v_ref), row(kk_ref), row(a_ref), row(ld_ref)
    s = [s_ref[bi, h] for bi, _, h in units]
    sa = [jnp.sum(_bf16_round(si) * _bf16_round(-kki), axis=1, keepdims=True) for si, kki in zip(s, kk)]
    v_col = [jnp.sum(eye * vi, axis=1, keepdims=True) for vi in v]
    s_new = [si * jnp.exp(ldi) + sai * (kki * ai) + vci * ki
             for si, ldi, sai, kki, ai, vci, ki in zip(s, ld, sa, kk, a, v_col, k)]
    y_col = [jnp.sum(_bf16_round(si) * _bf16_round(ri), axis=1, keepdims=True) for si, ri in zip(s_new, r)]
    y = [jnp.sum(eye * yc, axis=0, keepdims=True) for yc in y_col]
    for i, (bi, lanes, h) in enumerate(units):
        so_ref[bi, h] = s_new[i]
        o_ref[bi, :, lanes] = _rwkv_finish(y[i], r[i], k[i], v[i], g_ref[bi, :, lanes], rk_ref[:, lanes],
                                           lnw_ref[:, lanes], lnb_ref[:, lanes])


def _rwkv_step(r, ld, k, v, kk, a, g, r_k, ln_w, ln_b, state):
    n = r.shape[0]
    nb = RWKV_STEP_ROWS if n % RWKV_STEP_ROWS == 0 else 1
    tok = pl.BlockSpec((nb, 1, QA), lambda i: (i, 0, 0))
    const = pl.BlockSpec((1, QA), lambda i: (0, 0))
    st = pl.BlockSpec((nb, TOK_HEADS, HEAD_DIM, HEAD_DIM), lambda i: (i, 0, 0, 0))
    r3 = lambda z: z.reshape(n, 1, QA)
    return pl.pallas_call(
        _rwkv_step_kernel, grid=(n // nb,),
        in_specs=[tok] * 7 + [const] * 3 + [st],
        out_specs=[tok, st],
        out_shape=[jax.ShapeDtypeStruct((n, 1, QA), F32), jax.ShapeDtypeStruct(state.shape, F32)],
        compiler_params=_cparams("parallel"), name="rwkv_step",
    )(r3(r), r3(ld), r3(k), r3(v), r3(kk), r3(a), r3(g), r_k.reshape(1, QA), ln_w.reshape(1, QA),
      ln_b.reshape(1, QA), state)


def _split_w_in_a(w):
    offs = [0, QA]
    for _ in range(6):
        offs.append(offs[-1] + KVW)
    offs.append(offs[-1] + 3 * TOK_HEADS)
    offs.append(offs[-1] + MEMQ)
    pieces = [w[:, offs[i]:offs[i + 1]] for i in range(9)]
    pieces[7] = jnp.pad(pieces[7], ((0, 0), (0, LANES - 3 * TOK_HEADS)))
    return pieces


_A_OPS = ("hnorm", "none", "none", "hnorm", "none", "hnorm", "none", "sigmoid", "none")


def _ffn(x, mix, mem, w_out, g_ffn, wc, bc, wf, bf, w_gate, w_up, w_down, layer, tm_proj, tm_moe):
    x_new, hn, cw = _outproj_router(x, mix, mem, w_out, g_ffn, wc, bc, wf, bf, tm_proj)
    if x.shape[0] % MOE_TOKEN_TILE == 0:
        return _moe_routed(x_new, hn, cw, w_gate, w_up, w_down, layer)
    return _moe(x_new, hn, cw, w_gate, w_up, w_down, layer, tm_moe)


def kernel(x_prompt, x_sample, cache_cmp_k, cache_cmp_v, cache_sel_k, cache_sel_v, cache_win_k, cache_win_v, cache_mem_k, cache_mem_v, state_rwkv, state_shift, page_table, mem_prompt, norm_mix, norm_ffn, norm_mem, w_mem_kv, mem_q_gain, mem_k_gain, w_in_a, nsa_q_gain, nsa_k_gain, cmp_pe, cmp_w1, cmp_b1, cmp_w2, w_in_b, rwkv_mu, rwkv_w0, rwkv_w_up, rwkv_a0, rwkv_a_up, rwkv_g_up, rwkv_k_k, rwkv_k_a, rwkv_r_k, rwkv_ln_w, rwkv_ln_b, w_out, moe_w_coarse, moe_b_coarse, moe_w_fine, moe_b_fine, moe_w_gate, moe_w_up, moe_w_down):
    b, s, d = x_prompt.shape
    bd = x_sample.shape[0]
    depth = norm_mix.shape[0]
    m_len = mem_prompt.shape[1]
    wl = min(WINDOW, s)
    xp = x_prompt.reshape(b * s, d)
    xs = x_sample.reshape(bd, d)
    mem2 = mem_prompt.reshape(b * m_len, d)
    outs = {name: [] for name in ("pc_k", "pc_v", "ps_k", "ps_v", "pw_k", "pw_v", "pm_k", "pm_v", "pr_s", "pr_x",
                                  "sc_k", "sc_v", "ss_k", "ss_v", "sw_k", "sw_v", "sr_s", "sr_x")}
    for i in range(depth):
        km_p, vm_p = _norm_proj(mem2, norm_mem[i], [w_mem_kv[i][:, :MEMQ], w_mem_kv[i][:, MEMQ:]],
                                ("hnorm", "none"), [mem_k_gain[i]], 256)
        km_p, vm_p = km_p.reshape(b, m_len, MEMQ), vm_p.reshape(b, m_len, MEMQ)
        outs["pm_k"].append(km_p.reshape(b, m_len, MEM_HEADS, HEAD_DIM))
        outs["pm_v"].append(vm_p.reshape(b, m_len, MEM_HEADS, HEAD_DIM))
        if i % 2 == 0:
            ia = i // 2
            pieces = _split_w_in_a(w_in_a[ia])
            gains = [nsa_q_gain[ia], nsa_k_gain[ia, 1], nsa_k_gain[ia, 2]]
            cmp_args = (cmp_pe[ia], cmp_w1[ia], cmp_b1[ia], cmp_w2[ia], nsa_k_gain[ia, 0])
            q, kc, vc, ks, vs, kw, vw, gt, mq_p = _norm_proj(xp, norm_mix[i], pieces, _A_OPS, gains, PROJ_ROWS)
            r3 = lambda z: z.reshape(b, s, -1)
            kc, vc, ks, vs, kw, vw = (r3(z) for z in (kc, vc, ks, vs, kw, vw))
            ck, cv = _compress_prompt(kc, vc, *cmp_args)
            mix_p = _nsa_prompt(r3(q), r3(gt), ck, cv, ks, vs, kw, vw).reshape(b * s, QA)
            r5 = lambda z: z.reshape(b, -1, KV_GROUPS, HEAD_DIM)
            for name, z in (("pc_k", kc), ("pc_v", vc), ("ps_k", ks), ("ps_v", vs),
                            ("pw_k", kw[:, s - wl:]), ("pw_v", vw[:, s - wl:])):
                outs[name].append(r5(z))
            q, kc, vc, ks, vs, kw, vw, gt, mq_s = _norm_proj(xs, norm_mix[i], pieces, _A_OPS, gains, 256)
            ck, cv = _compress_sample(page_table, cache_cmp_k[ia], cache_cmp_v[ia], *cmp_args)
            mix_s, nwk, nwv = _nsa_sample(page_table, q, gt, ck, cv, ks, vs, kw, vw,
                                          cache_sel_k[ia], cache_sel_v[ia], cache_win_k[ia], cache_win_v[ia])
            mix_s = mix_s.reshape(bd, QA)
            r5 = lambda z: z.reshape(bd, -1, KV_GROUPS, HEAD_DIM)
            for name, z in (("sc_k", kc), ("sc_v", vc), ("ss_k", ks), ("ss_v", vs), ("sw_k", nwk), ("sw_v", nwv)):
                outs[name].append(r5(z))
        else:
            ib = i // 2
            pieces = [w_in_b[ib][:, :SHIFT_W], w_in_b[ib][:, SHIFT_W:]]
            wargs = _rwkv_weight_args(rwkv_mu[ib], rwkv_w0[ib], rwkv_w_up[ib], rwkv_a0[ib], rwkv_a_up[ib],
                                      rwkv_g_up[ib], rwkv_k_k[ib], rwkv_k_a[ib])
            fin = (rwkv_r_k[ib].reshape(-1), rwkv_ln_w[ib], rwkv_ln_b[ib])
            pr, mq_p = _norm_proj(xp, norm_mix[i], pieces, ("none", "none"), [], PROJ_ROWS)
            pr = pr.reshape(b, s, SHIFT_W)
            prep = _rwkv_prep_prompt(pr, wargs, 256)
            mix_p, st_p = _rwkv_chunk(*prep, *fin)
            mix_p = mix_p.reshape(b * s, QA)
            outs["pr_s"].append(st_p)
            outs["pr_x"].append(pr[:, s - 1])
            pr, mq_s = _norm_proj(xs, norm_mix[i], pieces, ("none", "none"), [], 256)
            prep = _rwkv_prep_sample(pr, state_shift[ib], wargs)
            mix_s, st_s = _rwkv_step(*prep, *fin, state_rwkv[ib])
            mix_s = mix_s.reshape(bd, QA)
            outs["sr_s"].append(st_s)
            outs["sr_x"].append(pr)
        mem_p = _mem_attn(mq_p.reshape(b, s, MEMQ), km_p, vm_p, mem_q_gain[i], MEM_Q_ROWS).reshape(b * s, MEMQ)
        mem_s = _mem_attn(mq_s.reshape(bd, 1, MEMQ), cache_mem_k[i].reshape(bd, -1, MEMQ),
                          cache_mem_v[i].reshape(bd, -1, MEMQ), mem_q_gain[i], 1).reshape(bd, MEMQ)
        ffn_w = (w_out[i], norm_ffn[i], moe_w_coarse[i], moe_b_coarse[i], moe_w_fine[i], moe_b_fine[i],
                 moe_w_gate, moe_w_up, moe_w_down, i)
        xp = _ffn(xp, mix_p, mem_p, *ffn_w, PROJ_ROWS, 1024)
        xs = _ffn(xs, mix_s, mem_s, *ffn_w, PROJ_ROWS, 1024)
    order = ("pc_k", "pc_v", "ps_k", "ps_v", "pw_k", "pw_v", "pm_k", "pm_v", "pr_s", "pr_x",
             "sc_k", "sc_v", "ss_k", "ss_v", "sw_k", "sw_v", "sr_s", "sr_x")
    return (xp.reshape(b, s, d), xs.reshape(bd, 1, d)) + tuple(jnp.stack(outs[name]) for name in order)
```

```python
import functools

import jax
import jax.numpy as jnp
import numpy as np
from jax import lax
from jax.experimental import pallas as pl
from jax.experimental.pallas import tpu as pltpu

F32 = jnp.float32
BF16 = jnp.bfloat16
HI = lax.Precision.HIGHEST

HEAD_DIM = 64
TOK_HEADS = 12
MEM_HEADS = 4
KV_GROUPS = 3
HEADS_PER_GROUP = TOK_HEADS // KV_GROUPS
QA = TOK_HEADS * HEAD_DIM
KVW = KV_GROUPS * HEAD_DIM
MEMQ = MEM_HEADS * HEAD_DIM
CMP_LEN = 32
CMP_STRIDE = 16
CMP_HID = 128
SEL_LEN = 64
SEL_TOPK = 16
WINDOW = 512
Q_BLOCK = 256
MEM_Q_ROWS = 1024
PROJ_ROWS = 1024
NSA_KEY_STEP = 512
PAGE = 128
N_GROUPS = 4
E_PER_GROUP = 8
N_EXPERTS = 32
MOE_TOKEN_TILE = 512
MOE_ROW_TILE = 512
RWKV_COLS = (768, 768, 768, 64, 64, 128)
SHIFT_W = sum(RWKV_COLS)
RWKV_GN_EPS = 64e-5
RWKV_CHUNK = 64
RWKV_STEP_ROWS = 4
RWKV_ROWS_PER_STEP = 2
RWKV_CHUNK_EXACT = False
NEG = -1e30
BIG = 1e30
LANES = 128
VMEM_LIMIT = 56 * 1024 * 1024


def _cparams(*sem):
    return pltpu.CompilerParams(dimension_semantics=sem, vmem_limit_bytes=VMEM_LIMIT)


def _dot_general(a, b, dims, exact):
    if exact:
        return lax.dot_general(a, b, (dims, ((), ())), precision=HI, preferred_element_type=F32)
    return lax.dot_general(a.astype(BF16), b.astype(BF16), (dims, ((), ())), preferred_element_type=F32)


def _dot(a, b, exact=False):
    return _dot_general(a, b, ((1,), (0,)), exact)


def _dot_nt(a, b, exact=False):
    return _dot_general(a, b, ((1,), (1,)), exact)


def _dot_tn(a, b, exact=False):
    return _dot_general(a, b, ((0,), (0,)), exact)


def _bf16_round(x):
    return x.astype(BF16).astype(F32)


def _rms(x, g, eps=1e-6):
    return x * lax.rsqrt(jnp.mean(x * x, axis=-1, keepdims=True) + eps) * g


def _masked_softmax(s, mask):
    s = jnp.where(mask, s, NEG)
    m = jnp.max(s, axis=-1, keepdims=True)
    e = jnp.where(mask, jnp.exp(s - m), 0.0)
    return e / jnp.maximum(jnp.sum(e, axis=-1, keepdims=True), 1e-30)


def _alibi_slope(h):
    return float(np.exp2(np.float32(-8.0) * np.float32(h + 1) / np.float32(TOK_HEADS)))


def _topk_mask(score, k, n_real):
    lane = lax.broadcasted_iota(jnp.int32, score.shape, 1)
    ahead = jnp.zeros(score.shape, F32)
    for j in range(n_real):
        col = score[:, j:j + 1]
        ahead = ahead + jnp.where((col > score) | ((col == score) & (lane > j)), 1.0, 0.0)
    return jnp.where(ahead < k, 1.0, 0.0)


def _topk_rows_t(score_t, k):
    n_blk = score_t.shape[0]
    idx = lax.broadcasted_iota(jnp.int32, score_t.shape, 0)
    ahead = jnp.zeros(score_t.shape, F32)
    for j in range(n_blk):
        row = score_t[j:j + 1]
        ahead = ahead + jnp.where((row > score_t) | ((row == score_t) & (idx > j)), 1.0, 0.0)
    return jnp.where(ahead < k, 1.0, 0.0)


def _sel_to_cmp(n_sel_rows, n_cmp_lanes):
    s0 = lax.broadcasted_iota(jnp.int32, (n_sel_rows, n_cmp_lanes), 0) * SEL_LEN
    c0 = lax.broadcasted_iota(jnp.int32, (n_sel_rows, n_cmp_lanes), 1) * CMP_STRIDE
    return ((c0 < s0 + SEL_LEN) & (c0 + CMP_LEN > s0)).astype(F32)


def _cmp_to_sel(n_cmp_rows, n_sel_lanes):
    n_i = lax.broadcasted_iota(jnp.int32, (n_cmp_rows, n_sel_lanes), 0)
    s_i = lax.broadcasted_iota(jnp.int32, (n_cmp_rows, n_sel_lanes), 1)
    c0 = n_i * CMP_STRIDE
    s0 = s_i * SEL_LEN
    return ((c0 < s0 + SEL_LEN) & (c0 + CMP_LEN > s0)).astype(F32)


def _block_expand(n_blk_rows, n_keys):
    b_i = lax.broadcasted_iota(jnp.int32, (n_blk_rows, n_keys), 0)
    k_i = lax.broadcasted_iota(jnp.int32, (n_blk_rows, n_keys), 1)
    return (jnp.right_shift(k_i, 6) == b_i).astype(BF16)


def _norm_proj_kernel(ops, x_ref, g_ref, *refs):
    n = len(ops)
    n_gain = sum(op == "hnorm" for op in ops)
    w_refs, gain_refs, o_refs = refs[:n], refs[n:n + n_gain], refs[n + n_gain:]
    h = _rms(x_ref[...], g_ref[...]).astype(BF16)
    gi = 0
    for op, w_ref, o_ref in zip(ops, w_refs, o_refs):
        z = _dot(h, w_ref[...])
        if op == "hnorm":
            gain2 = gain_refs[gi][...]
            gi += 1
            low = lax.broadcasted_iota(jnp.int32, (1, LANES), 1) < HEAD_DIM
            for t0 in range(0, z.shape[1], LANES):
                zt = z[:, t0:t0 + LANES]
                sq = zt * zt
                if zt.shape[1] == LANES:
                    ss = jnp.where(low, jnp.sum(jnp.where(low, sq, 0.0), axis=-1, keepdims=True),
                                   jnp.sum(jnp.where(low, 0.0, sq), axis=-1, keepdims=True))
                else:
                    ss = jnp.sum(sq, axis=-1, keepdims=True)
                o_ref[:, t0:t0 + LANES] = zt * lax.rsqrt(ss * (1.0 / HEAD_DIM) + 1e-6) * gain2[:, 0:zt.shape[1]]
        elif op == "sigmoid":
            o_ref[...] = jax.nn.sigmoid(z)
        else:
            o_ref[...] = z


def _norm_proj(x, g, weights, ops, gains, tm):
    n_rows, d = x.shape
    tm = min(tm, n_rows)
    assert n_rows % tm == 0
    in_specs = [pl.BlockSpec((tm, d), lambda i: (i, 0)), pl.BlockSpec((1, d), lambda i: (0, 0))]
    in_specs += [pl.BlockSpec(w.shape, lambda i: (0, 0)) for w in weights]
    in_specs += [pl.BlockSpec((1, LANES), lambda i: (0, 0)) for _ in gains]
    out_shape = [jax.ShapeDtypeStruct((n_rows, w.shape[1]), F32) for w in weights]
    out_specs = [pl.BlockSpec((tm, w.shape[1]), lambda i: (i, 0)) for w in weights]
    return pl.pallas_call(
        functools.partial(_norm_proj_kernel, tuple(ops)),
        grid=(n_rows // tm,), in_specs=in_specs, out_specs=out_specs, out_shape=out_shape,
        compiler_params=_cparams("parallel"), name="norm_proj",
    )(x, g.reshape(1, d), *[w.astype(BF16) for w in weights],
      *[jnp.tile(gn, LANES // HEAD_DIM).reshape(1, LANES) for gn in gains])


def _compress_rows(rows_ref, pe_ref, w1_ref, b1_ref, w2_ref, gain, o_ref, u_ref, n_chunk):
    for g in range(KV_GROUPS):
        for l in range(CMP_STRIDE):
            src = l * KVW + g * HEAD_DIM
            u_ref[:, l * HEAD_DIM:(l + 1) * HEAD_DIM] = rows_ref[:, src:src + HEAD_DIM]
        _compress_mlp(u_ref, pe_ref, w1_ref, b1_ref, w2_ref, gain, o_ref, g, n_chunk)


def _compress_mlp(u_ref, pe_ref, w1_ref, b1_ref, w2_ref, gain, o_ref, g, n_chunk):
    half = CMP_STRIDE * HEAD_DIM
    u = u_ref[...]
    p0 = _dot(u + pe_ref[:, 0:half], w1_ref[0:half, :])
    p1 = _dot(u + pe_ref[:, half:2 * half], w1_ref[half:2 * half, :])
    hid = b1_ref[...] + p0 + pltpu.roll(p1, n_chunk - 1, 0)
    z = _dot(jax.nn.gelu(hid), w2_ref[...])
    if gain is not None:
        z = _rms(z, gain)
    o_ref[:, g * HEAD_DIM:(g + 1) * HEAD_DIM] = z


def _compress_cols(xt_ref, pe_ref, w1_ref, b1_ref, w2_ref, gain, o_ref, xs_ref, u_ref, n_chunk):
    half = CMP_STRIDE * 2 * HEAD_DIM
    lane = lax.broadcasted_iota(jnp.int32, (1, 2 * HEAD_DIM), 1)
    for first in (0, HEAD_DIM):
        xs_ref[...] = xt_ref[first:first + 2 * HEAD_DIM, :].T
        for l in range(CMP_STRIDE):
            u_ref[:, l * 2 * HEAD_DIM:(l + 1) * 2 * HEAD_DIM] = xs_ref[pl.ds(l, n_chunk, stride=CMP_STRIDE), :]
        u = u_ref[...]
        p0 = _dot(u + pe_ref[:, 0:half], w1_ref[0:half, :])
        p1 = _dot(u + pe_ref[:, half:2 * half], w1_ref[half:2 * half, :])
        hid = b1_ref[...] + p0 + pltpu.roll(p1, n_chunk - 1, 0)
        z = _dot(jax.nn.gelu(hid), w2_ref[...])
        if gain is not None:
            sq = z * z
            ss_lo = jnp.sum(jnp.where(lane < HEAD_DIM, sq, 0.0), axis=-1, keepdims=True)
            ss_hi = jnp.sum(jnp.where(lane < HEAD_DIM, 0.0, sq), axis=-1, keepdims=True)
            ms = jnp.where(lane < HEAD_DIM, ss_lo, ss_hi) * (1.0 / HEAD_DIM)
            z = z * lax.rsqrt(ms + 1e-6) * gain
        if first == 0:
            o_ref[:, 0:2 * HEAD_DIM] = z
        else:
            o_ref[:, 2 * HEAD_DIM:3 * HEAD_DIM] = z[:, HEAD_DIM:2 * HEAD_DIM]


def _compress_prompt_kernel(k_ref, v_ref, pe_ref, w1_ref, b1_ref, w2_ref, gain_ref, ck_ref, cv_ref, u_ref,
                            *, n_chunk):
    _compress_rows(k_ref.at[0], pe_ref.at[0], w1_ref.at[0], b1_ref.at[0], w2_ref.at[0], gain_ref[...],
                   ck_ref.at[0], u_ref, n_chunk)
    _compress_rows(v_ref.at[0], pe_ref.at[1], w1_ref.at[1], b1_ref.at[1], w2_ref.at[1], None,
                   cv_ref.at[0], u_ref, n_chunk)


def _compress_weight_specs(nidx):
    zero = lambda *_: (0, 0, 0)
    del nidx
    return [pl.BlockSpec((2, 1, CMP_LEN * HEAD_DIM), zero),
            pl.BlockSpec((2, CMP_LEN * HEAD_DIM, CMP_HID), zero),
            pl.BlockSpec((2, 1, CMP_HID), zero),
            pl.BlockSpec((2, CMP_HID, HEAD_DIM), zero)]


def _compress_prompt(kc, vc, pe, w1, b1, w2, gain):
    b, s, _ = kc.shape
    n_chunk = s // CMP_STRIDE
    cw = CMP_STRIDE * KVW
    rows = pl.BlockSpec((1, n_chunk, cw), lambda i: (i, 0, 0))
    out = pl.BlockSpec((1, n_chunk, KVW), lambda i: (i, 0, 0))
    return pl.pallas_call(
        functools.partial(_compress_prompt_kernel, n_chunk=n_chunk),
        grid=(b,),
        in_specs=[rows, rows] + _compress_weight_specs(1) + [pl.BlockSpec((1, HEAD_DIM), lambda i: (0, 0))],
        out_specs=[out, out],
        out_shape=[jax.ShapeDtypeStruct((b, n_chunk, KVW), F32)] * 2,
        scratch_shapes=[pltpu.VMEM((n_chunk, CMP_STRIDE * HEAD_DIM), F32)],
        compiler_params=_cparams("parallel"), name="compress_prompt",
    )(kc.reshape(b, n_chunk, cw), vc.reshape(b, n_chunk, cw), pe.reshape(2, 1, -1), w1, b1.reshape(2, 1, -1), w2,
      gain.reshape(1, HEAD_DIM))


def _compress_sample_kernel(pt_ref, pk_ref, pv_ref, pe_ref, w1_ref, b1_ref, w2_ref, gain_ref,
                            ck_ref, cv_ref, kbuf, vbuf, xs_ref, u_ref, sem, *, n_pages):
    b = pl.program_id(0)
    n_chunk = n_pages * PAGE // CMP_STRIDE

    def k_copies(row_b):
        return [pltpu.make_async_copy(pk_ref.at[pt_ref[row_b, j]], kbuf.at[:, pl.ds(j * PAGE, PAGE)], sem.at[0])
                for j in range(n_pages)]

    v_copies = [pltpu.make_async_copy(pv_ref.at[pt_ref[b, j]], vbuf.at[:, pl.ds(j * PAGE, PAGE)], sem.at[1])
                for j in range(n_pages)]

    @pl.when(b == 0)
    def _():
        for cp in k_copies(b):
            cp.start()

    for cp in v_copies:
        cp.start()
    for cp in k_copies(b):
        cp.wait()
    _compress_cols(kbuf, pe_ref.at[0], w1_ref.at[0], b1_ref.at[0], w2_ref.at[0], gain_ref[...],
                   ck_ref.at[0], xs_ref, u_ref, n_chunk)

    @pl.when(b + 1 < pl.num_programs(0))
    def _():
        for cp in k_copies(b + 1):
            cp.start()

    for cp in v_copies:
        cp.wait()
    _compress_cols(vbuf, pe_ref.at[1], w1_ref.at[1], b1_ref.at[1], w2_ref.at[1], None,
                   cv_ref.at[0], xs_ref, u_ref, n_chunk)


def _compress_sample(page_table, pool_k, pool_v, pe, w1, b1, w2, gain):
    bd, n_pages = page_table.shape
    n_pool = pool_k.shape[0]
    n_chunk = n_pages * PAGE // CMP_STRIDE
    past = n_pages * PAGE
    pos_minor = lambda z: jnp.transpose(z, (0, 2, 3, 1)).reshape(n_pool, KVW, PAGE)
    eye2 = jnp.eye(2, dtype=F32)
    n_l, dh, hid = CMP_STRIDE, HEAD_DIM, CMP_HID
    pe2 = jnp.broadcast_to(pe.reshape(2, 2, n_l, 1, dh), (2, 2, n_l, 2, dh)).reshape(2, 1, 4 * n_l * dh)
    w1x = jnp.einsum("krldf,gh->krlgdhf", w1.reshape(2, 2, n_l, dh, hid), eye2).reshape(2, 4 * n_l * dh, 2 * hid)
    b1x = jnp.tile(b1, (1, 2)).reshape(2, 1, 2 * hid)
    w2x = jnp.einsum("kfd,gh->kgfhd", w2, eye2).reshape(2, 2 * hid, 2 * dh)
    zero3 = lambda i, pt: (0, 0, 0)
    any_spec = pl.BlockSpec(memory_space=pl.ANY)
    out = pl.BlockSpec((1, n_chunk, KVW), lambda i, pt: (i, 0, 0))
    grid_spec = pltpu.PrefetchScalarGridSpec(
        num_scalar_prefetch=1, grid=(bd,),
        in_specs=[any_spec, any_spec,
                  pl.BlockSpec(pe2.shape, zero3), pl.BlockSpec(w1x.shape, zero3),
                  pl.BlockSpec(b1x.shape, zero3), pl.BlockSpec(w2x.shape, zero3),
                  pl.BlockSpec((1, 2 * dh), lambda i, pt: (0, 0))],
        out_specs=[out, out],
        scratch_shapes=[pltpu.VMEM((KVW, past), F32), pltpu.VMEM((KVW, past), F32),
                        pltpu.VMEM((past, LANES), F32),
                        pltpu.VMEM((n_chunk, 2 * n_l * dh), F32), pltpu.SemaphoreType.DMA((2,))])
    return pl.pallas_call(
        functools.partial(_compress_sample_kernel, n_pages=n_pages),
        grid_spec=grid_spec,
        out_shape=[jax.ShapeDtypeStruct((bd, n_chunk, KVW), F32)] * 2,
        compiler_params=_cparams("arbitrary"), name="compress_sample",
    )(page_table, pos_minor(pool_k), pos_minor(pool_v),
      pe2, w1x.astype(BF16), b1x, w2x.astype(BF16), jnp.tile(gain, 2).reshape(1, 2 * dh))


def _softmax_rows(s):
    e = jnp.exp(s - jnp.max(s, axis=-1, keepdims=True))
    return e / jnp.sum(e, axis=-1, keepdims=True)


def _nsa_prompt_block(q_ref, gt_ref, ck_ref, cv_ref, ks_ref, vs_ref, kw_ref, vw_ref, o_ref,
                      *, seq, n_keys, first_block):
    tq = Q_BLOCK
    hpg = HEADS_PER_GROUP
    n_cmp = ck_ref.shape[1]
    n_sel = seq // SEL_LEN
    wlen = min(WINDOW + tq, seq)
    q0 = (pl.program_id(1) + first_block) * tq
    t1 = q0 + lax.broadcasted_iota(jnp.int32, (tq, 1), 0)
    s2c = _sel_to_cmp(n_sel, n_cmp)
    expand = _block_expand(n_sel, n_keys)
    blk = lax.broadcasted_iota(jnp.int32, (n_sel, 1), 0)
    t_row = q0 + lax.broadcasted_iota(jnp.int32, (1, tq), 1)
    cur = jnp.right_shift(t_row, 6)
    forced = (blk == 0) | (blk == cur) | (blk == cur - 1)
    valid = blk * SEL_LEN <= t_row
    dist_c = t1 - (lax.broadcasted_iota(jnp.int32, (1, n_cmp), 1) * CMP_STRIDE + (CMP_LEN - 1))
    mask_c = dist_c >= 0
    dist_cf = dist_c.astype(F32)
    dist_s = t1 - lax.broadcasted_iota(jnp.int32, (1, n_keys), 1)
    causal_s = dist_s >= 0
    dist_sf = dist_s.astype(F32)
    w_start = pl.multiple_of(jnp.clip(q0 - WINDOW, 0, seq - wlen), tq)
    dist_w = t1 - (w_start + lax.broadcasted_iota(jnp.int32, (1, wlen), 1))
    bias_w = jnp.where((dist_w >= 0) & (dist_w <= WINDOW), 0.0, NEG)
    dist_wf = dist_w.astype(F32)
    gt = gt_ref[0]
    for g in range(KV_GROUPS):
        lanes = slice(g * HEAD_DIM, (g + 1) * HEAD_DIM)
        heads = [g * hpg + j for j in range(hpg)]
        rows = [slice(j * tq, (j + 1) * tq) for j in range(hpg)]
        q4 = jnp.concatenate([q_ref[0, :, h * HEAD_DIM:(h + 1) * HEAD_DIM] for h in heads],
                             axis=0) * (HEAD_DIM ** -0.5)
        s_c = _dot_nt(q4, ck_ref[0, :, lanes])
        p_c = [_masked_softmax(s_c[rows[j]] - _alibi_slope(heads[j]) * dist_cf, mask_c) for j in range(hpg)]
        cv = cv_ref[0, :, lanes]
        o_c = [_dot(p, cv) for p in p_c]
        imps = [_dot_nt(s2c, p) for p in p_c]
        imp = imps[0]
        for j in range(1, hpg):
            imp = imp + imps[j]
        score = jnp.where(valid, jnp.where(forced, BIG, imp), -BIG)
        key_sel = _dot_tn(_topk_rows_t(score, min(SEL_TOPK, n_sel)), expand)
        bias_s = jnp.where((key_sel > 0.5) & causal_s, 0.0, NEG)
        s_s = _dot_nt(q4, ks_ref[0, 0:n_keys, lanes])
        vs = vs_ref[0, 0:n_keys, lanes]
        o_s = [_dot(_softmax_rows(s_s[rows[j]] - _alibi_slope(heads[j]) * dist_sf + bias_s), vs)
               for j in range(hpg)]
        s_w = _dot_nt(q4, kw_ref[0, pl.ds(w_start, wlen), lanes])
        vw = vw_ref[0, pl.ds(w_start, wlen), lanes]
        o_w = [_dot(_softmax_rows(s_w[rows[j]] - _alibi_slope(heads[j]) * dist_wf + bias_w), vw)
               for j in range(hpg)]
        for j, h in enumerate(heads):
            o_ref[0, :, h * HEAD_DIM:(h + 1) * HEAD_DIM] = (
                gt[:, 3 * h:3 * h + 1] * o_c[j] + gt[:, 3 * h + 1:3 * h + 2] * o_s[j]
                + gt[:, 3 * h + 2:3 * h + 3] * o_w[j])


def _nsa_prompt(q, gates, ck, cv, ks, vs, kw, vw):
    b, s, _ = q.shape
    n_cmp = ck.shape[1]
    key_step = min(NSA_KEY_STEP, s)
    per_class = key_step // Q_BLOCK
    full = lambda n: pl.BlockSpec((1, n, KVW), lambda i, j: (i, 0, 0))
    outs = []
    for c in range(s // key_step):
        n_keys = (c + 1) * key_step
        first = c * per_class
        qspec = lambda w, first=first: pl.BlockSpec((1, Q_BLOCK, w), lambda i, j: (i, j + first, 0))
        outs.append(pl.pallas_call(
            functools.partial(_nsa_prompt_block, seq=s, n_keys=n_keys, first_block=first),
            grid=(b, per_class),
            in_specs=[qspec(QA), qspec(LANES), full(n_cmp), full(n_cmp), full(n_keys), full(n_keys), full(s), full(s)],
            out_specs=pl.BlockSpec((1, Q_BLOCK, QA), lambda i, j: (i, j, 0)),
            out_shape=jax.ShapeDtypeStruct((b, key_step, QA), F32),
            compiler_params=_cparams("parallel", "parallel"), name=f"nsa_prompt_{n_keys}",
        )(q, gates, ck, cv, ks, vs, kw, vw))
    return jnp.concatenate(outs, axis=1)


def _gather_pages_t(pt_ref, b, pool_ref, dst_ref, sem, n_pages):
    return [pltpu.make_async_copy(pool_ref.at[pt_ref[b, j]], dst_ref.at[:, :, pl.ds(j * PAGE, PAGE)], sem)
            for j in range(n_pages)]


def _nsa_sample_kernel(pt_ref, q_ref, gt_ref, ck_ref, cv_ref, ksn_ref, vsn_ref, kwn_ref, vwn_ref,
                       pks_ref, pvs_ref, wk_ref, wv_ref, o_ref, owk_ref, owv_ref, kbuf, vbuf, sem,
                       *, n_pages):
    b = pl.program_id(0)
    hpg = HEADS_PER_GROUP
    past = n_pages * PAGE
    n_cmp = ck_ref.shape[1]
    n_sel_past = past // SEL_LEN
    n_sel = n_sel_past + 1
    wb = wk_ref.shape[3]
    sel_lanes = 2 * LANES
    slot = lax.rem(b, 2)

    def page_copies(row_b, to_slot):
        return (_gather_pages_t(pt_ref, row_b, pks_ref, kbuf.at[to_slot], sem.at[0, to_slot], n_pages)
                + _gather_pages_t(pt_ref, row_b, pvs_ref, vbuf.at[to_slot], sem.at[1, to_slot], n_pages))

    @pl.when(b == 0)
    def _():
        for cp in page_copies(b, slot):
            cp.start()

    @pl.when(b + 1 < pl.num_programs(0))
    def _():
        for cp in page_copies(b + 1, 1 - slot):
            cp.start()

    row = lax.broadcasted_iota(jnp.int32, (8, 1), 0)
    n = HEAD_DIM
    eye = (lax.broadcasted_iota(jnp.int32, (n, n), 0) == lax.broadcasted_iota(jnp.int32, (n, n), 1)).astype(F32)
    gt = gt_ref[0]
    c_end = lax.broadcasted_iota(jnp.int32, (1, n_cmp), 1) * CMP_STRIDE + (CMP_LEN - 1)
    dist_c = past - c_end
    blk = lax.broadcasted_iota(jnp.int32, (1, sel_lanes), 1)
    cur = past // SEL_LEN
    forced = (blk == 0) | (blk == cur) | (blk == cur - 1)
    valid = blk * SEL_LEN <= past
    c2s = _cmp_to_sel(n_cmp, sel_lanes)

    def q_rows(g):
        q4 = jnp.concatenate(
            [q_ref[0, :, (g * hpg + j) * HEAD_DIM:(g * hpg + j + 1) * HEAD_DIM] for j in range(hpg)]
            + [jnp.zeros((8 - hpg, HEAD_DIM), F32)], axis=0) * (HEAD_DIM ** -0.5)
        slope = jnp.full((8, 1), _alibi_slope(g * hpg + hpg - 1), F32)
        for j in range(hpg - 2, -1, -1):
            slope = jnp.where(row < j + 1, _alibi_slope(g * hpg + j), slope)
        return q4, slope

    o_cs, imps = [], []
    for g in range(KV_GROUPS):
        lanes = slice(g * HEAD_DIM, (g + 1) * HEAD_DIM)
        q4, slope = q_rows(g)
        p_c = _masked_softmax(_dot_nt(q4, ck_ref[0, :, lanes]) - slope * dist_c.astype(F32), dist_c >= 0)
        o_cs.append(_dot(p_c, cv_ref[0, :, lanes]))
        imps.append(jnp.sum(jnp.where(row < hpg, _dot(p_c, c2s), 0.0), axis=0, keepdims=True))
    imp = jnp.concatenate(imps + [jnp.zeros((8 - KV_GROUPS, sel_lanes), F32)], axis=0)
    score = jnp.where(valid, jnp.where(forced, BIG, imp), -BIG)
    score = jnp.where(blk < n_sel, score, -jnp.inf)
    sel = _topk_mask(score, min(SEL_TOPK, n_sel), n_sel)
    key_sel = _dot(sel[:, 0:n_sel_past], _block_expand(n_sel_past, past))
    sel_new = sel[:, n_sel_past:n_sel_past + 1]

    pos_s = lax.broadcasted_iota(jnp.int32, (1, past), 1)
    dist_s = (past - pos_s).astype(F32)
    lane_w = lax.broadcasted_iota(jnp.int32, (1, wb), 1)
    pos_w = past - wb + lane_w
    dist_w = past - pos_w
    mask_w = (dist_w >= 0) & (dist_w <= WINDOW) & (pos_w >= 0)
    for cp in page_copies(b, slot):
        cp.wait()
    for g in range(KV_GROUPS):
        lanes = slice(g * HEAD_DIM, (g + 1) * HEAD_DIM)
        q4, slope = q_rows(g)
        q4r = _bf16_round(q4)
        s_p = jnp.where(key_sel[g:g + 1] > 0.5, _dot(q4, kbuf[slot, g]) - slope * dist_s, NEG)
        new_ok = sel_new[g:g + 1] > 0.5
        s_n = jnp.where(new_ok, jnp.sum(q4r * _bf16_round(ksn_ref[0, :, lanes]), axis=-1, keepdims=True), NEG)
        m = jnp.maximum(jnp.max(s_p, axis=-1, keepdims=True), s_n)
        e_p = jnp.where(key_sel[g:g + 1] > 0.5, jnp.exp(s_p - m), 0.0)
        e_n = jnp.where(new_ok, jnp.exp(s_n - m), 0.0)
        den = jnp.maximum(jnp.sum(e_p, axis=-1, keepdims=True) + e_n, 1e-30)
        o_s = _dot_nt(e_p / den, vbuf[slot, g]) + _bf16_round(e_n / den) * _bf16_round(vsn_ref[0, :, lanes])
        s_p = jnp.where(mask_w, _dot(q4, wk_ref[0, g]) - slope * dist_w.astype(F32), NEG)
        s_n = jnp.sum(q4r * _bf16_round(kwn_ref[0, :, lanes]), axis=-1, keepdims=True)
        m = jnp.maximum(jnp.max(s_p, axis=-1, keepdims=True), s_n)
        e_p = jnp.where(mask_w, jnp.exp(s_p - m), 0.0)
        e_n = jnp.exp(s_n - m)
        den = jnp.maximum(jnp.sum(e_p, axis=-1, keepdims=True) + e_n, 1e-30)
        o_w = _dot_nt(e_p / den, wv_ref[0, g]) + _bf16_round(e_n / den) * _bf16_round(vwn_ref[0, :, lanes])
        for j in range(hpg):
            h = g * hpg + j
            o_ref[0, :, h * HEAD_DIM:(h + 1) * HEAD_DIM] = (
                gt[:, 3 * h:3 * h + 1] * o_cs[g][j:j + 1] + gt[:, 3 * h + 1:3 * h + 2] * o_s[j:j + 1]
                + gt[:, 3 * h + 2:3 * h + 3] * o_w[j:j + 1])
        k_col = jnp.sum(eye * kwn_ref[0, :, lanes], axis=1, keepdims=True)
        v_col = jnp.sum(eye * vwn_ref[0, :, lanes], axis=1, keepdims=True)
        owk_ref[0, g] = jnp.where(lane_w == wb - 1, k_col, pltpu.roll(wk_ref[0, g], wb - 1, 1))
        owv_ref[0, g] = jnp.where(lane_w == wb - 1, v_col, pltpu.roll(wv_ref[0, g], wb - 1, 1))


def _nsa_sample(page_table, q, gates, ck, cv, ks_new, vs_new, kw_new, vw_new, pool_ks, pool_vs, win_k, win_v):
    bd, n_pages = page_table.shape
    n_cmp = ck.shape[1]
    wb = win_k.shape[1]
    one = lambda w: pl.BlockSpec((1, 1, w), lambda i, pt: (i, 0, 0))
    rows = lambda m: pl.BlockSpec((1, m, KVW), lambda i, pt: (i, 0, 0))
    win = pl.BlockSpec((1, KV_GROUPS, HEAD_DIM, wb), lambda i, pt: (i, 0, 0, 0))
    any_spec = pl.BlockSpec(memory_space=pl.ANY)
    grid_spec = pltpu.PrefetchScalarGridSpec(
        num_scalar_prefetch=1, grid=(bd,),
        in_specs=[one(QA), one(LANES), rows(n_cmp), rows(n_cmp), one(KVW), one(KVW), one(KVW), one(KVW),
                  any_spec, any_spec, win, win],
        out_specs=[one(QA), win, win],
        scratch_shapes=[pltpu.VMEM((2, KV_GROUPS, HEAD_DIM, n_pages * PAGE), F32),
                        pltpu.VMEM((2, KV_GROUPS, HEAD_DIM, n_pages * PAGE), F32),
                        pltpu.SemaphoreType.DMA((2, 2))])
    r3 = lambda z: z.reshape(bd, 1, -1)
    pos_minor = lambda z: jnp.transpose(z, (0, 2, 3, 1))
    win_shape = jax.ShapeDtypeStruct((bd, KV_GROUPS, HEAD_DIM, wb), F32)
    mix, nwk, nwv = pl.pallas_call(
        functools.partial(_nsa_sample_kernel, n_pages=n_pages),
        grid_spec=grid_spec,
        out_shape=[jax.ShapeDtypeStruct((bd, 1, QA), F32), win_shape, win_shape],
        compiler_params=_cparams("arbitrary"), name="nsa_sample",
    )(page_table, r3(q), r3(gates), ck, cv, r3(ks_new), r3(vs_new), r3(kw_new), r3(vw_new),
      pos_minor(pool_ks), pos_minor(pool_vs), pos_minor(win_k), pos_minor(win_v))
    back = lambda z: jnp.transpose(z, (0, 3, 1, 2))
    return mix, back(nwk), back(nwv)


def _mem_attn_kernel(q_ref, k_ref, v_ref, gain_ref, o_ref):
    tq = q_ref.shape[1]
    pad = max(8 - tq, 0)
    for h in range(MEM_HEADS):
        lanes = slice(h * HEAD_DIM, (h + 1) * HEAD_DIM)
        q = _rms(q_ref[0, :, lanes], gain_ref[...]) * (HEAD_DIM ** -0.5)
        if pad:
            q = jnp.concatenate([q, jnp.zeros((pad, HEAD_DIM), F32)], axis=0)
        s = _dot_nt(q, k_ref[0, :, lanes])
        e = jnp.exp(s - jnp.max(s, axis=-1, keepdims=True))
        p = e / jnp.sum(e, axis=-1, keepdims=True)
        o_ref[0, :, lanes] = _dot(p, v_ref[0, :, lanes])[0:tq]


def _mem_attn(mq, km, vm, gain, tq):
    b, t, _ = mq.shape
    m = km.shape[1]
    tq = min(tq, t)
    qspec = pl.BlockSpec((1, tq, MEMQ), lambda i, j: (i, j, 0))
    kspec = pl.BlockSpec((1, m, MEMQ), lambda i, j: (i, 0, 0))
    return pl.pallas_call(
        _mem_attn_kernel, grid=(b, t // tq),
        in_specs=[qspec, kspec, kspec, pl.BlockSpec((1, HEAD_DIM), lambda i, j: (0, 0))],
        out_specs=qspec, out_shape=jax.ShapeDtypeStruct((b, t, MEMQ), F32),
        compiler_params=_cparams("parallel", "parallel"), name="mem_attn",
    )(mq, km, vm, gain.reshape(1, HEAD_DIM))


def _outproj_router_kernel(x_ref, mix_ref, mem_ref, wo_ref, g_ref, wc_ref, bc_ref, wf_ref, bf_ref,
                           xo_ref, hn_ref, cw_ref):
    x = x_ref[...] + (_dot(mix_ref[...], wo_ref[0:QA, :]) + _dot(mem_ref[...], wo_ref[QA:QA + MEMQ, :]))
    xo_ref[...] = x
    hn = _rms(x, g_ref[...])
    hn_ref[...] = hn
    hn = hn.astype(BF16)
    lane = lax.broadcasted_iota(jnp.int32, (1, LANES), 1).astype(F32)
    lg = jnp.where(lane < N_GROUPS, _dot(hn, wc_ref[...]) + bc_ref[...], -jnp.inf)
    m = jnp.max(lg, axis=-1, keepdims=True)
    grp = jnp.min(jnp.where(lg == m, lane, 1e9), axis=-1, keepdims=True)
    p_grp = 1.0 / jnp.sum(jnp.exp(lg - m), axis=-1, keepdims=True)
    in_grp = (lane >= grp * E_PER_GROUP) & (lane < (grp + 1.0) * E_PER_GROUP)
    lf = jnp.where(in_grp, _dot(hn, wf_ref[...]) + bf_ref[...], -jnp.inf)
    v1 = jnp.max(lf, axis=-1, keepdims=True)
    i1 = jnp.min(jnp.where(lf == v1, lane, 1e9), axis=-1, keepdims=True)
    lf2 = jnp.where(lane == i1, -jnp.inf, lf)
    v2 = jnp.max(lf2, axis=-1, keepdims=True)
    i2 = jnp.min(jnp.where(lf2 == v2, lane, 1e9), axis=-1, keepdims=True)
    e2 = jnp.exp(v2 - v1)
    den = 1.0 + e2
    cw_ref[...] = jnp.where(lane == i1, p_grp / den, 0.0) + jnp.where(lane == i2, p_grp * (e2 / den), 0.0)


def _outproj_router(x, mix, mem, w_out, g_ffn, w_coarse, b_coarse, w_fine, b_fine, tm):
    n, d = x.shape
    tm = min(tm, n)
    pad_w = lambda w: jnp.pad(w, ((0, 0), (0, LANES - w.shape[1])))
    pad_b = lambda v: jnp.pad(v, (0, LANES - v.shape[0])).reshape(1, LANES)
    row = lambda w: pl.BlockSpec((tm, w), lambda i: (i, 0))
    const = lambda r, c: pl.BlockSpec((r, c), lambda i: (0, 0))
    return pl.pallas_call(
        _outproj_router_kernel, grid=(n // tm,),
        in_specs=[row(d), row(QA), row(MEMQ), const(QA + MEMQ, d), const(1, d), const(d, LANES),
                  const(1, LANES), const(d, LANES), const(1, LANES)],
        out_specs=[row(d), row(d), row(LANES)],
        out_shape=[jax.ShapeDtypeStruct((n, d), F32), jax.ShapeDtypeStruct((n, d), F32),
                   jax.ShapeDtypeStruct((n, LANES), F32)],
        compiler_params=_cparams("parallel"), name="outproj_router",
    )(x, mix, mem, w_out.astype(BF16), g_ffn.reshape(1, d), pad_w(w_coarse).astype(BF16), pad_b(b_coarse),
      pad_w(w_fine).astype(BF16), pad_b(b_fine))


def _moe_kernel(x_ref, hn_ref, cw_ref, wg_ref, wu_ref, wd_ref, o_ref):
    e = pl.program_id(1)

    @pl.when(e == 0)
    def _():
        o_ref[...] = jnp.zeros_like(o_ref)

    hb = hn_ref[...].astype(BF16)
    gate = _dot(hb, wg_ref[0])
    up = _dot(hb, wu_ref[0])
    lane = lax.broadcasted_iota(jnp.int32, (1, LANES), 1)
    c = jnp.sum(jnp.where(lane == e, cw_ref[...], 0.0), axis=-1, keepdims=True)
    o_ref[...] += _dot(gate * jax.nn.sigmoid(gate) * up, wd_ref[0]) * c

    @pl.when(e == pl.num_programs(1) - 1)
    def _():
        o_ref[...] = x_ref[...] + o_ref[...]


def _moe(x, hn, cw, w_gate, w_up, w_down, layer, tm):
    n, d = x.shape
    tm = min(tm, n)
    _, n_exp, _, d_exp = w_gate.shape
    row = lambda w: pl.BlockSpec((tm, w), lambda i, e: (i, 0))
    return pl.pallas_call(
        _moe_kernel, grid=(n // tm, n_exp),
        in_specs=[row(d), row(d), row(LANES),
                  pl.BlockSpec((None, 1, d, d_exp), lambda i, e: (layer, e, 0, 0)),
                  pl.BlockSpec((None, 1, d, d_exp), lambda i, e: (layer, e, 0, 0)),
                  pl.BlockSpec((None, 1, d_exp, d), lambda i, e: (layer, e, 0, 0))],
        out_specs=row(d), out_shape=jax.ShapeDtypeStruct((n, d), F32),
        compiler_params=_cparams("parallel", "arbitrary"), name="moe",
    )(x, hn, cw, w_gate, w_up, w_down)


def _moe_plan_kernel(cw_ref, posa_ref, posb_ref, te_ref, nu_ref, cnt_ref, run_ref, start_ref, *, row_tile, n_rows):
    phase, i = pl.program_id(0), pl.program_id(1)
    tm = cw_ref.shape[0]
    cwt = cw_ref[...].T
    mask = cwt != 0.0
    maskf = mask.astype(F32)
    tile_cnt = jnp.sum(maskf, axis=1, keepdims=True)

    @pl.when((phase == 0) & (i == 0))
    def _():
        cnt_ref[...] = jnp.zeros_like(cnt_ref)

    @pl.when(phase == 0)
    def _():
        cnt_ref[...] += tile_cnt

    @pl.when((phase == 1) & (i == 0))
    def _():
        padded = jnp.floor((cnt_ref[...] + (row_tile - 1)) * (1.0 / row_tile)) * row_tile
        lower = (lax.broadcasted_iota(jnp.int32, (LANES, LANES), 0)
                 > lax.broadcasted_iota(jnp.int32, (LANES, LANES), 1)).astype(F32)
        start = _dot(lower, jnp.broadcast_to(padded, (LANES, LANES)), exact=True)[:, 0:1]
        start_ref[...] = start
        run_ref[...] = jnp.zeros_like(run_ref)
        tile_lo = lax.broadcasted_iota(jnp.int32, (1, 2 * LANES), 1).astype(F32) * row_tile
        n_done = jnp.sum(jnp.where(start + padded <= tile_lo, 1.0, 0.0), axis=0, keepdims=True)
        te_ref[...] = jnp.minimum(n_done, N_EXPERTS - 1.0).astype(jnp.int32)
        total = jnp.sum(padded, axis=0, keepdims=True)
        nu_ref[...] = jnp.broadcast_to(total * (1.0 / row_tile), (1, LANES)).astype(jnp.int32)

    @pl.when(phase == 1)
    def _():
        before = (lax.broadcasted_iota(jnp.int32, (tm, tm), 0)
                  < lax.broadcasted_iota(jnp.int32, (tm, tm), 1)).astype(BF16)
        pos = start_ref[...] + run_ref[...] + _dot(maskf, before)
        posa = jnp.min(jnp.where(mask, pos, 3e38), axis=0, keepdims=True)
        posb = jnp.max(jnp.where(mask, pos, -1.0), axis=0, keepdims=True)
        posb = jnp.where(jnp.sum(maskf, axis=0, keepdims=True) > 1.5, posb, n_rows - 1.0)
        posa_ref[0] = posa.astype(jnp.int32)
        posb_ref[0] = posb.astype(jnp.int32)
        run_ref[...] += tile_cnt


def _moe_row_copies(pos_refs, make_copy, tm):
    def issue(t, carry):
        for pos_ref in pos_refs:
            make_copy(pos_ref, t).start()
        return carry

    def drain(t, carry):
        for pos_ref in pos_refs:
            make_copy(pos_ref, 0).wait()
        return carry

    lax.fori_loop(0, tm, issue, 0, unroll=8)
    lax.fori_loop(0, tm, drain, 0, unroll=8)


def _moe_dispatch_kernel(posa_ref, posb_ref, hn_ref, xs_zero_ref, xs_ref, sem):
    del xs_zero_ref

    def make_copy(pos_ref, t):
        return pltpu.make_async_copy(hn_ref.at[pl.ds(t, 1)], xs_ref.at[pl.ds(pos_ref[0, 0, t], 1)], sem)

    _moe_row_copies((posa_ref, posb_ref), make_copy, hn_ref.shape[0])


def _moe_experts_kernel(te_ref, nu_ref, xs_ref, wg_ref, wu_ref, wd_ref, ys_ref):
    del te_ref
    used = pl.program_id(0) < nu_ref[0]

    @pl.when(used)
    def _():
        hb = xs_ref[...].astype(BF16)
        gate = _dot(hb, wg_ref[0])
        up = _dot(hb, wu_ref[0])
        ys_ref[...] = _dot(gate * jax.nn.sigmoid(gate) * up, wd_ref[0])

    @pl.when(jnp.logical_not(used))
    def _():
        ys_ref[...] = jnp.zeros_like(ys_ref)


def _moe_combine_kernel(posa_ref, posb_ref, x_ref, cw_ref, ys_ref, o_ref, ya_ref, yb_ref, sem):
    def make_copy(pos_ref, t):
        dst = ya_ref if pos_ref is posa_ref else yb_ref
        return pltpu.make_async_copy(ys_ref.at[pl.ds(pos_ref[0, 0, t], 1)], dst.at[pl.ds(t, 1)], sem)

    _moe_row_copies((posa_ref, posb_ref), make_copy, x_ref.shape[0])
    cw = cw_ref[...]
    lane = lax.broadcasted_iota(jnp.int32, (1, LANES), 1).astype(F32)
    routed = cw != 0.0
    ea = jnp.min(jnp.where(routed, lane, 1e9), axis=-1, keepdims=True)
    eb = jnp.max(jnp.where(routed, lane, -1.0), axis=-1, keepdims=True)
    wa = jnp.sum(jnp.where(lane == ea, cw, 0.0), axis=-1, keepdims=True)
    wb = jnp.sum(jnp.where((lane == eb) & (eb != ea), cw, 0.0), axis=-1, keepdims=True)
    o_ref[...] = x_ref[...] + (wa * ya_ref[...] + wb * yb_ref[...])


def _moe_routed(x, hn, cw, w_gate, w_up, w_down, layer, rows_buf=None):
    n, d = x.shape
    _, n_exp, _, d_exp = w_gate.shape
    tm, tr = MOE_TOKEN_TILE, MOE_ROW_TILE
    n_tiles = n // tm
    n_row_tiles = (2 * n + n_exp * (tr - 1)) // tr + 1
    n_rows = n_row_tiles * tr
    assert n % tm == 0 and n_row_tiles <= 2 * LANES
    pos_shape = jax.ShapeDtypeStruct((n_tiles, 1, tm), jnp.int32)
    posa, posb, tile_expert, n_used = pl.pallas_call(
        functools.partial(_moe_plan_kernel, row_tile=tr, n_rows=n_rows),
        grid=(2, n_tiles),
        in_specs=[pl.BlockSpec((tm, LANES), lambda p, i: (i, 0))],
        out_specs=[pl.BlockSpec((1, 1, tm), lambda p, i: (i * p, 0, 0)),
                   pl.BlockSpec((1, 1, tm), lambda p, i: (i * p, 0, 0)),
                   pl.BlockSpec((1, 2 * LANES), lambda p, i: (0, 0)),
                   pl.BlockSpec((1, LANES), lambda p, i: (0, 0))],
        out_shape=[pos_shape, pos_shape, jax.ShapeDtypeStruct((1, 2 * LANES), jnp.int32),
                   jax.ShapeDtypeStruct((1, LANES), jnp.int32)],
        scratch_shapes=[pltpu.VMEM((LANES, 1), F32)] * 3,
        compiler_params=_cparams("arbitrary", "arbitrary"), name="moe_plan",
    )(cw)
    pos_spec = pl.BlockSpec((1, 1, tm), lambda i: (i, 0, 0), memory_space=pltpu.SMEM)
    any_spec = pl.BlockSpec(memory_space=pl.ANY)
    row = lambda w: pl.BlockSpec((tm, w), lambda i: (i, 0))
    xs = pl.pallas_call(
        _moe_dispatch_kernel, grid=(n_tiles,),
        in_specs=[pos_spec, pos_spec, row(d), any_spec],
        out_specs=any_spec, out_shape=jax.ShapeDtypeStruct((n_rows, d), F32),
        scratch_shapes=[pltpu.SemaphoreType.DMA(())],
        input_output_aliases={3: 0},
        compiler_params=_cparams("arbitrary"), name="moe_dispatch",
    )(posa, posb, hn, jnp.zeros((n_rows, d), F32) if rows_buf is None else rows_buf)
    wspec = lambda a, c: pl.BlockSpec((None, 1, a, c), lambda i, te, nu: (layer, te[i], 0, 0))
    ys = pl.pallas_call(
        _moe_experts_kernel,
        grid_spec=pltpu.PrefetchScalarGridSpec(
            num_scalar_prefetch=2, grid=(n_row_tiles,),
            in_specs=[pl.BlockSpec((tr, d), lambda i, te, nu: (i, 0)), wspec(d, d_exp), wspec(d, d_exp),
                      wspec(d_exp, d)],
            out_specs=pl.BlockSpec((tr, d), lambda i, te, nu: (i, 0))),
        out_shape=jax.ShapeDtypeStruct((n_rows, d), F32),
        compiler_params=_cparams("arbitrary"), name="moe_experts",
    )(tile_expert.reshape(-1), n_used.reshape(-1), xs, w_gate, w_up, w_down)
    out = pl.pallas_call(
        _moe_combine_kernel, grid=(n_tiles,),
        in_specs=[pos_spec, pos_spec, row(d), row(LANES), any_spec],
        out_specs=row(d), out_shape=jax.ShapeDtypeStruct((n, d), F32),
        scratch_shapes=[pltpu.VMEM((tm, d), F32), pltpu.VMEM((tm, d), F32), pltpu.SemaphoreType.DMA(())],
        compiler_params=_cparams("arbitrary"), name="moe_combine",
    )(posa, posb, x, cw, ys)
    return out, xs


def _rwkv_prep_body(p, prev, mu_ref, w0_ref, wup_ref, a0_ref, aup_ref, gup_ref, kk_ref, ka_ref, outs):
    r_ref, ld_ref, k_ref, v_ref, kkn_ref, a_ref, g_ref = outs
    x = p + mu_ref[...] * (prev - p)
    c0, c1, c2, c3, c4 = 768, 1536, 2304, 2368, 2432
    r, k, v = x[:, 0:c0], x[:, c0:c1], x[:, c1:c2]
    xw, xa, xg = x[:, c2:c3], x[:, c3:c4], x[:, c4:SHIFT_W]
    z = -(w0_ref[...] + _dot(jnp.tanh(xw), wup_ref[...]))
    softplus = jnp.maximum(z, 0.0) + jnp.log(1.0 + jnp.exp(-jnp.abs(z)))
    w = -softplus - 0.5
    a = jax.nn.sigmoid(a0_ref[...] + _dot(xa, aup_ref[...]))
    r_ref[...] = r
    ld_ref[...] = -jnp.exp(w)
    k_ref[...] = k * (1.0 + (a - 1.0) * ka_ref[...])
    v_ref[...] = v
    a_ref[...] = a
    g_ref[...] = _dot(jax.nn.sigmoid(xg), gup_ref[...])
    kk = k * kk_ref[...]
    for h in range(TOK_HEADS):
        lanes = slice(h * HEAD_DIM, (h + 1) * HEAD_DIM)
        seg = kk[:, lanes]
        nrm = jnp.sqrt(jnp.sum(seg * seg, axis=-1, keepdims=True))
        kkn_ref[:, lanes] = seg / jnp.maximum(nrm, 1e-12)


def _rwkv_prep_prompt_kernel(p_ref, mu_ref, w0_ref, wup_ref, a0_ref, aup_ref, gup_ref, kk_ref, ka_ref,
                             *rest):
    outs, carry = rest[:7], rest[7]
    tm = p_ref.shape[1]

    @pl.when(pl.program_id(1) == 0)
    def _():
        carry[...] = jnp.zeros_like(carry)

    p = p_ref[0]
    first = lax.broadcasted_iota(jnp.int32, (tm, 1), 0) == 0
    prev = jnp.where(first, carry[...], pltpu.roll(p, 1, 0))
    carry[...] = p[tm - 1:tm]
    _rwkv_prep_body(p, prev, mu_ref, w0_ref, wup_ref, a0_ref, aup_ref, gup_ref, kk_ref, ka_ref,
                    [o.at[0] for o in outs])


def _rwkv_prep_sample_kernel(p_ref, prev_ref, mu_ref, w0_ref, wup_ref, a0_ref, aup_ref, gup_ref, kk_ref,
                             ka_ref, *outs):
    _rwkv_prep_body(p_ref[...], prev_ref[...], mu_ref, w0_ref, wup_ref, a0_ref, aup_ref, gup_ref, kk_ref,
                    ka_ref, outs)


def _rwkv_weight_args(mu, w0, w_up, a0, a_up, g_up, k_k, k_a):
    row = lambda v: v.reshape(1, -1)
    return (row(mu), row(w0), w_up, row(a0), a_up, g_up, row(k_k), row(k_a))


def _rwkv_prep_prompt(pr, wargs, tm):
    b, s, _ = pr.shape
    tm = min(tm, s)
    const = lambda a: pl.BlockSpec(a.shape, lambda i, j: (0, 0))
    ospec = pl.BlockSpec((1, tm, QA), lambda i, j: (i, j, 0))
    return pl.pallas_call(
        _rwkv_prep_prompt_kernel, grid=(b, s // tm),
        in_specs=[pl.BlockSpec((1, tm, SHIFT_W), lambda i, j: (i, j, 0))] + [const(a) for a in wargs],
        out_specs=[ospec] * 7, out_shape=[jax.ShapeDtypeStruct((b, s, QA), F32)] * 7,
        scratch_shapes=[pltpu.VMEM((1, SHIFT_W), F32)],
        compiler_params=_cparams("parallel", "arbitrary"), name="rwkv_prep_prompt",
    )(pr, *wargs)


def _rwkv_prep_sample(pr, prev, wargs):
    n = pr.shape[0]
    full = lambda a: pl.BlockSpec(a.shape, lambda i: (0, 0))
    return pl.pallas_call(
        _rwkv_prep_sample_kernel, grid=(1,),
        in_specs=[full(pr), full(prev)] + [full(a) for a in wargs],
        out_specs=[pl.BlockSpec((n, QA), lambda i: (0, 0))] * 7,
        out_shape=[jax.ShapeDtypeStruct((n, QA), F32)] * 7,
        compiler_params=_cparams("arbitrary"), name="rwkv_prep_sample",
    )(pr, prev, *wargs)


def _rwkv_finish(y, r, k, v, g, rk, lnw, lnb):
    m = jnp.mean(y, axis=-1, keepdims=True)
    var = jnp.mean(jnp.square(y - m), axis=-1, keepdims=True)
    yn = (y - m) * lax.rsqrt(var + RWKV_GN_EPS) * lnw + lnb
    bonus = jnp.sum(r * k * rk, axis=-1, keepdims=True) * v
    return (yn + bonus) * g


def _rwkv_chunk_kernel(r_ref, ld_ref, k_ref, v_ref, kk_ref, a_ref, g_ref, rk_ref, lnw_ref, lnb_ref,
                       o_ref, st_ref, z_ref):
    c = r_ref.shape[1]
    ci = pl.program_id(1)

    @pl.when(ci == 0)
    def _():
        z_ref[...] = jnp.zeros_like(z_ref)

    ri = lax.broadcasted_iota(jnp.int32, (c, c), 0)
    cj = lax.broadcasted_iota(jnp.int32, (c, c), 1)
    incl = ri >= cj
    strict = ri > cj
    n = HEAD_DIM
    eye = (lax.broadcasted_iota(jnp.int32, (n, n), 0) == lax.broadcasted_iota(jnp.int32, (n, n), 1)).astype(F32)
    ex = RWKV_CHUNK_EXACT
    hl = [slice(h * n, (h + 1) * n) for h in range(TOK_HEADS)]
    prep = []
    for bi in range(r_ref.shape[0]):
        ld, r_all, k_all, kk_all = ld_ref[bi], r_ref[bi], k_ref[bi], kk_ref[bi]
        lc = _dot(incl.astype(F32), ld, exact=True)
        l_end = lc[c - 1:c]
        b_all = kk_all * a_ref[bi]
        e_neg = jnp.exp(-lc)
        e_rem = jnp.exp(l_end - lc)
        prep.append(dict(
            r=r_all, k=k_all, v=v_ref[bi],
            at=-kk_all * jnp.exp(lc - ld),
            rt=r_all * jnp.exp(lc),
            bt=b_all * e_neg, kt=k_all * e_neg,
            bh=b_all * e_rem, kh=k_all * e_rem,
            p_end=jnp.exp(l_end)))
    units = [(bi, h) for bi in range(r_ref.shape[0]) for h in range(TOK_HEADS)]
    col = lambda name, u: prep[u[0]][name][:, hl[u[1]]]
    v = [col("v", u) for u in units]
    big = [_dot_nt(jnp.concatenate([col("at", u), col("rt", u)], axis=0),
                   jnp.concatenate([col("bt", u), col("kt", u)], axis=0), ex) for u in units]
    a_rbk = [jnp.concatenate([jnp.where(incl, m[c:2 * c, 0:c], 0.0), jnp.where(incl, m[c:2 * c, c:2 * c], 0.0)],
                             axis=1) for m in big]
    akv = [_dot(jnp.where(strict, m[0:c, c:2 * c], 0.0), vv, ex) for m, vv in zip(big, v)]
    x = [jnp.concatenate([col("at", u), t], axis=1) for u, t in zip(units, akv)]
    npow = [jnp.where(strict, m[0:c, 0:c], 0.0) for m in big]
    x = [xx + _dot(m, xx, ex) for m, xx in zip(npow, x)]
    steps = 1
    while 2 * steps < c:
        npow = [_dot(m, m, ex) for m in npow]
        x = [xx + _dot(m, xx, ex) for m, xx in zip(npow, x)]
        steps *= 2
    zeros = jnp.zeros((c, n), F32)
    wv = [jnp.concatenate([xx, jnp.concatenate([zeros, vv], axis=1)], axis=0) for xx, vv in zip(x, v)]
    rq_y0 = [_dot(m, w, ex) for m, w in zip(a_rbk, wv)]
    m_n = [_dot_tn(jnp.concatenate([col("bh", u), col("kh", u)], axis=0), w, ex)
           for u, w in zip(units, wv)]
    yz = [_dot(jnp.concatenate([col("rt", u) + rq_y0[i][:, 0:n],
                                m_n[i][:, 0:n] + eye * col("p_end", u)], axis=0), z_ref[u[0], u[1]], ex)
          for i, u in enumerate(units)]
    for i, (bi, h) in enumerate(units):
        z_ref[bi, h] = yz[i][c:c + n] + m_n[i][:, n:2 * n]
        y = yz[i][0:c] + rq_y0[i][:, n:2 * n]
        o_ref[bi, :, hl[h]] = _rwkv_finish(y, col("r", (bi, h)), col("k", (bi, h)), v[i], g_ref[bi, :, hl[h]],
                                           rk_ref[:, hl[h]], lnw_ref[:, hl[h]], lnb_ref[:, hl[h]])

    @pl.when(ci == pl.num_programs(1) - 1)
    def _():
        for bi, h in units:
            st_ref[bi, h] = z_ref[bi, h].T


def _rwkv_chunk(r, ld, k, v, kk, a, g, r_k, ln_w, ln_b):
    b, s, _ = r.shape
    c = RWKV_CHUNK
    nb = RWKV_ROWS_PER_STEP if b % RWKV_ROWS_PER_STEP == 0 else 1
    tok = pl.BlockSpec((nb, c, QA), lambda i, j: (i, j, 0))
    const = pl.BlockSpec((1, QA), lambda i, j: (0, 0))
    return pl.pallas_call(
        _rwkv_chunk_kernel, grid=(b // nb, s // c),
        in_specs=[tok] * 7 + [const] * 3,
        out_specs=[tok, pl.BlockSpec((nb, TOK_HEADS, HEAD_DIM, HEAD_DIM), lambda i, j: (i, 0, 0, 0))],
        out_shape=[jax.ShapeDtypeStruct((b, s, QA), F32),
                   jax.ShapeDtypeStruct((b, TOK_HEADS, HEAD_DIM, HEAD_DIM), F32)],
        scratch_shapes=[pltpu.VMEM((nb, TOK_HEADS, HEAD_DIM, HEAD_DIM), F32)],
        compiler_params=_cparams("parallel", "arbitrary"), name="rwkv_chunk",
    )(r, ld, k, v, kk, a, g, r_k.reshape(1, QA), ln_w.reshape(1, QA), ln_b.reshape(1, QA))


def _rwkv_step_kernel(r_ref, ld_ref, k_ref, v_ref, kk_ref, a_ref, g_ref, rk_ref, lnw_ref, lnb_ref, s_ref,
                      o_ref, so_ref):
    n = HEAD_DIM
    eye = (lax.broadcasted_iota(jnp.int32, (n, n), 0) == lax.broadcasted_iota(jnp.int32, (n, n), 1)).astype(F32)
    units = [(bi, slice(h * n, (h + 1) * n), h) for bi in range(r_ref.shape[0]) for h in range(TOK_HEADS)]
    row = lambda ref: [ref[bi, :, lanes] for bi, lanes, _ in units]
    r, k, v, kk, a, ld = row(r_ref), row(k_ref), row(v_ref), row(kk_ref), row(a_ref), row(ld_ref)
    s = [s_ref[bi, h] for bi, _, h in units]
    sa = [jnp.sum(_bf16_round(si) * _bf16_round(-kki), axis=1, keepdims=True) for si, kki in zip(s, kk)]
    v_col = [jnp.sum(eye * vi, axis=1, keepdims=True) for vi in v]
    s_new = [si * jnp.exp(ldi) + sai * (kki * ai) + vci * ki
             for si, ldi, sai, kki, ai, vci, ki in zip(s, ld, sa, kk, a, v_col, k)]
    y_col = [jnp.sum(_bf16_round(si) * _bf16_round(ri), axis=1, keepdims=True) for si, ri in zip(s_new, r)]
    y = [jnp.sum(eye * yc, axis=0, keepdims=True) for yc in y_col]
    for i, (bi, lanes, h) in enumerate(units):
        so_ref[bi, h] = s_new[i]
        o_ref[bi, :, lanes] = _rwkv_finish(y[i], r[i], k[i], v[i], g_ref[bi, :, lanes], rk_ref[:, lanes],
                                           lnw_ref[:, lanes], lnb_ref[:, lanes])


def _rwkv_step(r, ld, k, v, kk, a, g, r_k, ln_w, ln_b, state):
    n = r.shape[0]
    nb = RWKV_STEP_ROWS if n % RWKV_STEP_ROWS == 0 else 1
    tok = pl.BlockSpec((nb, 1, QA), lambda i: (i, 0, 0))
    const = pl.BlockSpec((1, QA), lambda i: (0, 0))
    st = pl.BlockSpec((nb, TOK_HEADS, HEAD_DIM, HEAD_DIM), lambda i: (i, 0, 0, 0))
    r3 = lambda z: z.reshape(n, 1, QA)
    return pl.pallas_call(
        _rwkv_step_kernel, grid=(n // nb,),
        in_specs=[tok] * 7 + [const] * 3 + [st],
        out_specs=[tok, st],
        out_shape=[jax.ShapeDtypeStruct((n, 1, QA), F32), jax.ShapeDtypeStruct(state.shape, F32)],
        compiler_params=_cparams("parallel"), name="rwkv_step",
    )(r3(r), r3(ld), r3(k), r3(v), r3(kk), r3(a), r3(g), r_k.reshape(1, QA), ln_w.reshape(1, QA),
      ln_b.reshape(1, QA), state)


def _split_w_in_a(w):
    offs = [0, QA]
    for _ in range(6):
        offs.append(offs[-1] + KVW)
    offs.append(offs[-1] + 3 * TOK_HEADS)
    offs.append(offs[-1] + MEMQ)
    pieces = [w[:, offs[i]:offs[i + 1]] for i in range(9)]
    pieces[7] = jnp.pad(pieces[7], ((0, 0), (0, LANES - 3 * TOK_HEADS)))
    return pieces


_A_OPS = ("hnorm", "none", "none", "hnorm", "none", "hnorm", "none", "sigmoid", "none")


def _ffn(x, mix, mem, w_out, g_ffn, wc, bc, wf, bf, w_gate, w_up, w_down, layer, tm_proj, tm_moe, rows_buf=None):
    x_new, hn, cw = _outproj_router(x, mix, mem, w_out, g_ffn, wc, bc, wf, bf, tm_proj)
    if x.shape[0] % MOE_TOKEN_TILE == 0:
        return _moe_routed(x_new, hn, cw, w_gate, w_up, w_down, layer, rows_buf)
    return _moe(x_new, hn, cw, w_gate, w_up, w_down, layer, tm_moe), None


def kernel(x_prompt, x_sample, cache_cmp_k, cache_cmp_v, cache_sel_k, cache_sel_v, cache_win_k, cache_win_v, cache_mem_k, cache_mem_v, state_rwkv, state_shift, page_table, mem_prompt, norm_mix, norm_ffn, norm_mem, w_mem_kv, mem_q_gain, mem_k_gain, w_in_a, nsa_q_gain, nsa_k_gain, cmp_pe, cmp_w1, cmp_b1, cmp_w2, w_in_b, rwkv_mu, rwkv_w0, rwkv_w_up, rwkv_a0, rwkv_a_up, rwkv_g_up, rwkv_k_k, rwkv_k_a, rwkv_r_k, rwkv_ln_w, rwkv_ln_b, w_out, moe_w_coarse, moe_b_coarse, moe_w_fine, moe_b_fine, moe_w_gate, moe_w_up, moe_w_down):
    b, s, d = x_prompt.shape
    bd = x_sample.shape[0]
    depth = norm_mix.shape[0]
    m_len = mem_prompt.shape[1]
    wl = min(WINDOW, s)
    xp = x_prompt.reshape(b * s, d)
    xs = x_sample.reshape(bd, d)
    mem2 = mem_prompt.reshape(b * m_len, d)
    outs = {name: [] for name in ("pc_k", "pc_v", "ps_k", "ps_v", "pw_k", "pw_v", "pm_k", "pm_v", "pr_s", "pr_x",
                                  "sc_k", "sc_v", "ss_k", "ss_v", "sw_k", "sw_v", "sr_s", "sr_x")}
    rows_buf = None
    for i in range(depth):
        km_p, vm_p = _norm_proj(mem2, norm_mem[i], [w_mem_kv[i][:, :MEMQ], w_mem_kv[i][:, MEMQ:]],
                                ("hnorm", "none"), [mem_k_gain[i]], 256)
        km_p, vm_p = km_p.reshape(b, m_len, MEMQ), vm_p.reshape(b, m_len, MEMQ)
        outs["pm_k"].append(km_p.reshape(b, m_len, MEM_HEADS, HEAD_DIM))
        outs["pm_v"].append(vm_p.reshape(b, m_len, MEM_HEADS, HEAD_DIM))
        if i % 2 == 0:
            ia = i // 2
            pieces = _split_w_in_a(w_in_a[ia])
            gains = [nsa_q_gain[ia], nsa_k_gain[ia, 1], nsa_k_gain[ia, 2]]
            cmp_args = (cmp_pe[ia], cmp_w1[ia], cmp_b1[ia], cmp_w2[ia], nsa_k_gain[ia, 0])
            q, kc, vc, ks, vs, kw, vw, gt, mq_p = _norm_proj(xp, norm_mix[i], pieces, _A_OPS, gains, PROJ_ROWS)
            r3 = lambda z: z.reshape(b, s, -1)
            kc, vc, ks, vs, kw, vw = (r3(z) for z in (kc, vc, ks, vs, kw, vw))
            ck, cv = _compress_prompt(kc, vc, *cmp_args)
            mix_p = _nsa_prompt(r3(q), r3(gt), ck, cv, ks, vs, kw, vw).reshape(b * s, QA)
            r5 = lambda z: z.reshape(b, -1, KV_GROUPS, HEAD_DIM)
            for name, z in (("pc_k", kc), ("pc_v", vc), ("ps_k", ks), ("ps_v", vs),
                            ("pw_k", kw[:, s - wl:]), ("pw_v", vw[:, s - wl:])):
                outs[name].append(r5(z))
            q, kc, vc, ks, vs, kw, vw, gt, mq_s = _norm_proj(xs, norm_mix[i], pieces, _A_OPS, gains, 256)
            ck, cv = _compress_sample(page_table, cache_cmp_k[ia], cache_cmp_v[ia], *cmp_args)
            mix_s, nwk, nwv = _nsa_sample(page_table, q, gt, ck, cv, ks, vs, kw, vw,
                                          cache_sel_k[ia], cache_sel_v[ia], cache_win_k[ia], cache_win_v[ia])
            mix_s = mix_s.reshape(bd, QA)
            r5 = lambda z: z.reshape(bd, -1, KV_GROUPS, HEAD_DIM)
            for name, z in (("sc_k", kc), ("sc_v", vc), ("ss_k", ks), ("ss_v", vs), ("sw_k", nwk), ("sw_v", nwv)):
                outs[name].append(r5(z))
        else:
            ib = i // 2
            pieces = [w_in_b[ib][:, :SHIFT_W], w_in_b[ib][:, SHIFT_W:]]
            wargs = _rwkv_weight_args(rwkv_mu[ib], rwkv_w0[ib], rwkv_w_up[ib], rwkv_a0[ib], rwkv_a_up[ib],
                                      rwkv_g_up[ib], rwkv_k_k[ib], rwkv_k_a[ib])
            fin = (rwkv_r_k[ib].reshape(-1), rwkv_ln_w[ib], rwkv_ln_b[ib])
            pr, mq_p = _norm_proj(xp, norm_mix[i], pieces, ("none", "none"), [], PROJ_ROWS)
            pr = pr.reshape(b, s, SHIFT_W)
            prep = _rwkv_prep_prompt(pr, wargs, 256)
            mix_p, st_p = _rwkv_chunk(*prep, *fin)
            mix_p = mix_p.reshape(b * s, QA)
            outs["pr_s"].append(st_p)
            outs["pr_x"].append(pr[:, s - 1])
            pr, mq_s = _norm_proj(xs, norm_mix[i], pieces, ("none", "none"), [], 256)
            prep = _rwkv_prep_sample(pr, state_shift[ib], wargs)
            mix_s, st_s = _rwkv_step(*prep, *fin, state_rwkv[ib])
            mix_s = mix_s.reshape(bd, QA)
            outs["sr_s"].append(st_s)
            outs["sr_x"].append(pr)
        mem_p = _mem_attn(mq_p.reshape(b, s, MEMQ), km_p, vm_p, mem_q_gain[i], MEM_Q_ROWS).reshape(b * s, MEMQ)
        mem_s = _mem_attn(mq_s.reshape(bd, 1, MEMQ), cache_mem_k[i].reshape(bd, -1, MEMQ),
                          cache_mem_v[i].reshape(bd, -1, MEMQ), mem_q_gain[i], 1).reshape(bd, MEMQ)
        ffn_w = (w_out[i], norm_ffn[i], moe_w_coarse[i], moe_b_coarse[i], moe_w_fine[i], moe_b_fine[i],
                 moe_w_gate, moe_w_up, moe_w_down, i)
        xp, rows_buf = _ffn(xp, mix_p, mem_p, *ffn_w, PROJ_ROWS, 1024, rows_buf)
        xs, _ = _ffn(xs, mix_s, mem_s, *ffn_w, PROJ_ROWS, 1024)
    order = ("pc_k", "pc_v", "ps_k", "ps_v", "pw_k", "pw_v", "pm_k", "pm_v", "pr_s", "pr_x",
             "sc_k", "sc_v", "ss_k", "ss_v", "sw_k", "sw_v", "sr_s", "sr_x")
    return (xp.reshape(b, s, d), xs.reshape(bd, 1, d)) + tuple(jnp.stack(outs[name]) for name in order)
```

```python
import functools

import jax
import jax.numpy as jnp
import numpy as np
from jax import lax
from jax.experimental import pallas as pl
from jax.experimental.pallas import tpu as pltpu

F32 = jnp.float32
BF16 = jnp.bfloat16
HI = lax.Precision.HIGHEST

HEAD_DIM = 64
TOK_HEADS = 12
MEM_HEADS = 4
KV_GROUPS = 3
HEADS_PER_GROUP = TOK_HEADS // KV_GROUPS
QA = TOK_HEADS * HEAD_DIM
KVW = KV_GROUPS * HEAD_DIM
MEMQ = MEM_HEADS * HEAD_DIM
CMP_LEN = 32
CMP_STRIDE = 16
CMP_HID = 128
SEL_LEN = 64
SEL_TOPK = 16
WINDOW = 512
Q_BLOCK = 256
MEM_Q_ROWS = 1024
PROJ_ROWS = 1024
NSA_KEY_STEP = 512
PAGE = 128
N_GROUPS = 4
E_PER_GROUP = 8
N_EXPERTS = 32
MOE_TOKEN_TILE = 1024
MOE_ROW_TILE = 512
RWKV_COLS = (768, 768, 768, 64, 64, 128)
SHIFT_W = sum(RWKV_COLS)
RWKV_GN_EPS = 64e-5
RWKV_CHUNK = 64
RWKV_STEP_ROWS = 4
RWKV_ROWS_PER_STEP = 2
RWKV_CHUNK_EXACT = False
NEG = -1e30
BIG = 1e30
LANES = 128
VMEM_LIMIT = 56 * 1024 * 1024


def _cparams(*sem):
    return pltpu.CompilerParams(dimension_semantics=sem, vmem_limit_bytes=VMEM_LIMIT)


def _dot_general(a, b, dims, exact):
    if exact:
        return lax.dot_general(a, b, (dims, ((), ())), precision=HI, preferred_element_type=F32)
    return lax.dot_general(a.astype(BF16), b.astype(BF16), (dims, ((), ())), preferred_element_type=F32)


def _dot(a, b, exact=False):
    return _dot_general(a, b, ((1,), (0,)), exact)


def _dot_nt(a, b, exact=False):
    return _dot_general(a, b, ((1,), (1,)), exact)


def _dot_tn(a, b, exact=False):
    return _dot_general(a, b, ((0,), (0,)), exact)


def _bf16_round(x):
    return x.astype(BF16).astype(F32)


def _rms(x, g, eps=1e-6):
    return x * lax.rsqrt(jnp.mean(x * x, axis=-1, keepdims=True) + eps) * g


def _masked_softmax(s, mask):
    s = jnp.where(mask, s, NEG)
    m = jnp.max(s, axis=-1, keepdims=True)
    e = jnp.where(mask, jnp.exp(s - m), 0.0)
    return e / jnp.maximum(jnp.sum(e, axis=-1, keepdims=True), 1e-30)


def _alibi_slope(h):
    return float(np.exp2(np.float32(-8.0) * np.float32(h + 1) / np.float32(TOK_HEADS)))


def _topk_mask(score, k, n_real):
    lane = lax.broadcasted_iota(jnp.int32, score.shape, 1)
    ahead = jnp.zeros(score.shape, F32)
    for j in range(n_real):
        col = score[:, j:j + 1]
        ahead = ahead + jnp.where((col > score) | ((col == score) & (lane > j)), 1.0, 0.0)
    return jnp.where(ahead < k, 1.0, 0.0)


def _topk_rows_t(score_t, k):
    n_blk = score_t.shape[0]
    idx = lax.broadcasted_iota(jnp.int32, score_t.shape, 0)
    ahead = jnp.zeros(score_t.shape, F32)
    for j in range(n_blk):
        row = score_t[j:j + 1]
        ahead = ahead + jnp.where((row > score_t) | ((row == score_t) & (idx > j)), 1.0, 0.0)
    return jnp.where(ahead < k, 1.0, 0.0)


def _sel_to_cmp(n_sel_rows, n_cmp_lanes):
    s0 = lax.broadcasted_iota(jnp.int32, (n_sel_rows, n_cmp_lanes), 0) * SEL_LEN
    c0 = lax.broadcasted_iota(jnp.int32, (n_sel_rows, n_cmp_lanes), 1) * CMP_STRIDE
    return ((c0 < s0 + SEL_LEN) & (c0 + CMP_LEN > s0)).astype(F32)


def _cmp_to_sel(n_cmp_rows, n_sel_lanes):
    n_i = lax.broadcasted_iota(jnp.int32, (n_cmp_rows, n_sel_lanes), 0)
    s_i = lax.broadcasted_iota(jnp.int32, (n_cmp_rows, n_sel_lanes), 1)
    c0 = n_i * CMP_STRIDE
    s0 = s_i * SEL_LEN
    return ((c0 < s0 + SEL_LEN) & (c0 + CMP_LEN > s0)).astype(F32)


def _block_expand(n_blk_rows, n_keys):
    b_i = lax.broadcasted_iota(jnp.int32, (n_blk_rows, n_keys), 0)
    k_i = lax.broadcasted_iota(jnp.int32, (n_blk_rows, n_keys), 1)
    return (jnp.right_shift(k_i, 6) == b_i).astype(BF16)


def _norm_proj_kernel(ops, x_ref, g_ref, *refs):
    n = len(ops)
    n_gain = sum(op == "hnorm" for op in ops)
    w_refs, gain_refs, o_refs = refs[:n], refs[n:n + n_gain], refs[n + n_gain:]
    h = _rms(x_ref[...], g_ref[...]).astype(BF16)
    gi = 0
    for op, w_ref, o_ref in zip(ops, w_refs, o_refs):
        z = _dot(h, w_ref[...])
        if op == "hnorm":
            gain2 = gain_refs[gi][...]
            gi += 1
            low = lax.broadcasted_iota(jnp.int32, (1, LANES), 1) < HEAD_DIM
            for t0 in range(0, z.shape[1], LANES):
                zt = z[:, t0:t0 + LANES]
                sq = zt * zt
                if zt.shape[1] == LANES:
                    ss = jnp.where(low, jnp.sum(jnp.where(low, sq, 0.0), axis=-1, keepdims=True),
                                   jnp.sum(jnp.where(low, 0.0, sq), axis=-1, keepdims=True))
                else:
                    ss = jnp.sum(sq, axis=-1, keepdims=True)
                o_ref[:, t0:t0 + LANES] = zt * lax.rsqrt(ss * (1.0 / HEAD_DIM) + 1e-6) * gain2[:, 0:zt.shape[1]]
        elif op == "sigmoid":
            o_ref[...] = jax.nn.sigmoid(z)
        else:
            o_ref[...] = z


def _norm_proj(x, g, weights, ops, gains, tm):
    n_rows, d = x.shape
    tm = min(tm, n_rows)
    assert n_rows % tm == 0
    in_specs = [pl.BlockSpec((tm, d), lambda i: (i, 0)), pl.BlockSpec((1, d), lambda i: (0, 0))]
    in_specs += [pl.BlockSpec(w.shape, lambda i: (0, 0)) for w in weights]
    in_specs += [pl.BlockSpec((1, LANES), lambda i: (0, 0)) for _ in gains]
    out_shape = [jax.ShapeDtypeStruct((n_rows, w.shape[1]), F32) for w in weights]
    out_specs = [pl.BlockSpec((tm, w.shape[1]), lambda i: (i, 0)) for w in weights]
    return pl.pallas_call(
        functools.partial(_norm_proj_kernel, tuple(ops)),
        grid=(n_rows // tm,), in_specs=in_specs, out_specs=out_specs, out_shape=out_shape,
        compiler_params=_cparams("parallel"), name="norm_proj",
    )(x, g.reshape(1, d), *[w.astype(BF16) for w in weights],
      *[jnp.tile(gn, LANES // HEAD_DIM).reshape(1, LANES) for gn in gains])


def _compress_rows(rows_ref, pe_ref, w1_ref, b1_ref, w2_ref, gain, o_ref, u_ref, n_chunk):
    for g in range(KV_GROUPS):
        for l in range(CMP_STRIDE):
            src = l * KVW + g * HEAD_DIM
            u_ref[:, l * HEAD_DIM:(l + 1) * HEAD_DIM] = rows_ref[:, src:src + HEAD_DIM]
        _compress_mlp(u_ref, pe_ref, w1_ref, b1_ref, w2_ref, gain, o_ref, g, n_chunk)


def _compress_mlp(u_ref, pe_ref, w1_ref, b1_ref, w2_ref, gain, o_ref, g, n_chunk):
    half = CMP_STRIDE * HEAD_DIM
    u = u_ref[...]
    p0 = _dot(u + pe_ref[:, 0:half], w1_ref[0:half, :])
    p1 = _dot(u + pe_ref[:, half:2 * half], w1_ref[half:2 * half, :])
    hid = b1_ref[...] + p0 + pltpu.roll(p1, n_chunk - 1, 0)
    z = _dot(jax.nn.gelu(hid), w2_ref[...])
    if gain is not None:
        z = _rms(z, gain)
    o_ref[:, g * HEAD_DIM:(g + 1) * HEAD_DIM] = z


def _compress_cols(xt_ref, pe_ref, w1_ref, b1_ref, w2_ref, gain, o_ref, xs_ref, u_ref, n_chunk):
    half = CMP_STRIDE * 2 * HEAD_DIM
    lane = lax.broadcasted_iota(jnp.int32, (1, 2 * HEAD_DIM), 1)
    for first in (0, HEAD_DIM):
        xs_ref[...] = xt_ref[first:first + 2 * HEAD_DIM, :].T
        for l in range(CMP_STRIDE):
            u_ref[:, l * 2 * HEAD_DIM:(l + 1) * 2 * HEAD_DIM] = xs_ref[pl.ds(l, n_chunk, stride=CMP_STRIDE), :]
        u = u_ref[...]
        p0 = _dot(u + pe_ref[:, 0:half], w1_ref[0:half, :])
        p1 = _dot(u + pe_ref[:, half:2 * half], w1_ref[half:2 * half, :])
        hid = b1_ref[...] + p0 + pltpu.roll(p1, n_chunk - 1, 0)
        z = _dot(jax.nn.gelu(hid), w2_ref[...])
        if gain is not None:
            sq = z * z
            ss_lo = jnp.sum(jnp.where(lane < HEAD_DIM, sq, 0.0), axis=-1, keepdims=True)
            ss_hi = jnp.sum(jnp.where(lane < HEAD_DIM, 0.0, sq), axis=-1, keepdims=True)
            ms = jnp.where(lane < HEAD_DIM, ss_lo, ss_hi) * (1.0 / HEAD_DIM)
            z = z * lax.rsqrt(ms + 1e-6) * gain
        if first == 0:
            o_ref[:, 0:2 * HEAD_DIM] = z
        else:
            o_ref[:, 2 * HEAD_DIM:3 * HEAD_DIM] = z[:, HEAD_DIM:2 * HEAD_DIM]


def _compress_prompt_kernel(k_ref, v_ref, pe_ref, w1_ref, b1_ref, w2_ref, gain_ref, ck_ref, cv_ref, u_ref,
                            *, n_chunk):
    _compress_rows(k_ref.at[0], pe_ref.at[0], w1_ref.at[0], b1_ref.at[0], w2_ref.at[0], gain_ref[...],
                   ck_ref.at[0], u_ref, n_chunk)
    _compress_rows(v_ref.at[0], pe_ref.at[1], w1_ref.at[1], b1_ref.at[1], w2_ref.at[1], None,
                   cv_ref.at[0], u_ref, n_chunk)


def _compress_weight_specs(nidx):
    zero = lambda *_: (0, 0, 0)
    del nidx
    return [pl.BlockSpec((2, 1, CMP_LEN * HEAD_DIM), zero),
            pl.BlockSpec((2, CMP_LEN * HEAD_DIM, CMP_HID), zero),
            pl.BlockSpec((2, 1, CMP_HID), zero),
            pl.BlockSpec((2, CMP_HID, HEAD_DIM), zero)]


def _compress_prompt(kc, vc, pe, w1, b1, w2, gain):
    b, s, _ = kc.shape
    n_chunk = s // CMP_STRIDE
    cw = CMP_STRIDE * KVW
    rows = pl.BlockSpec((1, n_chunk, cw), lambda i: (i, 0, 0))
    out = pl.BlockSpec((1, n_chunk, KVW), lambda i: (i, 0, 0))
    return pl.pallas_call(
        functools.partial(_compress_prompt_kernel, n_chunk=n_chunk),
        grid=(b,),
        in_specs=[rows, rows] + _compress_weight_specs(1) + [pl.BlockSpec((1, HEAD_DIM), lambda i: (0, 0))],
        out_specs=[out, out],
        out_shape=[jax.ShapeDtypeStruct((b, n_chunk, KVW), F32)] * 2,
        scratch_shapes=[pltpu.VMEM((n_chunk, CMP_STRIDE * HEAD_DIM), F32)],
        compiler_params=_cparams("parallel"), name="compress_prompt",
    )(kc.reshape(b, n_chunk, cw), vc.reshape(b, n_chunk, cw), pe.reshape(2, 1, -1), w1, b1.reshape(2, 1, -1), w2,
      gain.reshape(1, HEAD_DIM))


def _compress_sample_kernel(pt_ref, pk_ref, pv_ref, pe_ref, w1_ref, b1_ref, w2_ref, gain_ref,
                            ck_ref, cv_ref, kbuf, vbuf, xs_ref, u_ref, sem, *, n_pages):
    b = pl.program_id(0)
    n_chunk = n_pages * PAGE // CMP_STRIDE

    def k_copies(row_b):
        return [pltpu.make_async_copy(pk_ref.at[pt_ref[row_b, j]], kbuf.at[:, pl.ds(j * PAGE, PAGE)], sem.at[0])
                for j in range(n_pages)]

    v_copies = [pltpu.make_async_copy(pv_ref.at[pt_ref[b, j]], vbuf.at[:, pl.ds(j * PAGE, PAGE)], sem.at[1])
                for j in range(n_pages)]

    @pl.when(b == 0)
    def _():
        for cp in k_copies(b):
            cp.start()

    for cp in v_copies:
        cp.start()
    for cp in k_copies(b):
        cp.wait()
    _compress_cols(kbuf, pe_ref.at[0], w1_ref.at[0], b1_ref.at[0], w2_ref.at[0], gain_ref[...],
                   ck_ref.at[0], xs_ref, u_ref, n_chunk)

    @pl.when(b + 1 < pl.num_programs(0))
    def _():
        for cp in k_copies(b + 1):
            cp.start()

    for cp in v_copies:
        cp.wait()
    _compress_cols(vbuf, pe_ref.at[1], w1_ref.at[1], b1_ref.at[1], w2_ref.at[1], None,
                   cv_ref.at[0], xs_ref, u_ref, n_chunk)


def _compress_sample(page_table, pool_k, pool_v, pe, w1, b1, w2, gain):
    bd, n_pages = page_table.shape
    n_pool = pool_k.shape[0]
    n_chunk = n_pages * PAGE // CMP_STRIDE
    past = n_pages * PAGE
    pos_minor = lambda z: jnp.transpose(z, (0, 2, 3, 1)).reshape(n_pool, KVW, PAGE)
    eye2 = jnp.eye(2, dtype=F32)
    n_l, dh, hid = CMP_STRIDE, HEAD_DIM, CMP_HID
    pe2 = jnp.broadcast_to(pe.reshape(2, 2, n_l, 1, dh), (2, 2, n_l, 2, dh)).reshape(2, 1, 4 * n_l * dh)
    w1x = jnp.einsum("krldf,gh->krlgdhf", w1.reshape(2, 2, n_l, dh, hid), eye2).reshape(2, 4 * n_l * dh, 2 * hid)
    b1x = jnp.tile(b1, (1, 2)).reshape(2, 1, 2 * hid)
    w2x = jnp.einsum("kfd,gh->kgfhd", w2, eye2).reshape(2, 2 * hid, 2 * dh)
    zero3 = lambda i, pt: (0, 0, 0)
    any_spec = pl.BlockSpec(memory_space=pl.ANY)
    out = pl.BlockSpec((1, n_chunk, KVW), lambda i, pt: (i, 0, 0))
    grid_spec = pltpu.PrefetchScalarGridSpec(
        num_scalar_prefetch=1, grid=(bd,),
        in_specs=[any_spec, any_spec,
                  pl.BlockSpec(pe2.shape, zero3), pl.BlockSpec(w1x.shape, zero3),
                  pl.BlockSpec(b1x.shape, zero3), pl.BlockSpec(w2x.shape, zero3),
                  pl.BlockSpec((1, 2 * dh), lambda i, pt: (0, 0))],
        out_specs=[out, out],
        scratch_shapes=[pltpu.VMEM((KVW, past), F32), pltpu.VMEM((KVW, past), F32),
                        pltpu.VMEM((past, LANES), F32),
                        pltpu.VMEM((n_chunk, 2 * n_l * dh), F32), pltpu.SemaphoreType.DMA((2,))])
    return pl.pallas_call(
        functools.partial(_compress_sample_kernel, n_pages=n_pages),
        grid_spec=grid_spec,
        out_shape=[jax.ShapeDtypeStruct((bd, n_chunk, KVW), F32)] * 2,
        compiler_params=_cparams("arbitrary"), name="compress_sample",
    )(page_table, pos_minor(pool_k), pos_minor(pool_v),
      pe2, w1x.astype(BF16), b1x, w2x.astype(BF16), jnp.tile(gain, 2).reshape(1, 2 * dh))


def _softmax_rows(s):
    e = jnp.exp(s - jnp.max(s, axis=-1, keepdims=True))
    return e / jnp.sum(e, axis=-1, keepdims=True)


def _nsa_prompt_block(q_ref, gt_ref, ck_ref, cv_ref, ks_ref, vs_ref, kw_ref, vw_ref, o_ref,
                      *, seq, n_keys, first_block):
    tq = Q_BLOCK
    hpg = HEADS_PER_GROUP
    n_cmp = ck_ref.shape[1]
    n_sel = seq // SEL_LEN
    wlen = min(WINDOW + tq, seq)
    q0 = (pl.program_id(1) + first_block) * tq
    t1 = q0 + lax.broadcasted_iota(jnp.int32, (tq, 1), 0)
    s2c = _sel_to_cmp(n_sel, n_cmp)
    expand = _block_expand(n_sel, n_keys)
    blk = lax.broadcasted_iota(jnp.int32, (n_sel, 1), 0)
    t_row = q0 + lax.broadcasted_iota(jnp.int32, (1, tq), 1)
    cur = jnp.right_shift(t_row, 6)
    forced = (blk == 0) | (blk == cur) | (blk == cur - 1)
    valid = blk * SEL_LEN <= t_row
    dist_c = t1 - (lax.broadcasted_iota(jnp.int32, (1, n_cmp), 1) * CMP_STRIDE + (CMP_LEN - 1))
    mask_c = dist_c >= 0
    dist_cf = dist_c.astype(F32)
    dist_s = t1 - lax.broadcasted_iota(jnp.int32, (1, n_keys), 1)
    causal_s = dist_s >= 0
    dist_sf = dist_s.astype(F32)
    w_start = pl.multiple_of(jnp.clip(q0 - WINDOW, 0, seq - wlen), tq)
    dist_w = t1 - (w_start + lax.broadcasted_iota(jnp.int32, (1, wlen), 1))
    bias_w = jnp.where((dist_w >= 0) & (dist_w <= WINDOW), 0.0, NEG)
    dist_wf = dist_w.astype(F32)
    gt = gt_ref[0]
    for g in range(KV_GROUPS):
        lanes = slice(g * HEAD_DIM, (g + 1) * HEAD_DIM)
        heads = [g * hpg + j for j in range(hpg)]
        rows = [slice(j * tq, (j + 1) * tq) for j in range(hpg)]
        q4 = jnp.concatenate([q_ref[0, :, h * HEAD_DIM:(h + 1) * HEAD_DIM] for h in heads],
                             axis=0) * (HEAD_DIM ** -0.5)
        s_c = _dot_nt(q4, ck_ref[0, :, lanes])
        p_c = [_masked_softmax(s_c[rows[j]] - _alibi_slope(heads[j]) * dist_cf, mask_c) for j in range(hpg)]
        cv = cv_ref[0, :, lanes]
        o_c = [_dot(p, cv) for p in p_c]
        imps = [_dot_nt(s2c, p) for p in p_c]
        imp = imps[0]
        for j in range(1, hpg):
            imp = imp + imps[j]
        score = jnp.where(valid, jnp.where(forced, BIG, imp), -BIG)
        key_sel = _dot_tn(_topk_rows_t(score, min(SEL_TOPK, n_sel)), expand)
        bias_s = jnp.where((key_sel > 0.5) & causal_s, 0.0, NEG)
        s_s = _dot_nt(q4, ks_ref[0, 0:n_keys, lanes])
        vs = vs_ref[0, 0:n_keys, lanes]
        o_s = [_dot(_softmax_rows(s_s[rows[j]] - _alibi_slope(heads[j]) * dist_sf + bias_s), vs)
               for j in range(hpg)]
        s_w = _dot_nt(q4, kw_ref[0, pl.ds(w_start, wlen), lanes])
        vw = vw_ref[0, pl.ds(w_start, wlen), lanes]
        o_w = [_dot(_softmax_rows(s_w[rows[j]] - _alibi_slope(heads[j]) * dist_wf + bias_w), vw)
               for j in range(hpg)]
        for j, h in enumerate(heads):
            o_ref[0, :, h * HEAD_DIM:(h + 1) * HEAD_DIM] = (
                gt[:, 3 * h:3 * h + 1] * o_c[j] + gt[:, 3 * h + 1:3 * h + 2] * o_s[j]
                + gt[:, 3 * h + 2:3 * h + 3] * o_w[j])


def _nsa_prompt(q, gates, ck, cv, ks, vs, kw, vw):
    b, s, _ = q.shape
    n_cmp = ck.shape[1]
    key_step = min(NSA_KEY_STEP, s)
    per_class = key_step // Q_BLOCK
    full = lambda n: pl.BlockSpec((1, n, KVW), lambda i, j: (i, 0, 0))
    outs = []
    for c in range(s // key_step):
        n_keys = (c + 1) * key_step
        first = c * per_class
        qspec = lambda w, first=first: pl.BlockSpec((1, Q_BLOCK, w), lambda i, j: (i, j + first, 0))
        outs.append(pl.pallas_call(
            functools.partial(_nsa_prompt_block, seq=s, n_keys=n_keys, first_block=first),
            grid=(b, per_class),
            in_specs=[qspec(QA), qspec(LANES), full(n_cmp), full(n_cmp), full(n_keys), full(n_keys), full(s), full(s)],
            out_specs=pl.BlockSpec((1, Q_BLOCK, QA), lambda i, j: (i, j, 0)),
            out_shape=jax.ShapeDtypeStruct((b, key_step, QA), F32),
            compiler_params=_cparams("parallel", "parallel"), name=f"nsa_prompt_{n_keys}",
        )(q, gates, ck, cv, ks, vs, kw, vw))
    return jnp.concatenate(outs, axis=1)


def _gather_pages_t(pt_ref, b, pool_ref, dst_ref, sem, n_pages):
    return [pltpu.make_async_copy(pool_ref.at[pt_ref[b, j]], dst_ref.at[:, :, pl.ds(j * PAGE, PAGE)], sem)
            for j in range(n_pages)]


def _nsa_sample_kernel(pt_ref, q_ref, gt_ref, ck_ref, cv_ref, ksn_ref, vsn_ref, kwn_ref, vwn_ref,
                       pks_ref, pvs_ref, wk_ref, wv_ref, o_ref, owk_ref, owv_ref, kbuf, vbuf, sem,
                       *, n_pages):
    b = pl.program_id(0)
    hpg = HEADS_PER_GROUP
    past = n_pages * PAGE
    n_cmp = ck_ref.shape[1]
    n_sel_past = past // SEL_LEN
    n_sel = n_sel_past + 1
    wb = wk_ref.shape[3]
    sel_lanes = 2 * LANES
    slot = lax.rem(b, 2)

    def page_copies(row_b, to_slot):
        return (_gather_pages_t(pt_ref, row_b, pks_ref, kbuf.at[to_slot], sem.at[0, to_slot], n_pages)
                + _gather_pages_t(pt_ref, row_b, pvs_ref, vbuf.at[to_slot], sem.at[1, to_slot], n_pages))

    @pl.when(b == 0)
    def _():
        for cp in page_copies(b, slot):
            cp.start()

    @pl.when(b + 1 < pl.num_programs(0))
    def _():
        for cp in page_copies(b + 1, 1 - slot):
            cp.start()

    row = lax.broadcasted_iota(jnp.int32, (8, 1), 0)
    n = HEAD_DIM
    eye = (lax.broadcasted_iota(jnp.int32, (n, n), 0) == lax.broadcasted_iota(jnp.int32, (n, n), 1)).astype(F32)
    gt = gt_ref[0]
    c_end = lax.broadcasted_iota(jnp.int32, (1, n_cmp), 1) * CMP_STRIDE + (CMP_LEN - 1)
    dist_c = past - c_end
    blk = lax.broadcasted_iota(jnp.int32, (1, sel_lanes), 1)
    cur = past // SEL_LEN
    forced = (blk == 0) | (blk == cur) | (blk == cur - 1)
    valid = blk * SEL_LEN <= past
    c2s = _cmp_to_sel(n_cmp, sel_lanes)

    def q_rows(g):
        q4 = jnp.concatenate(
            [q_ref[0, :, (g * hpg + j) * HEAD_DIM:(g * hpg + j + 1) * HEAD_DIM] for j in range(hpg)]
            + [jnp.zeros((8 - hpg, HEAD_DIM), F32)], axis=0) * (HEAD_DIM ** -0.5)
        slope = jnp.full((8, 1), _alibi_slope(g * hpg + hpg - 1), F32)
        for j in range(hpg - 2, -1, -1):
            slope = jnp.where(row < j + 1, _alibi_slope(g * hpg + j), slope)
        return q4, slope

    o_cs, imps = [], []
    for g in range(KV_GROUPS):
        lanes = slice(g * HEAD_DIM, (g + 1) * HEAD_DIM)
        q4, slope = q_rows(g)
        p_c = _masked_softmax(_dot_nt(q4, ck_ref[0, :, lanes]) - slope * dist_c.astype(F32), dist_c >= 0)
        o_cs.append(_dot(p_c, cv_ref[0, :, lanes]))
        imps.append(jnp.sum(jnp.where(row < hpg, _dot(p_c, c2s), 0.0), axis=0, keepdims=True))
    imp = jnp.concatenate(imps + [jnp.zeros((8 - KV_GROUPS, sel_lanes), F32)], axis=0)
    score = jnp.where(valid, jnp.where(forced, BIG, imp), -BIG)
    score = jnp.where(blk < n_sel, score, -jnp.inf)
    sel = _topk_mask(score, min(SEL_TOPK, n_sel), n_sel)
    key_sel = _dot(sel[:, 0:n_sel_past], _block_expand(n_sel_past, past))
    sel_new = sel[:, n_sel_past:n_sel_past + 1]

    pos_s = lax.broadcasted_iota(jnp.int32, (1, past), 1)
    dist_s = (past - pos_s).astype(F32)
    lane_w = lax.broadcasted_iota(jnp.int32, (1, wb), 1)
    pos_w = past - wb + lane_w
    dist_w = past - pos_w
    mask_w = (dist_w >= 0) & (dist_w <= WINDOW) & (pos_w >= 0)
    for cp in page_copies(b, slot):
        cp.wait()
    for g in range(KV_GROUPS):
        lanes = slice(g * HEAD_DIM, (g + 1) * HEAD_DIM)
        q4, slope = q_rows(g)
        q4r = _bf16_round(q4)
        s_p = jnp.where(key_sel[g:g + 1] > 0.5, _dot(q4, kbuf[slot, g]) - slope * dist_s, NEG)
        new_ok = sel_new[g:g + 1] > 0.5
        s_n = jnp.where(new_ok, jnp.sum(q4r * _bf16_round(ksn_ref[0, :, lanes]), axis=-1, keepdims=True), NEG)
        m = jnp.maximum(jnp.max(s_p, axis=-1, keepdims=True), s_n)
        e_p = jnp.where(key_sel[g:g + 1] > 0.5, jnp.exp(s_p - m), 0.0)
        e_n = jnp.where(new_ok, jnp.exp(s_n - m), 0.0)
        den = jnp.maximum(jnp.sum(e_p, axis=-1, keepdims=True) + e_n, 1e-30)
        o_s = _dot_nt(e_p / den, vbuf[slot, g]) + _bf16_round(e_n / den) * _bf16_round(vsn_ref[0, :, lanes])
        s_p = jnp.where(mask_w, _dot(q4, wk_ref[0, g]) - slope * dist_w.astype(F32), NEG)
        s_n = jnp.sum(q4r * _bf16_round(kwn_ref[0, :, lanes]), axis=-1, keepdims=True)
        m = jnp.maximum(jnp.max(s_p, axis=-1, keepdims=True), s_n)
        e_p = jnp.where(mask_w, jnp.exp(s_p - m), 0.0)
        e_n = jnp.exp(s_n - m)
        den = jnp.maximum(jnp.sum(e_p, axis=-1, keepdims=True) + e_n, 1e-30)
        o_w = _dot_nt(e_p / den, wv_ref[0, g]) + _bf16_round(e_n / den) * _bf16_round(vwn_ref[0, :, lanes])
        for j in range(hpg):
            h = g * hpg + j
            o_ref[0, :, h * HEAD_DIM:(h + 1) * HEAD_DIM] = (
                gt[:, 3 * h:3 * h + 1] * o_cs[g][j:j + 1] + gt[:, 3 * h + 1:3 * h + 2] * o_s[j:j + 1]
                + gt[:, 3 * h + 2:3 * h + 3] * o_w[j:j + 1])
        k_col = jnp.sum(eye * kwn_ref[0, :, lanes], axis=1, keepdims=True)
        v_col = jnp.sum(eye * vwn_ref[0, :, lanes], axis=1, keepdims=True)
        owk_ref[0, g] = jnp.where(lane_w == wb - 1, k_col, pltpu.roll(wk_ref[0, g], wb - 1, 1))
        owv_ref[0, g] = jnp.where(lane_w == wb - 1, v_col, pltpu.roll(wv_ref[0, g], wb - 1, 1))


def _nsa_sample(page_table, q, gates, ck, cv, ks_new, vs_new, kw_new, vw_new, pool_ks, pool_vs, win_k, win_v):
    bd, n_pages = page_table.shape
    n_cmp = ck.shape[1]
    wb = win_k.shape[1]
    one = lambda w: pl.BlockSpec((1, 1, w), lambda i, pt: (i, 0, 0))
    rows = lambda m: pl.BlockSpec((1, m, KVW), lambda i, pt: (i, 0, 0))
    win = pl.BlockSpec((1, KV_GROUPS, HEAD_DIM, wb), lambda i, pt: (i, 0, 0, 0))
    any_spec = pl.BlockSpec(memory_space=pl.ANY)
    grid_spec = pltpu.PrefetchScalarGridSpec(
        num_scalar_prefetch=1, grid=(bd,),
        in_specs=[one(QA), one(LANES), rows(n_cmp), rows(n_cmp), one(KVW), one(KVW), one(KVW), one(KVW),
                  any_spec, any_spec, win, win],
        out_specs=[one(QA), win, win],
        scratch_shapes=[pltpu.VMEM((2, KV_GROUPS, HEAD_DIM, n_pages * PAGE), F32),
                        pltpu.VMEM((2, KV_GROUPS, HEAD_DIM, n_pages * PAGE), F32),
                        pltpu.SemaphoreType.DMA((2, 2))])
    r3 = lambda z: z.reshape(bd, 1, -1)
    pos_minor = lambda z: jnp.transpose(z, (0, 2, 3, 1))
    win_shape = jax.ShapeDtypeStruct((bd, KV_GROUPS, HEAD_DIM, wb), F32)
    mix, nwk, nwv = pl.pallas_call(
        functools.partial(_nsa_sample_kernel, n_pages=n_pages),
        grid_spec=grid_spec,
        out_shape=[jax.ShapeDtypeStruct((bd, 1, QA), F32), win_shape, win_shape],
        compiler_params=_cparams("arbitrary"), name="nsa_sample",
    )(page_table, r3(q), r3(gates), ck, cv, r3(ks_new), r3(vs_new), r3(kw_new), r3(vw_new),
      pos_minor(pool_ks), pos_minor(pool_vs), pos_minor(win_k), pos_minor(win_v))
    back = lambda z: jnp.transpose(z, (0, 3, 1, 2))
    return mix, back(nwk), back(nwv)


def _mem_attn_kernel(q_ref, k_ref, v_ref, gain_ref, o_ref):
    tq = q_ref.shape[1]
    pad = max(8 - tq, 0)
    for h in range(MEM_HEADS):
        lanes = slice(h * HEAD_DIM, (h + 1) * HEAD_DIM)
        q = _rms(q_ref[0, :, lanes], gain_ref[...]) * (HEAD_DIM ** -0.5)
        if pad:
            q = jnp.concatenate([q, jnp.zeros((pad, HEAD_DIM), F32)], axis=0)
        s = _dot_nt(q, k_ref[0, :, lanes])
        e = jnp.exp(s - jnp.max(s, axis=-1, keepdims=True))
        p = e / jnp.sum(e, axis=-1, keepdims=True)
        o_ref[0, :, lanes] = _dot(p, v_ref[0, :, lanes])[0:tq]


def _mem_attn(mq, km, vm, gain, tq):
    b, t, _ = mq.shape
    m = km.shape[1]
    tq = min(tq, t)
    qspec = pl.BlockSpec((1, tq, MEMQ), lambda i, j: (i, j, 0))
    kspec = pl.BlockSpec((1, m, MEMQ), lambda i, j: (i, 0, 0))
    return pl.pallas_call(
        _mem_attn_kernel, grid=(b, t // tq),
        in_specs=[qspec, kspec, kspec, pl.BlockSpec((1, HEAD_DIM), lambda i, j: (0, 0))],
        out_specs=qspec, out_shape=jax.ShapeDtypeStruct((b, t, MEMQ), F32),
        compiler_params=_cparams("parallel", "parallel"), name="mem_attn",
    )(mq, km, vm, gain.reshape(1, HEAD_DIM))


def _outproj_router_kernel(x_ref, mix_ref, mem_ref, wo_ref, g_ref, wc_ref, bc_ref, wf_ref, bf_ref,
                           xo_ref, hn_ref, cw_ref):
    x = x_ref[...] + (_dot(mix_ref[...], wo_ref[0:QA, :]) + _dot(mem_ref[...], wo_ref[QA:QA + MEMQ, :]))
    xo_ref[...] = x
    hn = _rms(x, g_ref[...])
    hn_ref[...] = hn
    hn = hn.astype(BF16)
    lane = lax.broadcasted_iota(jnp.int32, (1, LANES), 1).astype(F32)
    lg = jnp.where(lane < N_GROUPS, _dot(hn, wc_ref[...]) + bc_ref[...], -jnp.inf)
    m = jnp.max(lg, axis=-1, keepdims=True)
    grp = jnp.min(jnp.where(lg == m, lane, 1e9), axis=-1, keepdims=True)
    p_grp = 1.0 / jnp.sum(jnp.exp(lg - m), axis=-1, keepdims=True)
    in_grp = (lane >= grp * E_PER_GROUP) & (lane < (grp + 1.0) * E_PER_GROUP)
    lf = jnp.where(in_grp, _dot(hn, wf_ref[...]) + bf_ref[...], -jnp.inf)
    v1 = jnp.max(lf, axis=-1, keepdims=True)
    i1 = jnp.min(jnp.where(lf == v1, lane, 1e9), axis=-1, keepdims=True)
    lf2 = jnp.where(lane == i1, -jnp.inf, lf)
    v2 = jnp.max(lf2, axis=-1, keepdims=True)
    i2 = jnp.min(jnp.where(lf2 == v2, lane, 1e9), axis=-1, keepdims=True)
    e2 = jnp.exp(v2 - v1)
    den = 1.0 + e2
    cw_ref[...] = jnp.where(lane == i1, p_grp / den, 0.0) + jnp.where(lane == i2, p_grp * (e2 / den), 0.0)


def _outproj_router(x, mix, mem, w_out, g_ffn, w_coarse, b_coarse, w_fine, b_fine, tm):
    n, d = x.shape
    tm = min(tm, n)
    pad_w = lambda w: jnp.pad(w, ((0, 0), (0, LANES - w.shape[1])))
    pad_b = lambda v: jnp.pad(v, (0, LANES - v.shape[0])).reshape(1, LANES)
    row = lambda w: pl.BlockSpec((tm, w), lambda i: (i, 0))
    const = lambda r, c: pl.BlockSpec((r, c), lambda i: (0, 0))
    return pl.pallas_call(
        _outproj_router_kernel, grid=(n // tm,),
        in_specs=[row(d), row(QA), row(MEMQ), const(QA + MEMQ, d), const(1, d), const(d, LANES),
                  const(1, LANES), const(d, LANES), const(1, LANES)],
        out_specs=[row(d), row(d), row(LANES)],
        out_shape=[jax.ShapeDtypeStruct((n, d), F32), jax.ShapeDtypeStruct((n, d), F32),
                   jax.ShapeDtypeStruct((n, LANES), F32)],
        compiler_params=_cparams("parallel"), name="outproj_router",
    )(x, mix, mem, w_out.astype(BF16), g_ffn.reshape(1, d), pad_w(w_coarse).astype(BF16), pad_b(b_coarse),
      pad_w(w_fine).astype(BF16), pad_b(b_fine))


def _moe_kernel(x_ref, hn_ref, cw_ref, wg_ref, wu_ref, wd_ref, o_ref):
    e = pl.program_id(1)

    @pl.when(e == 0)
    def _():
        o_ref[...] = jnp.zeros_like(o_ref)

    hb = hn_ref[...].astype(BF16)
    gate = _dot(hb, wg_ref[0])
    up = _dot(hb, wu_ref[0])
    lane = lax.broadcasted_iota(jnp.int32, (1, LANES), 1)
    c = jnp.sum(jnp.where(lane == e, cw_ref[...], 0.0), axis=-1, keepdims=True)
    o_ref[...] += _dot(gate * jax.nn.sigmoid(gate) * up, wd_ref[0]) * c

    @pl.when(e == pl.num_programs(1) - 1)
    def _():
        o_ref[...] = x_ref[...] + o_ref[...]


def _moe(x, hn, cw, w_gate, w_up, w_down, layer, tm):
    n, d = x.shape
    tm = min(tm, n)
    _, n_exp, _, d_exp = w_gate.shape
    row = lambda w: pl.BlockSpec((tm, w), lambda i, e: (i, 0))
    return pl.pallas_call(
        _moe_kernel, grid=(n // tm, n_exp),
        in_specs=[row(d), row(d), row(LANES),
                  pl.BlockSpec((None, 1, d, d_exp), lambda i, e: (layer, e, 0, 0)),
                  pl.BlockSpec((None, 1, d, d_exp), lambda i, e: (layer, e, 0, 0)),
                  pl.BlockSpec((None, 1, d_exp, d), lambda i, e: (layer, e, 0, 0))],
        out_specs=row(d), out_shape=jax.ShapeDtypeStruct((n, d), F32),
        compiler_params=_cparams("parallel", "arbitrary"), name="moe",
    )(x, hn, cw, w_gate, w_up, w_down)


def _moe_plan_kernel(cw_ref, posa_ref, posb_ref, te_ref, nu_ref, cnt_ref, run_ref, start_ref, *, row_tile, n_rows):
    phase, i = pl.program_id(0), pl.program_id(1)
    tm = cw_ref.shape[0]
    cwt = cw_ref[...].T
    mask = cwt != 0.0
    maskf = mask.astype(F32)
    tile_cnt = jnp.sum(maskf, axis=1, keepdims=True)

    @pl.when((phase == 0) & (i == 0))
    def _():
        cnt_ref[...] = jnp.zeros_like(cnt_ref)

    @pl.when(phase == 0)
    def _():
        cnt_ref[...] += tile_cnt

    @pl.when((phase == 1) & (i == 0))
    def _():
        padded = jnp.floor((cnt_ref[...] + (row_tile - 1)) * (1.0 / row_tile)) * row_tile
        lower = (lax.broadcasted_iota(jnp.int32, (LANES, LANES), 0)
                 > lax.broadcasted_iota(jnp.int32, (LANES, LANES), 1)).astype(F32)
        start = _dot(lower, jnp.broadcast_to(padded, (LANES, LANES)), exact=True)[:, 0:1]
        start_ref[...] = start
        run_ref[...] = jnp.zeros_like(run_ref)
        tile_lo = lax.broadcasted_iota(jnp.int32, (1, 2 * LANES), 1).astype(F32) * row_tile
        n_done = jnp.sum(jnp.where(start + padded <= tile_lo, 1.0, 0.0), axis=0, keepdims=True)
        te_ref[...] = jnp.minimum(n_done, N_EXPERTS - 1.0).astype(jnp.int32)
        total = jnp.sum(padded, axis=0, keepdims=True)
        nu_ref[...] = jnp.broadcast_to(total * (1.0 / row_tile), (1, LANES)).astype(jnp.int32)

    @pl.when(phase == 1)
    def _():
        before = (lax.broadcasted_iota(jnp.int32, (tm, tm), 0)
                  < lax.broadcasted_iota(jnp.int32, (tm, tm), 1)).astype(BF16)
        pos = start_ref[...] + run_ref[...] + _dot(maskf, before)
        posa = jnp.min(jnp.where(mask, pos, 3e38), axis=0, keepdims=True)
        posb = jnp.max(jnp.where(mask, pos, -1.0), axis=0, keepdims=True)
        posb = jnp.where(jnp.sum(maskf, axis=0, keepdims=True) > 1.5, posb, n_rows - 1.0)
        posa_ref[0] = posa.astype(jnp.int32)
        posb_ref[0] = posb.astype(jnp.int32)
        run_ref[...] += tile_cnt


def _moe_row_copies(pos_refs, make_copy, tm):
    def issue(t, carry):
        for pos_ref in pos_refs:
            make_copy(pos_ref, t).start()
        return carry

    def drain(t, carry):
        for pos_ref in pos_refs:
            make_copy(pos_ref, 0).wait()
        return carry

    lax.fori_loop(0, tm, issue, 0, unroll=8)
    lax.fori_loop(0, tm, drain, 0, unroll=8)


def _moe_dispatch_kernel(posa_ref, posb_ref, hn_ref, xs_zero_ref, xs_ref, sem):
    del xs_zero_ref

    def make_copy(pos_ref, t):
        return pltpu.make_async_copy(hn_ref.at[pl.ds(t, 1)], xs_ref.at[pl.ds(pos_ref[0, 0, t], 1)], sem)

    _moe_row_copies((posa_ref, posb_ref), make_copy, hn_ref.shape[0])


def _moe_experts_kernel(te_ref, nu_ref, xs_ref, wg_ref, wu_ref, wd_ref, ys_ref):
    del te_ref
    used = pl.program_id(0) < nu_ref[0]

    @pl.when(used)
    def _():
        hb = xs_ref[...].astype(BF16)
        gate = _dot(hb, wg_ref[0])
        up = _dot(hb, wu_ref[0])
        ys_ref[...] = _dot(gate * jax.nn.sigmoid(gate) * up, wd_ref[0])

    @pl.when(jnp.logical_not(used))
    def _():
        ys_ref[...] = jnp.zeros_like(ys_ref)


def _moe_combine_kernel(posa_ref, posb_ref, x_ref, cw_ref, ys_ref, o_ref, ya_ref, yb_ref, sem):
    def make_copy(pos_ref, t):
        dst = ya_ref if pos_ref is posa_ref else yb_ref
        return pltpu.make_async_copy(ys_ref.at[pl.ds(pos_ref[0, 0, t], 1)], dst.at[pl.ds(t, 1)], sem)

    _moe_row_copies((posa_ref, posb_ref), make_copy, x_ref.shape[0])
    cw = cw_ref[...]
    lane = lax.broadcasted_iota(jnp.int32, (1, LANES), 1).astype(F32)
    routed = cw != 0.0
    ea = jnp.min(jnp.where(routed, lane, 1e9), axis=-1, keepdims=True)
    eb = jnp.max(jnp.where(routed, lane, -1.0), axis=-1, keepdims=True)
    wa = jnp.sum(jnp.where(lane == ea, cw, 0.0), axis=-1, keepdims=True)
    wb = jnp.sum(jnp.where((lane == eb) & (eb != ea), cw, 0.0), axis=-1, keepdims=True)
    o_ref[...] = x_ref[...] + (wa * ya_ref[...] + wb * yb_ref[...])


def _moe_routed(x, hn, cw, w_gate, w_up, w_down, layer, rows_buf=None):
    n, d = x.shape
    _, n_exp, _, d_exp = w_gate.shape
    tm, tr = MOE_TOKEN_TILE, MOE_ROW_TILE
    n_tiles = n // tm
    n_row_tiles = (2 * n + n_exp * (tr - 1)) // tr + 1
    n_rows = n_row_tiles * tr
    assert n % tm == 0 and n_row_tiles <= 2 * LANES
    pos_shape = jax.ShapeDtypeStruct((n_tiles, 1, tm), jnp.int32)
    posa, posb, tile_expert, n_used = pl.pallas_call(
        functools.partial(_moe_plan_kernel, row_tile=tr, n_rows=n_rows),
        grid=(2, n_tiles),
        in_specs=[pl.BlockSpec((tm, LANES), lambda p, i: (i, 0))],
        out_specs=[pl.BlockSpec((1, 1, tm), lambda p, i: (i * p, 0, 0)),
                   pl.BlockSpec((1, 1, tm), lambda p, i: (i * p, 0, 0)),
                   pl.BlockSpec((1, 2 * LANES), lambda p, i: (0, 0)),
                   pl.BlockSpec((1, LANES), lambda p, i: (0, 0))],
        out_shape=[pos_shape, pos_shape, jax.ShapeDtypeStruct((1, 2 * LANES), jnp.int32),
                   jax.ShapeDtypeStruct((1, LANES), jnp.int32)],
        scratch_shapes=[pltpu.VMEM((LANES, 1), F32)] * 3,
        compiler_params=_cparams("arbitrary", "arbitrary"), name="moe_plan",
    )(cw)
    pos_spec = pl.BlockSpec((1, 1, tm), lambda i: (i, 0, 0), memory_space=pltpu.SMEM)
    any_spec = pl.BlockSpec(memory_space=pl.ANY)
    row = lambda w: pl.BlockSpec((tm, w), lambda i: (i, 0))
    xs = pl.pallas_call(
        _moe_dispatch_kernel, grid=(n_tiles,),
        in_specs=[pos_spec, pos_spec, row(d), any_spec],
        out_specs=any_spec, out_shape=jax.ShapeDtypeStruct((n_rows, d), F32),
        scratch_shapes=[pltpu.SemaphoreType.DMA(())],
        input_output_aliases={3: 0},
        compiler_params=_cparams("arbitrary"), name="moe_dispatch",
    )(posa, posb, hn, jnp.zeros((n_rows, d), F32) if rows_buf is None else rows_buf)
    wspec = lambda a, c: pl.BlockSpec((None, 1, a, c), lambda i, te, nu: (layer, te[i], 0, 0))
    ys = pl.pallas_call(
        _moe_experts_kernel,
        grid_spec=pltpu.PrefetchScalarGridSpec(
            num_scalar_prefetch=2, grid=(n_row_tiles,),
            in_specs=[pl.BlockSpec((tr, d), lambda i, te, nu: (i, 0)), wspec(d, d_exp), wspec(d, d_exp),
                      wspec(d_exp, d)],
            out_specs=pl.BlockSpec((tr, d), lambda i, te, nu: (i, 0))),
        out_shape=jax.ShapeDtypeStruct((n_rows, d), F32),
        compiler_params=_cparams("arbitrary"), name="moe_experts",
    )(tile_expert.reshape(-1), n_used.reshape(-1), xs, w_gate, w_up, w_down)
    out = pl.pallas_call(
        _moe_combine_kernel, grid=(n_tiles,),
        in_specs=[pos_spec, pos_spec, row(d), row(LANES), any_spec],
        out_specs=row(d), out_shape=jax.ShapeDtypeStruct((n, d), F32),
        scratch_shapes=[pltpu.VMEM((tm, d), F32), pltpu.VMEM((tm, d), F32), pltpu.SemaphoreType.DMA(())],
        compiler_params=_cparams("arbitrary"), name="moe_combine",
    )(posa, posb, x, cw, ys)
    return out, xs


def _rwkv_prep_body(p, prev, mu_ref, w0_ref, wup_ref, a0_ref, aup_ref, gup_ref, kk_ref, ka_ref, outs):
    r_ref, ld_ref, k_ref, v_ref, kkn_ref, a_ref, g_ref = outs
    x = p + mu_ref[...] * (prev - p)
    c0, c1, c2, c3, c4 = 768, 1536, 2304, 2368, 2432
    r, k, v = x[:, 0:c0], x[:, c0:c1], x[:, c1:c2]
    xw, xa, xg = x[:, c2:c3], x[:, c3:c4], x[:, c4:SHIFT_W]
    z = -(w0_ref[...] + _dot(jnp.tanh(xw), wup_ref[...]))
    softplus = jnp.maximum(z, 0.0) + jnp.log(1.0 + jnp.exp(-jnp.abs(z)))
    w = -softplus - 0.5
    a = jax.nn.sigmoid(a0_ref[...] + _dot(xa, aup_ref[...]))
    r_ref[...] = r
    ld_ref[...] = -jnp.exp(w)
    k_ref[...] = k * (1.0 + (a - 1.0) * ka_ref[...])
    v_ref[...] = v
    a_ref[...] = a
    g_ref[...] = _dot(jax.nn.sigmoid(xg), gup_ref[...])
    kk = k * kk_ref[...]
    for h in range(TOK_HEADS):
        lanes = slice(h * HEAD_DIM, (h + 1) * HEAD_DIM)
        seg = kk[:, lanes]
        nrm = jnp.sqrt(jnp.sum(seg * seg, axis=-1, keepdims=True))
        kkn_ref[:, lanes] = seg / jnp.maximum(nrm, 1e-12)


def _rwkv_prep_prompt_kernel(p_ref, mu_ref, w0_ref, wup_ref, a0_ref, aup_ref, gup_ref, kk_ref, ka_ref,
                             *rest):
    outs, carry = rest[:7], rest[7]
    tm = p_ref.shape[1]

    @pl.when(pl.program_id(1) == 0)
    def _():
        carry[...] = jnp.zeros_like(carry)

    p = p_ref[0]
    first = lax.broadcasted_iota(jnp.int32, (tm, 1), 0) == 0
    prev = jnp.where(first, carry[...], pltpu.roll(p, 1, 0))
    carry[...] = p[tm - 1:tm]
    _rwkv_prep_body(p, prev, mu_ref, w0_ref, wup_ref, a0_ref, aup_ref, gup_ref, kk_ref, ka_ref,
                    [o.at[0] for o in outs])


def _rwkv_prep_sample_kernel(p_ref, prev_ref, mu_ref, w0_ref, wup_ref, a0_ref, aup_ref, gup_ref, kk_ref,
                             ka_ref, *outs):
    _rwkv_prep_body(p_ref[...], prev_ref[...], mu_ref, w0_ref, wup_ref, a0_ref, aup_ref, gup_ref, kk_ref,
                    ka_ref, outs)


def _rwkv_weight_args(mu, w0, w_up, a0, a_up, g_up, k_k, k_a):
    row = lambda v: v.reshape(1, -1)
    return (row(mu), row(w0), w_up, row(a0), a_up, g_up, row(k_k), row(k_a))


def _rwkv_prep_prompt(pr, wargs, tm):
    b, s, _ = pr.shape
    tm = min(tm, s)
    const = lambda a: pl.BlockSpec(a.shape, lambda i, j: (0, 0))
    ospec = pl.BlockSpec((1, tm, QA), lambda i, j: (i, j, 0))
    return pl.pallas_call(
        _rwkv_prep_prompt_kernel, grid=(b, s // tm),
        in_specs=[pl.BlockSpec((1, tm, SHIFT_W), lambda i, j: (i, j, 0))] + [const(a) for a in wargs],
        out_specs=[ospec] * 7, out_shape=[jax.ShapeDtypeStruct((b, s, QA), F32)] * 7,
        scratch_shapes=[pltpu.VMEM((1, SHIFT_W), F32)],
        compiler_params=_cparams("parallel", "arbitrary"), name="rwkv_prep_prompt",
    )(pr, *wargs)


def _rwkv_prep_sample(pr, prev, wargs):
    n = pr.shape[0]
    full = lambda a: pl.BlockSpec(a.shape, lambda i: (0, 0))
    return pl.pallas_call(
        _rwkv_prep_sample_kernel, grid=(1,),
        in_specs=[full(pr), full(prev)] + [full(a) for a in wargs],
        out_specs=[pl.BlockSpec((n, QA), lambda i: (0, 0))] * 7,
        out_shape=[jax.ShapeDtypeStruct((n, QA), F32)] * 7,
        compiler_params=_cparams("arbitrary"), name="rwkv_prep_sample",
    )(pr, prev, *wargs)


def _rwkv_finish(y, r, k, v, g, rk, lnw, lnb):
    m = jnp.mean(y, axis=-1, keepdims=True)
    var = jnp.mean(jnp.square(y - m), axis=-1, keepdims=True)
    yn = (y - m) * lax.rsqrt(var + RWKV_GN_EPS) * lnw + lnb
    bonus = jnp.sum(r * k * rk, axis=-1, keepdims=True) * v
    return (yn + bonus) * g


def _rwkv_chunk_kernel(r_ref, ld_ref, k_ref, v_ref, kk_ref, a_ref, g_ref, rk_ref, lnw_ref, lnb_ref,
                       o_ref, st_ref, z_ref):
    c = r_ref.shape[1]
    ci = pl.program_id(1)

    @pl.when(ci == 0)
    def _():
        z_ref[...] = jnp.zeros_like(z_ref)

    ri = lax.broadcasted_iota(jnp.int32, (c, c), 0)
    cj = lax.broadcasted_iota(jnp.int32, (c, c), 1)
    incl = ri >= cj
    strict = ri > cj
    n = HEAD_DIM
    eye = (lax.broadcasted_iota(jnp.int32, (n, n), 0) == lax.broadcasted_iota(jnp.int32, (n, n), 1)).astype(F32)
    ex = RWKV_CHUNK_EXACT
    hl = [slice(h * n, (h + 1) * n) for h in range(TOK_HEADS)]
    prep = []
    for bi in range(r_ref.shape[0]):
        ld, r_all, k_all, kk_all = ld_ref[bi], r_ref[bi], k_ref[bi], kk_ref[bi]
        lc = _dot(incl.astype(F32), ld, exact=True)
        l_end = lc[c - 1:c]
        b_all = kk_all * a_ref[bi]
        e_neg = jnp.exp(-lc)
        e_rem = jnp.exp(l_end - lc)
        prep.append(dict(
            r=r_all, k=k_all, v=v_ref[bi],
            at=-kk_all * jnp.exp(lc - ld),
            rt=r_all * jnp.exp(lc),
            bt=b_all * e_neg, kt=k_all * e_neg,
            bh=b_all * e_rem, kh=k_all * e_rem,
            p_end=jnp.exp(l_end)))
    units = [(bi, h) for bi in range(r_ref.shape[0]) for h in range(TOK_HEADS)]
    col = lambda name, u: prep[u[0]][name][:, hl[u[1]]]
    v = [col("v", u) for u in units]
    big = [_dot_nt(jnp.concatenate([col("at", u), col("rt", u)], axis=0),
                   jnp.concatenate([col("bt", u), col("kt", u)], axis=0), ex) for u in units]
    a_rbk = [jnp.concatenate([jnp.where(incl, m[c:2 * c, 0:c], 0.0), jnp.where(incl, m[c:2 * c, c:2 * c], 0.0)],
                             axis=1) for m in big]
    akv = [_dot(jnp.where(strict, m[0:c, c:2 * c], 0.0), vv, ex) for m, vv in zip(big, v)]
    x = [jnp.concatenate([col("at", u), t], axis=1) for u, t in zip(units, akv)]
    npow = [jnp.where(strict, m[0:c, 0:c], 0.0) for m in big]
    x = [xx + _dot(m, xx, ex) for m, xx in zip(npow, x)]
    steps = 1
    while 2 * steps < c:
        npow = [_dot(m, m, ex) for m in npow]
        x = [xx + _dot(m, xx, ex) for m, xx in zip(npow, x)]
        steps *= 2
    zeros = jnp.zeros((c, n), F32)
    wv = [jnp.concatenate([xx, jnp.concatenate([zeros, vv], axis=1)], axis=0) for xx, vv in zip(x, v)]
    rq_y0 = [_dot(m, w, ex) for m, w in zip(a_rbk, wv)]
    m_n = [_dot_tn(jnp.concatenate([col("bh", u), col("kh", u)], axis=0), w, ex)
           for u, w in zip(units, wv)]
    yz = [_dot(jnp.concatenate([col("rt", u) + rq_y0[i][:, 0:n],
                                m_n[i][:, 0:n] + eye * col("p_end", u)], axis=0), z_ref[u[0], u[1]], ex)
          for i, u in enumerate(units)]
    for i, (bi, h) in enumerate(units):
        z_ref[bi, h] = yz[i][c:c + n] + m_n[i][:, n:2 * n]
        y = yz[i][0:c] + rq_y0[i][:, n:2 * n]
        o_ref[bi, :, hl[h]] = _rwkv_finish(y, col("r", (bi, h)), col("k", (bi, h)), v[i], g_ref[bi, :, hl[h]],
                                           rk_ref[:, hl[h]], lnw_ref[:, hl[h]], lnb_ref[:, hl[h]])

    @pl.when(ci == pl.num_programs(1) - 1)
    def _():
        for bi, h in units:
            st_ref[bi, h] = z_ref[bi, h].T


def _rwkv_chunk(r, ld, k, v, kk, a, g, r_k, ln_w, ln_b):
    b, s, _ = r.shape
    c = RWKV_CHUNK
    nb = RWKV_ROWS_PER_STEP if b % RWKV_ROWS_PER_STEP == 0 else 1
    tok = pl.BlockSpec((nb, c, QA), lambda i, j: (i, j, 0))
    const = pl.BlockSpec((1, QA), lambda i, j: (0, 0))
    return pl.pallas_call(
        _rwkv_chunk_kernel, grid=(b // nb, s // c),
        in_specs=[tok] * 7 + [const] * 3,
        out_specs=[tok, pl.BlockSpec((nb, TOK_HEADS, HEAD_DIM, HEAD_DIM), lambda i, j: (i, 0, 0, 0))],
        out_shape=[jax.ShapeDtypeStruct((b, s, QA), F32),
                   jax.ShapeDtypeStruct((b, TOK_HEADS, HEAD_DIM, HEAD_DIM), F32)],
        scratch_shapes=[pltpu.VMEM((nb, TOK_HEADS, HEAD_DIM, HEAD_DIM), F32)],
        compiler_params=_cparams("parallel", "arbitrary"), name="rwkv_chunk",
    )(r, ld, k, v, kk, a, g, r_k.reshape(1, QA), ln_w.reshape(1, QA), ln_b.reshape(1, QA))


def _rwkv_step_kernel(r_ref, ld_ref, k_ref, v_ref, kk_ref, a_ref, g_ref, rk_ref, lnw_ref, lnb_ref, s_ref,
                      o_ref, so_ref):
    n = HEAD_DIM
    eye = (lax.broadcasted_iota(jnp.int32, (n, n), 0) == lax.broadcasted_iota(jnp.int32, (n, n), 1)).astype(F32)
    units = [(bi, slice(h * n, (h + 1) * n), h) for bi in range(r_ref.shape[0]) for h in range(TOK_HEADS)]
    row = lambda ref: [ref[bi, :, lanes] for bi, lanes, _ in units]
    r, k, v, kk, a, ld = row(r_ref), row(k_ref), row(v_ref), row(kk_ref), row(a_ref), row(ld_ref)
    s = [s_ref[bi, h] for bi, _, h in units]
    sa = [jnp.sum(_bf16_round(si) * _bf16_round(-kki), axis=1, keepdims=True) for si, kki in zip(s, kk)]
    v_col = [jnp.sum(eye * vi, axis=1, keepdims=True) for vi in v]
    s_new = [si * jnp.exp(ldi) + sai * (kki * ai) + vci * ki
             for si, ldi, sai, kki, ai, vci, ki in zip(s, ld, sa, kk, a, v_col, k)]
    y_col = [jnp.sum(_bf16_round(si) * _bf16_round(ri), axis=1, keepdims=True) for si, ri in zip(s_new, r)]
    y = [jnp.sum(eye * yc, axis=0, keepdims=True) for yc in y_col]
    for i, (bi, lanes, h) in enumerate(units):
        so_ref[bi, h] = s_new[i]
        o_ref[bi, :, lanes] = _rwkv_finish(y[i], r[i], k[i], v[i], g_ref[bi, :, lanes], rk_ref[:, lanes],
                                           lnw_ref[:, lanes], lnb_ref[:, lanes])


def _rwkv_step(r, ld, k, v, kk, a, g, r_k, ln_w, ln_b, state):
    n = r.shape[0]
    nb = RWKV_STEP_ROWS if n % RWKV_STEP_ROWS == 0 else 1
    tok = pl.BlockSpec((nb, 1, QA), lambda i: (i, 0, 0))
    const = pl.BlockSpec((1, QA), lambda i: (0, 0))
    st = pl.BlockSpec((nb, TOK_HEADS, HEAD_DIM, HEAD_DIM), lambda i: (i, 0, 0, 0))
    r3 = lambda z: z.reshape(n, 1, QA)
    return pl.pallas_call(
        _rwkv_step_kernel, grid=(n // nb,),
        in_specs=[tok] * 7 + [const] * 3 + [st],
        out_specs=[tok, st],
        out_shape=[jax.ShapeDtypeStruct((n, 1, QA), F32), jax.ShapeDtypeStruct(state.shape, F32)],
        compiler_params=_cparams("parallel"), name="rwkv_step",
    )(r3(r), r3(ld), r3(k), r3(v), r3(kk), r3(a), r3(g), r_k.reshape(1, QA), ln_w.reshape(1, QA),
      ln_b.reshape(1, QA), state)


def _split_w_in_a(w):
    offs = [0, QA]
    for _ in range(6):
        offs.append(offs[-1] + KVW)
    offs.append(offs[-1] + 3 * TOK_HEADS)
    offs.append(offs[-1] + MEMQ)
    pieces = [w[:, offs[i]:offs[i + 1]] for i in range(9)]
    pieces[7] = jnp.pad(pieces[7], ((0, 0), (0, LANES - 3 * TOK_HEADS)))
    return pieces


_A_OPS = ("hnorm", "none", "none", "hnorm", "none", "hnorm", "none", "sigmoid", "none")


def _ffn(x, mix, mem, w_out, g_ffn, wc, bc, wf, bf, w_gate, w_up, w_down, layer, tm_proj, tm_moe, rows_buf=None):
    x_new, hn, cw = _outproj_router(x, mix, mem, w_out, g_ffn, wc, bc, wf, bf, tm_proj)
    if x.shape[0] % MOE_TOKEN_TILE == 0:
        return _moe_routed(x_new, hn, cw, w_gate, w_up, w_down, layer, rows_buf)
    return _moe(x_new, hn, cw, w_gate, w_up, w_down, layer, tm_moe), None


def kernel(x_prompt, x_sample, cache_cmp_k, cache_cmp_v, cache_sel_k, cache_sel_v, cache_win_k, cache_win_v, cache_mem_k, cache_mem_v, state_rwkv, state_shift, page_table, mem_prompt, norm_mix, norm_ffn, norm_mem, w_mem_kv, mem_q_gain, mem_k_gain, w_in_a, nsa_q_gain, nsa_k_gain, cmp_pe, cmp_w1, cmp_b1, cmp_w2, w_in_b, rwkv_mu, rwkv_w0, rwkv_w_up, rwkv_a0, rwkv_a_up, rwkv_g_up, rwkv_k_k, rwkv_k_a, rwkv_r_k, rwkv_ln_w, rwkv_ln_b, w_out, moe_w_coarse, moe_b_coarse, moe_w_fine, moe_b_fine, moe_w_gate, moe_w_up, moe_w_down):
    b, s, d = x_prompt.shape
    bd = x_sample.shape[0]
    depth = norm_mix.shape[0]
    m_len = mem_prompt.shape[1]
    wl = min(WINDOW, s)
    xp = x_prompt.reshape(b * s, d)
    xs = x_sample.reshape(bd, d)
    mem2 = mem_prompt.reshape(b * m_len, d)
    outs = {name: [] for name in ("pc_k", "pc_v", "ps_k", "ps_v", "pw_k", "pw_v", "pm_k", "pm_v", "pr_s", "pr_x",
                                  "sc_k", "sc_v", "ss_k", "ss_v", "sw_k", "sw_v", "sr_s", "sr_x")}
    rows_buf = None
    for i in range(depth):
        km_p, vm_p = _norm_proj(mem2, norm_mem[i], [w_mem_kv[i][:, :MEMQ], w_mem_kv[i][:, MEMQ:]],
                                ("hnorm", "none"), [mem_k_gain[i]], 256)
        km_p, vm_p = km_p.reshape(b, m_len, MEMQ), vm_p.reshape(b, m_len, MEMQ)
        outs["pm_k"].append(km_p.reshape(b, m_len, MEM_HEADS, HEAD_DIM))
        outs["pm_v"].append(vm_p.reshape(b, m_len, MEM_HEADS, HEAD_DIM))
        if i % 2 == 0:
            ia = i // 2
            pieces = _split_w_in_a(w_in_a[ia])
            gains = [nsa_q_gain[ia], nsa_k_gain[ia, 1], nsa_k_gain[ia, 2]]
            cmp_args = (cmp_pe[ia], cmp_w1[ia], cmp_b1[ia], cmp_w2[ia], nsa_k_gain[ia, 0])
            q, kc, vc, ks, vs, kw, vw, gt, mq_p = _norm_proj(xp, norm_mix[i], pieces, _A_OPS, gains, PROJ_ROWS)
            r3 = lambda z: z.reshape(b, s, -1)
            kc, vc, ks, vs, kw, vw = (r3(z) for z in (kc, vc, ks, vs, kw, vw))
            ck, cv = _compress_prompt(kc, vc, *cmp_args)
            mix_p = _nsa_prompt(r3(q), r3(gt), ck, cv, ks, vs, kw, vw).reshape(b * s, QA)
            r5 = lambda z: z.reshape(b, -1, KV_GROUPS, HEAD_DIM)
            for name, z in (("pc_k", kc), ("pc_v", vc), ("ps_k", ks), ("ps_v", vs),
                            ("pw_k", kw[:, s - wl:]), ("pw_v", vw[:, s - wl:])):
                outs[name].append(r5(z))
            q, kc, vc, ks, vs, kw, vw, gt, mq_s = _norm_proj(xs, norm_mix[i], pieces, _A_OPS, gains, 256)
            ck, cv = _compress_sample(page_table, cache_cmp_k[ia], cache_cmp_v[ia], *cmp_args)
            mix_s, nwk, nwv = _nsa_sample(page_table, q, gt, ck, cv, ks, vs, kw, vw,
                                          cache_sel_k[ia], cache_sel_v[ia], cache_win_k[ia], cache_win_v[ia])
            mix_s = mix_s.reshape(bd, QA)
            r5 = lambda z: z.reshape(bd, -1, KV_GROUPS, HEAD_DIM)
            for name, z in (("sc_k", kc), ("sc_v", vc), ("ss_k", ks), ("ss_v", vs), ("sw_k", nwk), ("sw_v", nwv)):
                outs[name].append(r5(z))
        else:
            ib = i // 2
            pieces = [w_in_b[ib][:, :SHIFT_W], w_in_b[ib][:, SHIFT_W:]]
            wargs = _rwkv_weight_args(rwkv_mu[ib], rwkv_w0[ib], rwkv_w_up[ib], rwkv_a0[ib], rwkv_a_up[ib],
                                      rwkv_g_up[ib], rwkv_k_k[ib], rwkv_k_a[ib])
            fin = (rwkv_r_k[ib].reshape(-1), rwkv_ln_w[ib], rwkv_ln_b[ib])
            pr, mq_p = _norm_proj(xp, norm_mix[i], pieces, ("none", "none"), [], PROJ_ROWS)
            pr = pr.reshape(b, s, SHIFT_W)
            prep = _rwkv_prep_prompt(pr, wargs, 256)
            mix_p, st_p = _rwkv_chunk(*prep, *fin)
            mix_p = mix_p.reshape(b * s, QA)
            outs["pr_s"].append(st_p)
            outs["pr_x"].append(pr[:, s - 1])
            pr, mq_s = _norm_proj(xs, norm_mix[i], pieces, ("none", "none"), [], 256)
            prep = _rwkv_prep_sample(pr, state_shift[ib], wargs)
            mix_s, st_s = _rwkv_step(*prep, *fin, state_rwkv[ib])
            mix_s = mix_s.reshape(bd, QA)
            outs["sr_s"].append(st_s)
            outs["sr_x"].append(pr)
        mem_p = _mem_attn(mq_p.reshape(b, s, MEMQ), km_p, vm_p, mem_q_gain[i], MEM_Q_ROWS).reshape(b * s, MEMQ)
        mem_s = _mem_attn(mq_s.reshape(bd, 1, MEMQ), cache_mem_k[i].reshape(bd, -1, MEMQ),
                          cache_mem_v[i].reshape(bd, -1, MEMQ), mem_q_gain[i], 1).reshape(bd, MEMQ)
        ffn_w = (w_out[i], norm_ffn[i], moe_w_coarse[i], moe_b_coarse[i], moe_w_fine[i], moe_b_fine[i],
                 moe_w_gate, moe_w_up, moe_w_down, i)
        xp, rows_buf = _ffn(xp, mix_p, mem_p, *ffn_w, PROJ_ROWS, 1024, rows_buf)
        xs, _ = _ffn(xs, mix_s, mem_s, *ffn_w, PROJ_ROWS, 1024)
    order = ("pc_k", "pc_v", "ps_k", "ps_v", "pw_k", "pw_v", "pm_k", "pm_v", "pr_s", "pr_x",
             "sc_k", "sc_v", "ss_k", "ss_v", "sw_k", "sw_v", "sr_s", "sr_x")
    return (xp.reshape(b, s, d), xs.reshape(bd, 1, d)) + tuple(jnp.stack(outs[name]) for name in order)
```
